```python
import jax, jax.numpy as jnp
from jax import lax
import numpy as np

D_MODEL = 2048
BATCH = 4
SEQ = 2048
DEPTH = 2
DEC_BATCH = 8
DEC_SEQ = 32
PAST_LEN = 1024

CHUNK = 64
N_MIXERS = 2
N_A_LAYERS = (DEPTH + 1) // 2
N_B_LAYERS = DEPTH // 2
EXPAND = 2
GM_WIDTH = EXPAND * D_MODEL
GM_BLOCK = 128
GM_GROUPS = 16
GM_GROUP_DIM = GM_WIDTH // GM_GROUPS
SB_HEADS = 16
SB_HEAD_DIM = D_MODEL // SB_HEADS
SB_WIDTH = SB_HEADS * SB_HEAD_DIM
SB_Q_BLOCK = 128
NORM_EPS = 1e-6
LN_EPS = 1e-5

kernel_name = "stickbreak_gmlp_hybrid_stream_step"


def rms_norm(x, g):
    xf = x.astype(jnp.float32)
    y = xf * lax.rsqrt(jnp.mean(xf * xf, axis=-1, keepdims=True) + NORM_EPS)
    return (y * g.astype(jnp.float32)).astype(x.dtype)


def layer_norm(x, g, b):
    xf = x.astype(jnp.float32)
    mu = jnp.mean(xf, axis=-1, keepdims=True)
    xc = xf - mu
    var = jnp.mean(xc * xc, axis=-1, keepdims=True)
    y = xc * lax.rsqrt(var + LN_EPS) * g.astype(jnp.float32) + b.astype(jnp.float32)
    return y.astype(x.dtype)


def chunk_causal_mask(n):
    pos = jnp.arange(n)
    return (pos[None, :] // CHUNK) <= (pos[:, None] // CHUNK)


def gmlp_branch(h, w_in, ln_g, ln_b, w_s, b_s, w_out):
    bsz, seq_len, _ = h.shape
    blk = min(seq_len, GM_BLOCK)
    n_blk = seq_len // blk
    proj = jnp.einsum('bld,de->ble', h, w_in)
    u, v, z = jnp.split(proj, 3, axis=-1)
    u = jax.nn.gelu(u)
    v = layer_norm(jax.nn.gelu(v), ln_g, ln_b)
    w = w_s[:, :blk, :blk] * chunk_causal_mask(blk).astype(w_s.dtype)
    vb = v.reshape(bsz, n_blk, blk, GM_GROUPS, GM_GROUP_DIM)
    mixed = jnp.einsum('gts,bnsgc->bntgc', w, vb) + b_s[:, :blk].T[None, None, :, :, None]
    s = u * mixed.reshape(bsz, seq_len, GM_WIDTH)
    y = s * jax.nn.silu(z)
    return jnp.einsum('ble,ed->bld', y, w_out), v


def stick_breaking_attend(q, k, v, q_offset):
    tq = q.shape[1]
    scale = SB_HEAD_DIM ** -0.5
    outs = []
    for start in range(0, tq, SB_Q_BLOCK):
        stop = min(start + SB_Q_BLOCK, tq)
        n_keys = q_offset + stop
        qb = q[:, start:stop].astype(jnp.float32)
        kb = k[:, :n_keys].astype(jnp.float32)
        vb = v[:, :n_keys].astype(jnp.float32)
        logits = jnp.einsum('bqhd,bkhd->bhqk', qb, kb) * scale
        t_pos = q_offset + jnp.arange(start, stop)
        s_pos = jnp.arange(n_keys)
        strict = s_pos[None, :] < t_pos[:, None]
        log_fail = jnp.where(strict, jax.nn.log_sigmoid(-logits), 0.0)
        later_fail = lax.cumsum(log_fail, axis=3, reverse=True) - log_fail
        weights = jnp.where(strict, jnp.exp(jax.nn.log_sigmoid(logits) + later_fail), 0.0)
        outs.append(jnp.einsum('bhqk,bkhd->bqhd', weights, vb))
    return jnp.concatenate(outs, axis=1).astype(v.dtype)


def sb_branch(h, w_in, w_out, cache_k, cache_v):
    bsz, seq_len, _ = h.shape
    proj = jnp.einsum('bld,de->ble', h, w_in)
    q, k, v, z = jnp.split(proj, 4, axis=-1)
    q = q.reshape(bsz, seq_len, SB_HEADS, SB_HEAD_DIM)
    k = k.reshape(bsz, seq_len, SB_HEADS, SB_HEAD_DIM)
    v = v.reshape(bsz, seq_len, SB_HEADS, SB_HEAD_DIM)
    if cache_k is None:
        o = stick_breaking_attend(q, k, v, 0)
    else:
        k_all = jnp.concatenate([cache_k.astype(k.dtype), k], axis=1)
        v_all = jnp.concatenate([cache_v.astype(v.dtype), v], axis=1)
        o = stick_breaking_attend(q, k_all, v_all, cache_k.shape[1])
    y = o.reshape(bsz, seq_len, SB_WIDTH) * jax.nn.silu(z)
    return jnp.einsum('ble,ed->bld', y, w_out), k, v


def setup_inputs(seed: int = 0) -> dict:
    key = jax.random.key(seed)
    ks = jax.random.split(key, 16)
    f32 = jnp.float32
    x_prompt = jax.random.normal(ks[0], (BATCH, SEQ, D_MODEL), f32)
    x_sample = jax.random.normal(ks[1], (DEC_BATCH, DEC_SEQ, D_MODEL), f32)
    cache_sb_k = jax.random.normal(ks[2], (N_B_LAYERS, DEC_BATCH, PAST_LEN, SB_HEADS, SB_HEAD_DIM), f32)
    cache_sb_v = jax.random.normal(ks[3], (N_B_LAYERS, DEC_BATCH, PAST_LEN, SB_HEADS, SB_HEAD_DIM), f32)
    norm_g = 1.0 + 0.02 * jax.random.normal(ks[4], (DEPTH, D_MODEL), f32)
    final_norm_g = 1.0 + 0.02 * jax.random.normal(ks[5], (D_MODEL,), f32)
    gm_w_in = jax.random.normal(ks[6], (N_A_LAYERS, D_MODEL, 3 * GM_WIDTH), f32) * D_MODEL ** -0.5
    gm_ln_g = 1.0 + 0.02 * jax.random.normal(ks[7], (N_A_LAYERS, GM_WIDTH), f32)
    gm_ln_b = 0.02 * jax.random.normal(ks[8], (N_A_LAYERS, GM_WIDTH), f32)
    gm_w_s = jax.random.normal(ks[9], (N_A_LAYERS, GM_GROUPS, GM_BLOCK, GM_BLOCK), f32) * GM_BLOCK ** -0.5
    gm_b_s = 1.0 + 0.02 * jax.random.normal(ks[10], (N_A_LAYERS, GM_GROUPS, GM_BLOCK), f32)
    gm_w_out = jax.random.normal(ks[11], (N_A_LAYERS, GM_WIDTH, D_MODEL), f32) * GM_WIDTH ** -0.5
    sb_w_in = jax.random.normal(ks[12], (N_B_LAYERS, D_MODEL, 4 * SB_WIDTH), f32) * D_MODEL ** -0.5
    sb_w_out = jax.random.normal(ks[13], (N_B_LAYERS, SB_WIDTH, D_MODEL), f32) * SB_WIDTH ** -0.5
    return {"x_prompt": x_prompt, "x_sample": x_sample, "cache_sb_k": cache_sb_k, "cache_sb_v": cache_sb_v,
            "norm_g": norm_g, "final_norm_g": final_norm_g,
            "gm_w_in": gm_w_in, "gm_ln_g": gm_ln_g, "gm_ln_b": gm_ln_b, "gm_w_s": gm_w_s, "gm_b_s": gm_b_s,
            "gm_w_out": gm_w_out, "sb_w_in": sb_w_in, "sb_w_out": sb_w_out}


def reference(x_prompt, x_sample, cache_sb_k, cache_sb_v, norm_g, final_norm_g,
              gm_w_in, gm_ln_g, gm_ln_b, gm_w_s, gm_b_s, gm_w_out, sb_w_in, sb_w_out):
    xp, xs = x_prompt, x_sample
    gm_v_rows = []
    kp_rows, vp_rows, ks_rows, vs_rows = [], [], [], []
    for i in range(DEPTH):
        hp = rms_norm(xp, norm_g[i])
        hs = rms_norm(xs, norm_g[i])
        j = i // N_MIXERS
        if i % N_MIXERS == 0:
            dp, _ = gmlp_branch(hp, gm_w_in[j], gm_ln_g[j], gm_ln_b[j], gm_w_s[j], gm_b_s[j], gm_w_out[j])
            ds, v_new = gmlp_branch(hs, gm_w_in[j], gm_ln_g[j], gm_ln_b[j], gm_w_s[j], gm_b_s[j], gm_w_out[j])
            gm_v_rows.append(v_new)
        else:
            dp, kp, vp = sb_branch(hp, sb_w_in[j], sb_w_out[j], None, None)
            ds, kn, vn = sb_branch(hs, sb_w_in[j], sb_w_out[j], cache_sb_k[j], cache_sb_v[j])
            kp_rows.append(kp)
            vp_rows.append(vp)
            ks_rows.append(kn)
            vs_rows.append(vn)
        xp = xp + dp
        xs = xs + ds
    y_prompt = rms_norm(xp, final_norm_g)
    y_sample = rms_norm(xs, final_norm_g)
    k_prompt_new = jnp.stack(kp_rows)
    v_prompt_new = jnp.stack(vp_rows)
    k_sample_new = jnp.stack(ks_rows)
    v_sample_new = jnp.stack(vs_rows)
    gm_v_sample = jnp.stack(gm_v_rows)
    return (y_prompt, y_sample, k_prompt_new, v_prompt_new, k_sample_new, v_sample_new, gm_v_sample)
```

```python
import functools

import jax
import jax.numpy as jnp
from jax import lax
from jax.experimental import pallas as pl
from jax.experimental.pallas import tpu as pltpu

D_MODEL = 2048
DEPTH = 2
CHUNK = 64
N_MIXERS = 2
GM_WIDTH = 2 * D_MODEL
GM_BLOCK = 128
GM_GROUPS = 16
GM_GROUP_DIM = GM_WIDTH // GM_GROUPS
SB_HEADS = 16
SB_HEAD_DIM = D_MODEL // SB_HEADS
SB_WIDTH = SB_HEADS * SB_HEAD_DIM
NORM_EPS = 1e-6
LN_EPS = 1e-5

LANES = 128
VMEM_LIMIT_BYTES = 48 * 1024 * 1024

F32 = jnp.float32
BF16 = jnp.bfloat16


def _params(n_grid_dims):
    return pltpu.CompilerParams(
        dimension_semantics=("arbitrary",) * n_grid_dims,
        vmem_limit_bytes=VMEM_LIMIT_BYTES,
    )


def _gelu(x):
    c = 0.7978845608028654
    return x * (0.5 * (1.0 + jnp.tanh(c * (x + 0.044715 * (x * x * x)))))


def _silu(x):
    return x * (1.0 / (1.0 + jnp.exp(-x)))


def _rmsnorm_kernel(x_ref, g_ref, o_ref):
    x = x_ref[...]
    ms = jnp.mean(x * x, axis=-1, keepdims=True)
    o_ref[...] = (x * lax.rsqrt(ms + NORM_EPS) * g_ref[...]).astype(o_ref.dtype)


def _rmsnorm(x, g, out_dtype, tm):
    m, d = x.shape
    return pl.pallas_call(
        _rmsnorm_kernel,
        grid=(m // tm,),
        in_specs=[pl.BlockSpec((tm, d), lambda i: (i, 0)),
                  pl.BlockSpec((1, d), lambda i: (0, 0))],
        out_specs=pl.BlockSpec((tm, d), lambda i: (i, 0)),
        out_shape=jax.ShapeDtypeStruct((m, d), out_dtype),
        compiler_params=_params(1),
        name="rmsnorm",
    )(x, g.reshape(1, d))


def _mm_kernel(*refs, act, scale, has_res):
    if has_res:
        x_ref, w_ref, r_ref, o_ref = refs
    else:
        x_ref, w_ref, o_ref = refs
    acc = jnp.dot(x_ref[...], w_ref[...], preferred_element_type=F32)
    if act == "gelu":
        acc = _gelu(acc)
    elif act == "silu":
        acc = _silu(acc)
    if scale is not None:
        acc = acc * scale
    if has_res:
        acc = acc + r_ref[...]
    o_ref[...] = acc.astype(o_ref.dtype)


def _matmul(x, w, out_dtype, *, tm, tn, act=None, scale=None, res=None, name="matmul"):
    m, k = x.shape
    _, n = w.shape
    in_specs = [pl.BlockSpec((tm, k), lambda i, j: (i, 0)),
                pl.BlockSpec((k, tn), lambda i, j: (0, j))]
    args = [x, w]
    if res is not None:
        in_specs.append(pl.BlockSpec((tm, tn), lambda i, j: (i, j)))
        args.append(res)
    return pl.pallas_call(
        functools.partial(_mm_kernel, act=act, scale=scale, has_res=res is not None),
        grid=(m // tm, n // tn),
        in_specs=in_specs,
        out_specs=pl.BlockSpec((tm, tn), lambda i, j: (i, j)),
        out_shape=jax.ShapeDtypeStruct((m, n), out_dtype),
        compiler_params=_params(2),
        name=name,
    )(*args)


def _gm_mix_kernel(u_ref, gv_ref, sz_ref, wm_ref, bt_ref, lg_ref, lb_ref, *out_refs, emit_vn):
    if emit_vn:
        y_ref, vn_ref = out_refs
    else:
        (y_ref,) = out_refs
    tm = u_ref.shape[0]
    inv_w = 1.0 / GM_WIDTH
    s1 = jnp.zeros((tm, 1), F32)
    for g in range(GM_GROUPS):
        cols = slice(g * GM_GROUP_DIM, (g + 1) * GM_GROUP_DIM)
        s1 = s1 + jnp.sum(gv_ref[:, cols].astype(F32), axis=-1, keepdims=True)
    mu = s1 * inv_w
    s2 = jnp.zeros((tm, 1), F32)
    for g in range(GM_GROUPS):
        cols = slice(g * GM_GROUP_DIM, (g + 1) * GM_GROUP_DIM)
        xc = gv_ref[:, cols].astype(F32) - mu
        s2 = s2 + jnp.sum(xc * xc, axis=-1, keepdims=True)
    rstd = lax.rsqrt(s2 * inv_w + LN_EPS)
    for g in range(GM_GROUPS):
        cols = slice(g * GM_GROUP_DIM, (g + 1) * GM_GROUP_DIM)
        vn = (gv_ref[:, cols].astype(F32) - mu) * rstd * lg_ref[:, cols] + lb_ref[:, cols]
        if emit_vn:
            vn_ref[:, cols] = vn
        vnb = vn.astype(BF16)
        wg = wm_ref[g]
        bias = bt_ref[:, g:g + 1]
        for r in range(tm // GM_BLOCK):
            rows = slice(r * GM_BLOCK, (r + 1) * GM_BLOCK)
            mixed = jnp.dot(wg, vnb[rows], preferred_element_type=F32) + bias
            y = u_ref[rows, cols].astype(F32) * mixed * sz_ref[rows, cols].astype(F32)
            y_ref[rows, cols] = y.astype(y_ref.dtype)


def _gm_mix(ug, sz, wm, bt, ln_g, ln_b, *, tm, emit_vn):
    m = sz.shape[0]
    row_blk = lambda c: pl.BlockSpec((tm, GM_WIDTH), lambda i, c=c: (i, c))
    full = lambda shape: pl.BlockSpec(shape, lambda i: (0,) * len(shape))
    out_shape = [jax.ShapeDtypeStruct((m, GM_WIDTH), BF16)]
    out_specs = [row_blk(0)]
    if emit_vn:
        out_shape.append(jax.ShapeDtypeStruct((m, GM_WIDTH), F32))
        out_specs.append(row_blk(0))
    return pl.pallas_call(
        functools.partial(_gm_mix_kernel, emit_vn=emit_vn),
        grid=(m // tm,),
        in_specs=[row_blk(0), row_blk(1), row_blk(0),
                  full((GM_GROUPS, GM_BLOCK, GM_BLOCK)), full((GM_BLOCK, GM_GROUPS)),
                  full((1, GM_WIDTH)), full((1, GM_WIDTH))],
        out_specs=out_specs,
        out_shape=out_shape,
        compiler_params=_params(1),
        name="gm_mix",
    )(ug, ug, sz, wm, bt, ln_g.reshape(1, GM_WIDTH), ln_b.reshape(1, GM_WIDTH))


def _sb_chunk(s, strict, carry, tt):
    r, width = s.shape
    nb = width // LANES
    lf = -(jnp.maximum(s, 0.0) + jnp.log1p(jnp.exp(-jnp.abs(s))))
    if strict is not None:
        lf = jnp.where(strict, lf, 0.0)
    stacked = jnp.concatenate([lf[:, b * LANES:(b + 1) * LANES] for b in range(nb)], axis=0)
    hi = stacked.astype(BF16)
    lo = (stacked - hi.astype(F32)).astype(BF16)
    cr = (jnp.dot(hi, tt, preferred_element_type=F32)
          + jnp.dot(lo, tt, preferred_element_type=F32))
    ps = [None] * nb
    for b in reversed(range(nb)):
        blk = cr[b * r:(b + 1) * r]
        ps[b] = jnp.exp(s[:, b * LANES:(b + 1) * LANES] + (blk[:, :LANES] + carry))
        carry = carry + blk[:, LANES:]
    p = ps[0] if nb == 1 else jnp.concatenate(ps, axis=1)
    if strict is not None:
        p = jnp.where(strict, p, 0.0)
    return p, carry


def _qk(q, k):
    return lax.dot_general(q, k, (((1,), (1,)), ((), ())), preferred_element_type=F32)


def _sb_prompt_kernel(q_ref, k_ref, v_ref, sz_ref, tt_ref, o_ref, kb_ref, vb_ref, *, tq):
    qi = pl.program_id(2)

    @pl.when(qi == 0)
    def _():
        kb_ref[...] = k_ref[0].astype(BF16)
        vb_ref[...] = v_ref[0].astype(BF16)

    tt = tt_ref[...]
    q = q_ref[0]
    row = lax.broadcasted_iota(jnp.int32, (tq, tq), 0)
    col = lax.broadcasted_iota(jnp.int32, (tq, tq), 1)
    strict = col < row

    def chunk(k0, mask, carry, acc):
        s = _qk(q, kb_ref[pl.ds(k0, tq), :])
        p, carry = _sb_chunk(s, mask, carry, tt)
        acc = acc + jnp.dot(p.astype(BF16), vb_ref[pl.ds(k0, tq), :], preferred_element_type=F32)
        return carry, acc

    zeros = jnp.zeros((tq, LANES), F32)
    state = chunk(pl.multiple_of(qi * tq, tq), strict, zeros, zeros)

    def body(it, st):
        return chunk(pl.multiple_of((qi - 1 - it) * tq, tq), None, *st)

    _, acc = lax.fori_loop(0, qi, body, state)
    o_ref[0] = (acc * sz_ref[0].astype(F32)).astype(o_ref.dtype)


def _sb_prompt(q, k, v, sz, tt, *, tq):
    b, l, _ = q.shape
    blk_q = pl.BlockSpec((1, tq, SB_HEAD_DIM), lambda bi, h, qi: (bi, qi, h))
    blk_kv = pl.BlockSpec((1, l, SB_HEAD_DIM), lambda bi, h, qi: (bi, 0, h))
    return pl.pallas_call(
        functools.partial(_sb_prompt_kernel, tq=tq),
        grid=(b, SB_HEADS, l // tq),
        in_specs=[blk_q, blk_kv, blk_kv, blk_q,
                  pl.BlockSpec((LANES, 2 * LANES), lambda bi, h, qi: (0, 0))],
        out_specs=blk_q,
        out_shape=jax.ShapeDtypeStruct(q.shape, BF16),
        scratch_shapes=[pltpu.VMEM((l, SB_HEAD_DIM), BF16), pltpu.VMEM((l, SB_HEAD_DIM), BF16)],
        compiler_params=_params(3),
        name="sb_prompt",
    )(q, k, v, sz, tt)


def _sb_decode_kernel(q_ref, kn_ref, vn_ref, kc_ref, vc_ref, sz_ref, tt_ref, o_ref):
    tq = q_ref.shape[1]
    tt = tt_ref[...]
    q = q_ref[0]
    row = lax.broadcasted_iota(jnp.int32, (tq, LANES), 0)
    col = lax.broadcasted_iota(jnp.int32, (tq, LANES), 1)
    zeros = jnp.zeros((tq, LANES), F32)
    p, carry = _sb_chunk(_qk(q, kn_ref[0].astype(BF16)), col < row, zeros, tt)
    acc = jnp.dot(p.astype(BF16), vn_ref[0].astype(BF16), preferred_element_type=F32)
    p, _ = _sb_chunk(_qk(q, kc_ref[0].astype(BF16)), None, carry, tt)
    acc = acc + jnp.dot(p.astype(BF16), vc_ref[0].astype(BF16), preferred_element_type=F32)
    o_ref[0] = (acc * sz_ref[0].astype(F32)).astype(o_ref.dtype)


def _sb_decode(q, k_new, v_new, k_cache, v_cache, sz, tt):
    b, t, _ = q.shape
    p = k_cache.shape[1]
    blk = lambda rows: pl.BlockSpec((1, rows, SB_HEAD_DIM), lambda bi, h: (bi, 0, h))
    return pl.pallas_call(
        _sb_decode_kernel,
        grid=(b, SB_HEADS),
        in_specs=[blk(t), blk(LANES), blk(LANES), blk(p), blk(p), blk(t),
                  pl.BlockSpec((LANES, 2 * LANES), lambda bi, h: (0, 0))],
        out_specs=blk(t),
        out_shape=jax.ShapeDtypeStruct(q.shape, BF16),
        compiler_params=_params(2),
        name="sb_decode",
    )(q, k_new, v_new, k_cache, v_cache, sz, tt)


def _chunk_causal(n):
    pos = jnp.arange(n)
    return (pos[None, :] // CHUNK) <= (pos[:, None] // CHUNK)


def _gm_layer(x, seq_len, norm_g, w_uv, w_z, ln_g, ln_b, w_s, b_s, w_out, *, tm, tm_mix, emit_vn):
    act_dtype = F32 if emit_vn else BF16
    h = _rmsnorm(x, norm_g, BF16, tm)
    ug = _matmul(h, w_uv, act_dtype, tm=tm, tn=512, act="gelu", name="gm_in_uv")
    sz = _matmul(h, w_z, act_dtype, tm=tm, tn=512, act="silu", name="gm_in_z")
    blk = min(seq_len, GM_BLOCK)
    w = w_s[:, :blk, :blk] * _chunk_causal(blk).astype(w_s.dtype)
    reps = GM_BLOCK // blk
    wm = jnp.einsum("ab,gts->gatbs", jnp.eye(reps, dtype=w.dtype), w).reshape(
        GM_GROUPS, GM_BLOCK, GM_BLOCK).astype(BF16)
    bt = jnp.tile(b_s[:, :blk].T, (reps, 1))
    outs = _gm_mix(ug, sz, wm, bt, ln_g, ln_b, tm=tm_mix, emit_vn=emit_vn)
    x_new = _matmul(outs[0], w_out, F32, tm=tm, tn=512, res=x, name="gm_out")
    return x_new, (outs[1] if emit_vn else None)


def _sb_in(x, norm_g, w_q, w_k, w_v, w_z, *, tm):
    h = _rmsnorm(x, norm_g, BF16, tm)
    q = _matmul(h, w_q, BF16, tm=tm, tn=512, scale=SB_HEAD_DIM ** -0.5, name="sb_in_q")
    k = _matmul(h, w_k, F32, tm=tm, tn=512, name="sb_in_k")
    v = _matmul(h, w_v, F32, tm=tm, tn=512, name="sb_in_v")
    sz = _matmul(h, w_z, BF16, tm=tm, tn=512, act="silu", name="sb_in_z")
    return q, k, v, sz


def kernel(x_prompt, x_sample, cache_sb_k, cache_sb_v, norm_g, final_norm_g, gm_w_in, gm_ln_g, gm_ln_b,
           gm_w_s, gm_b_s, gm_w_out, sb_w_in, sb_w_out):
    bsz, seq, d = x_prompt.shape
    dbsz, dseq, _ = x_sample.shape
    past = cache_sb_k.shape[2]
    xp = x_prompt.reshape(bsz * seq, d)
    xs = x_sample.reshape(dbsz * dseq, d)
    tm_p, tm_s = 1024, dbsz * dseq

    tri = (jnp.arange(LANES)[:, None] >= jnp.arange(LANES)[None, :])
    tt = jnp.concatenate([tri, jnp.ones_like(tri)], axis=1).astype(BF16)

    gm_v_rows, kp_rows, vp_rows, ks_rows, vs_rows = [], [], [], [], []
    for i in range(DEPTH):
        j = i // N_MIXERS
        if i % N_MIXERS == 0:
            w_uv = gm_w_in[j][:, :2 * GM_WIDTH].astype(BF16)
            w_z = gm_w_in[j][:, 2 * GM_WIDTH:].astype(BF16)
            w_out = gm_w_out[j].astype(BF16)
            common = (norm_g[i], w_uv, w_z, gm_ln_g[j], gm_ln_b[j], gm_w_s[j], gm_b_s[j], w_out)
            xp, _ = _gm_layer(xp, seq, *common, tm=tm_p, tm_mix=256, emit_vn=False)
            xs, v_new = _gm_layer(xs, dseq, *common, tm=tm_s, tm_mix=tm_s, emit_vn=True)
            gm_v_rows.append(v_new.reshape(dbsz, dseq, GM_WIDTH))
        else:
            ws = [sb_w_in[j][:, c * SB_WIDTH:(c + 1) * SB_WIDTH].astype(BF16) for c in range(4)]
            w_out = sb_w_out[j].astype(BF16)
            q, k, v, sz = _sb_in(xp, norm_g[i], *ws, tm=tm_p)
            shp = (bsz, seq, SB_WIDTH)
            y = _sb_prompt(q.reshape(shp), k.reshape(shp), v.reshape(shp), sz.reshape(shp), tt, tq=256)
            xp = _matmul(y.reshape(bsz * seq, SB_WIDTH), w_out, F32, tm=tm_p, tn=512, res=xp, name="sb_out")
            kp_rows.append(k.reshape(bsz, seq, SB_HEADS, SB_HEAD_DIM))
            vp_rows.append(v.reshape(bsz, seq, SB_HEADS, SB_HEAD_DIM))

            q, k, v, sz = _sb_in(xs, norm_g[i], *ws, tm=tm_s)
            shp = (dbsz, dseq, SB_WIDTH)
            pad = ((0, 0), (0, LANES - dseq), (0, 0))
            y = _sb_decode(q.reshape(shp), jnp.pad(k.reshape(shp), pad), jnp.pad(v.reshape(shp), pad),
                           cache_sb_k[j].reshape(dbsz, past, SB_WIDTH),
                           cache_sb_v[j].reshape(dbsz, past, SB_WIDTH), sz.reshape(shp), tt)
            xs = _matmul(y.reshape(dbsz * dseq, SB_WIDTH), w_out, F32, tm=tm_s, tn=512, res=xs, name="sb_out")
            ks_rows.append(k.reshape(dbsz, dseq, SB_HEADS, SB_HEAD_DIM))
            vs_rows.append(v.reshape(dbsz, dseq, SB_HEADS, SB_HEAD_DIM))

    y_prompt = _rmsnorm(xp, final_norm_g, F32, 512).reshape(bsz, seq, d)
    y_sample = _rmsnorm(xs, final_norm_g, F32, tm_s).reshape(dbsz, dseq, d)
    return (y_prompt, y_sample, jnp.stack(kp_rows), jnp.stack(vp_rows), jnp.stack(ks_rows),
            jnp.stack(vs_rows), jnp.stack(gm_v_rows))
```

```python
import functools

import jax
import jax.numpy as jnp
from jax import lax
from jax.experimental import pallas as pl
from jax.experimental.pallas import tpu as pltpu

D_MODEL = 2048
DEPTH = 2
CHUNK = 64
N_MIXERS = 2
GM_WIDTH = 2 * D_MODEL
GM_BLOCK = 128
GM_GROUPS = 16
GM_GROUP_DIM = GM_WIDTH // GM_GROUPS
SB_HEADS = 16
SB_HEAD_DIM = D_MODEL // SB_HEADS
SB_WIDTH = SB_HEADS * SB_HEAD_DIM
NORM_EPS = 1e-6
LN_EPS = 1e-5

LANES = 128
VMEM_LIMIT_BYTES = 48 * 1024 * 1024

F32 = jnp.float32
BF16 = jnp.bfloat16


def _params(n_grid_dims):
    return pltpu.CompilerParams(
        dimension_semantics=("arbitrary",) * n_grid_dims,
        vmem_limit_bytes=VMEM_LIMIT_BYTES,
    )


def _gelu(x):
    c = 0.7978845608028654
    return x * (0.5 * (1.0 + jnp.tanh(c * (x + 0.044715 * (x * x * x)))))


def _silu(x):
    return x * (1.0 / (1.0 + jnp.exp(-x)))


def _rmsnorm_kernel(x_ref, g_ref, o_ref):
    x = x_ref[...]
    ms = jnp.mean(x * x, axis=-1, keepdims=True)
    o_ref[...] = (x * lax.rsqrt(ms + NORM_EPS) * g_ref[...]).astype(o_ref.dtype)


def _rmsnorm(x, g, out_dtype, tm):
    m, d = x.shape
    return pl.pallas_call(
        _rmsnorm_kernel,
        grid=(m // tm,),
        in_specs=[pl.BlockSpec((tm, d), lambda i: (i, 0)),
                  pl.BlockSpec((1, d), lambda i: (0, 0))],
        out_specs=pl.BlockSpec((tm, d), lambda i: (i, 0)),
        out_shape=jax.ShapeDtypeStruct((m, d), out_dtype),
        compiler_params=_params(1),
        name="rmsnorm",
    )(x, g.reshape(1, d))


def _mm_kernel(*refs, act, scale, has_res):
    if has_res:
        x_ref, w_ref, r_ref, o_ref = refs
    else:
        x_ref, w_ref, o_ref = refs
    acc = jnp.dot(x_ref[...], w_ref[...], preferred_element_type=F32)
    if act == "gelu":
        acc = _gelu(acc)
    elif act == "silu":
        acc = _silu(acc)
    if scale is not None:
        acc = acc * scale
    if has_res:
        acc = acc + r_ref[...]
    o_ref[...] = acc.astype(o_ref.dtype)


def _matmul(x, w, out_dtype, *, tm, tn, act=None, scale=None, res=None, name="matmul"):
    m, k = x.shape
    _, n = w.shape
    in_specs = [pl.BlockSpec((tm, k), lambda i, j: (i, 0)),
                pl.BlockSpec((k, tn), lambda i, j: (0, j))]
    args = [x, w]
    if res is not None:
        in_specs.append(pl.BlockSpec((tm, tn), lambda i, j: (i, j)))
        args.append(res)
    return pl.pallas_call(
        functools.partial(_mm_kernel, act=act, scale=scale, has_res=res is not None),
        grid=(m // tm, n // tn),
        in_specs=in_specs,
        out_specs=pl.BlockSpec((tm, tn), lambda i, j: (i, j)),
        out_shape=jax.ShapeDtypeStruct((m, n), out_dtype),
        compiler_params=_params(2),
        name=name,
    )(*args)


def _gm_mix_kernel(u_ref, gv_ref, sz_ref, wm_ref, bt_ref, lg_ref, lb_ref, *out_refs, emit_vn):
    if emit_vn:
        y_ref, vn_ref = out_refs
    else:
        (y_ref,) = out_refs
    tm = u_ref.shape[0]
    inv_w = 1.0 / GM_WIDTH
    s1 = jnp.zeros((tm, 1), F32)
    for g in range(GM_GROUPS):
        cols = slice(g * GM_GROUP_DIM, (g + 1) * GM_GROUP_DIM)
        s1 = s1 + jnp.sum(gv_ref[:, cols].astype(F32), axis=-1, keepdims=True)
    mu = s1 * inv_w
    s2 = jnp.zeros((tm, 1), F32)
    for g in range(GM_GROUPS):
        cols = slice(g * GM_GROUP_DIM, (g + 1) * GM_GROUP_DIM)
        xc = gv_ref[:, cols].astype(F32) - mu
        s2 = s2 + jnp.sum(xc * xc, axis=-1, keepdims=True)
    rstd = lax.rsqrt(s2 * inv_w + LN_EPS)
    for g in range(GM_GROUPS):
        cols = slice(g * GM_GROUP_DIM, (g + 1) * GM_GROUP_DIM)
        vn = (gv_ref[:, cols].astype(F32) - mu) * rstd * lg_ref[:, cols] + lb_ref[:, cols]
        if emit_vn:
            vn_ref[:, cols] = vn
        vnb = vn.astype(BF16)
        wg = wm_ref[g]
        bias = bt_ref[:, g:g + 1]
        for r in range(tm // GM_BLOCK):
            rows = slice(r * GM_BLOCK, (r + 1) * GM_BLOCK)
            mixed = jnp.dot(wg, vnb[rows], preferred_element_type=F32) + bias
            y = u_ref[rows, cols].astype(F32) * mixed * sz_ref[rows, cols].astype(F32)
            y_ref[rows, cols] = y.astype(y_ref.dtype)


def _gm_mix(ug, sz, wm, bt, ln_g, ln_b, *, tm, emit_vn):
    m = sz.shape[0]
    row_blk = lambda c: pl.BlockSpec((tm, GM_WIDTH), lambda i, c=c: (i, c))
    full = lambda shape: pl.BlockSpec(shape, lambda i: (0,) * len(shape))
    out_shape = [jax.ShapeDtypeStruct((m, GM_WIDTH), BF16)]
    out_specs = [row_blk(0)]
    if emit_vn:
        out_shape.append(jax.ShapeDtypeStruct((m, GM_WIDTH), F32))
        out_specs.append(row_blk(0))
    return pl.pallas_call(
        functools.partial(_gm_mix_kernel, emit_vn=emit_vn),
        grid=(m // tm,),
        in_specs=[row_blk(0), row_blk(1), row_blk(0),
                  full((GM_GROUPS, GM_BLOCK, GM_BLOCK)), full((GM_BLOCK, GM_GROUPS)),
                  full((1, GM_WIDTH)), full((1, GM_WIDTH))],
        out_specs=out_specs,
        out_shape=out_shape,
        compiler_params=_params(1),
        name="gm_mix",
    )(ug, ug, sz, wm, bt, ln_g.reshape(1, GM_WIDTH), ln_b.reshape(1, GM_WIDTH))


SB_BLOCK = 256
SB_TQ = 256
SB_DEAD_LOG = -105.0


def _sb_chunk(s, strict, carry, tri):
    bw = tri.shape[0]
    nb = s.shape[1] // bw
    ps = [None] * nb
    for b in reversed(range(nb)):
        sb = s[:, b * bw:(b + 1) * bw]
        masked = strict is not None and b == nb - 1
        lf = -(jnp.maximum(sb, 0.0) + jnp.log1p(jnp.exp(-jnp.abs(sb))))
        if masked:
            lf = jnp.where(strict, lf, 0.0)
        hi = lf.astype(BF16)
        lo = (lf - hi.astype(F32)).astype(BF16)
        c = (jnp.dot(hi, tri, preferred_element_type=F32)
             + jnp.dot(lo, tri, preferred_element_type=F32))
        p = jnp.exp(sb + (c + carry))
        if masked:
            p = jnp.where(strict, p, 0.0)
        ps[b] = p.astype(BF16)
        carry = carry + c[:, 0:1]
    return ps, carry


def _qk(q, k):
    return lax.dot_general(q, k, (((1,), (1,)), ((), ())), preferred_element_type=F32)


def _pv(ps, v):
    bw = ps[0].shape[1]
    acc = None
    for b, p in enumerate(ps):
        d = jnp.dot(p, v[b * bw:(b + 1) * bw], preferred_element_type=F32)
        acc = d if acc is None else acc + d
    return acc


def _sb_prompt_kernel(q_ref, k_ref, v_ref, sz_ref, tri_ref, o_ref, kb_ref, vb_ref, kmax_ref):
    qi = pl.program_id(2)
    tq = SB_TQ

    @pl.when(qi == 0)
    def _():
        kb = k_ref[0].astype(BF16)
        kb_ref[...] = kb
        vb_ref[...] = v_ref[0].astype(BF16)
        kf = kb.astype(F32)
        kmax_ref[...] = jnp.sqrt(jnp.max(jnp.sum(kf * kf, axis=-1, keepdims=True), axis=0, keepdims=True))

    tri = tri_ref[...]
    q = q_ref[0]
    qf = q.astype(F32)
    bound = jnp.sqrt(jnp.sum(qf * qf, axis=-1, keepdims=True)) * kmax_ref[...] * 1.001
    row = lax.broadcasted_iota(jnp.int32, (tq, SB_BLOCK), 0)
    col = lax.broadcasted_iota(jnp.int32, (tq, SB_BLOCK), 1)
    strict = col < row
    zero_carry = jnp.zeros((tq, 1), F32)
    gate = sz_ref[0].astype(F32)

    def chunk(k0, width, mask, carry):
        s = _qk(q, kb_ref[pl.ds(k0, width), :])
        ps, carry = _sb_chunk(s, mask, carry, tri)
        return _pv(ps, vb_ref[pl.ds(k0, width), :]), carry

    def alive(carry):
        return (jnp.max(carry + bound) > SB_DEAD_LOG).astype(jnp.int32)

    @pl.when(qi == 0)
    def _():
        acc, _ = chunk(0, tq, strict, zero_carry)
        o_ref[0] = (acc * gate).astype(o_ref.dtype)

    @pl.when(qi > 0)
    def _():
        k0 = pl.multiple_of((qi - 1) * SB_BLOCK, SB_BLOCK)
        acc, carry = chunk(k0, SB_BLOCK + tq, strict, zero_carry)

        def cond(st):
            return jnp.logical_and(st[0] >= 0, st[1] > 0)

        def body(st):
            kb_idx, _, carry, acc = st
            d, carry = chunk(pl.multiple_of(kb_idx * SB_BLOCK, SB_BLOCK), SB_BLOCK, None, carry)
            return kb_idx - 1, alive(carry), carry, acc + d

        st = lax.while_loop(cond, body, (qi - 2, alive(carry), carry, acc))
        o_ref[0] = (st[3] * gate).astype(o_ref.dtype)


def _sb_prompt(q, k, v, sz, tri):
    b, l, _ = q.shape
    assert SB_TQ == SB_BLOCK and l % SB_TQ == 0
    blk_q = pl.BlockSpec((1, SB_TQ, SB_HEAD_DIM), lambda bi, h, qi: (bi, qi, h))
    blk_kv = pl.BlockSpec((1, l, SB_HEAD_DIM), lambda bi, h, qi: (bi, 0, h))
    return pl.pallas_call(
        _sb_prompt_kernel,
        grid=(b, SB_HEADS, l // SB_TQ),
        in_specs=[blk_q, blk_kv, blk_kv, blk_q,
                  pl.BlockSpec((SB_BLOCK, SB_BLOCK), lambda bi, h, qi: (0, 0))],
        out_specs=blk_q,
        out_shape=jax.ShapeDtypeStruct(q.shape, BF16),
        scratch_shapes=[pltpu.VMEM((l, SB_HEAD_DIM), BF16), pltpu.VMEM((l, SB_HEAD_DIM), BF16),
                        pltpu.VMEM((1, 1), F32)],
        compiler_params=_params(3),
        name="sb_prompt",
    )(q, k, v, sz, tri)


def _sb_decode_kernel(q_ref, kn_ref, vn_ref, kc_ref, vc_ref, sz_ref, tri_ref, o_ref):
    tq = q_ref.shape[1]
    nk = kn_ref.shape[1]
    tri = tri_ref[...]
    q = q_ref[0]
    row = lax.broadcasted_iota(jnp.int32, (tq, nk), 0)
    col = lax.broadcasted_iota(jnp.int32, (tq, nk), 1)
    ps, carry = _sb_chunk(_qk(q, kn_ref[0].astype(BF16)), col < row, jnp.zeros((tq, 1), F32),
                          tri[:nk, :nk])
    acc = _pv(ps, vn_ref[0].astype(BF16))
    ps, _ = _sb_chunk(_qk(q, kc_ref[0].astype(BF16)), None, carry, tri)
    acc = acc + _pv(ps, vc_ref[0].astype(BF16))
    o_ref[0] = (acc * sz_ref[0].astype(F32)).astype(o_ref.dtype)


def _sb_decode(q, k_new, v_new, k_cache, v_cache, sz, tri):
    b, t, _ = q.shape
    p = k_cache.shape[1]
    blk = lambda rows: pl.BlockSpec((1, rows, SB_HEAD_DIM), lambda bi, h: (bi, 0, h))
    return pl.pallas_call(
        _sb_decode_kernel,
        grid=(b, SB_HEADS),
        in_specs=[blk(t), blk(LANES), blk(LANES), blk(p), blk(p), blk(t),
                  pl.BlockSpec((SB_BLOCK, SB_BLOCK), lambda bi, h: (0, 0))],
        out_specs=blk(t),
        out_shape=jax.ShapeDtypeStruct(q.shape, BF16),
        compiler_params=_params(2),
        name="sb_decode",
    )(q, k_new, v_new, k_cache, v_cache, sz, tri)


def _chunk_causal(n):
    pos = jnp.arange(n)
    return (pos[None, :] // CHUNK) <= (pos[:, None] // CHUNK)


def _gm_layer(x, seq_len, norm_g, w_uv, w_z, ln_g, ln_b, w_s, b_s, w_out, *, tm, tm_mix, emit_vn):
    act_dtype = F32 if emit_vn else BF16
    h = _rmsnorm(x, norm_g, BF16, tm)
    ug = _matmul(h, w_uv, act_dtype, tm=tm, tn=512, act="gelu", name="gm_in_uv")
    sz = _matmul(h, w_z, act_dtype, tm=tm, tn=512, act="silu", name="gm_in_z")
    blk = min(seq_len, GM_BLOCK)
    w = w_s[:, :blk, :blk] * _chunk_causal(blk).astype(w_s.dtype)
    reps = GM_BLOCK // blk
    wm = jnp.einsum("ab,gts->gatbs", jnp.eye(reps, dtype=w.dtype), w).reshape(
        GM_GROUPS, GM_BLOCK, GM_BLOCK).astype(BF16)
    bt = jnp.tile(b_s[:, :blk].T, (reps, 1))
    outs = _gm_mix(ug, sz, wm, bt, ln_g, ln_b, tm=tm_mix, emit_vn=emit_vn)
    x_new = _matmul(outs[0], w_out, F32, tm=tm, tn=512, res=x, name="gm_out")
    return x_new, (outs[1] if emit_vn else None)


def _sb_in(x, norm_g, w_q, w_k, w_v, w_z, *, tm):
    h = _rmsnorm(x, norm_g, BF16, tm)
    q = _matmul(h, w_q, BF16, tm=tm, tn=512, scale=SB_HEAD_DIM ** -0.5, name="sb_in_q")
    k = _matmul(h, w_k, F32, tm=tm, tn=512, name="sb_in_k")
    v = _matmul(h, w_v, F32, tm=tm, tn=512, name="sb_in_v")
    sz = _matmul(h, w_z, BF16, tm=tm, tn=512, act="silu", name="sb_in_z")
    return q, k, v, sz


def kernel(x_prompt, x_sample, cache_sb_k, cache_sb_v, norm_g, final_norm_g, gm_w_in, gm_ln_g, gm_ln_b,
           gm_w_s, gm_b_s, gm_w_out, sb_w_in, sb_w_out):
    bsz, seq, d = x_prompt.shape
    dbsz, dseq, _ = x_sample.shape
    past = cache_sb_k.shape[2]
    xp = x_prompt.reshape(bsz * seq, d)
    xs = x_sample.reshape(dbsz * dseq, d)
    tm_p, tm_s = 1024, dbsz * dseq

    tt = (jnp.arange(SB_BLOCK)[:, None] >= jnp.arange(SB_BLOCK)[None, :]).astype(BF16)

    gm_v_rows, kp_rows, vp_rows, ks_rows, vs_rows = [], [], [], [], []
    for i in range(DEPTH):
        j = i // N_MIXERS
        if i % N_MIXERS == 0:
            w_uv = gm_w_in[j][:, :2 * GM_WIDTH].astype(BF16)
            w_z = gm_w_in[j][:, 2 * GM_WIDTH:].astype(BF16)
            w_out = gm_w_out[j].astype(BF16)
            common = (norm_g[i], w_uv, w_z, gm_ln_g[j], gm_ln_b[j], gm_w_s[j], gm_b_s[j], w_out)
            xp, _ = _gm_layer(xp, seq, *common, tm=tm_p, tm_mix=256, emit_vn=False)
            xs, v_new = _gm_layer(xs, dseq, *common, tm=tm_s, tm_mix=tm_s, emit_vn=True)
            gm_v_rows.append(v_new.reshape(dbsz, dseq, GM_WIDTH))
        else:
            ws = [sb_w_in[j][:, c * SB_WIDTH:(c + 1) * SB_WIDTH].astype(BF16) for c in range(4)]
            w_out = sb_w_out[j].astype(BF16)
            q, k, v, sz = _sb_in(xp, norm_g[i], *ws, tm=tm_p)
            shp = (bsz, seq, SB_WIDTH)
            y = _sb_prompt(q.reshape(shp), k.reshape(shp), v.reshape(shp), sz.reshape(shp), tt)
            xp = _matmul(y.reshape(bsz * seq, SB_WIDTH), w_out, F32, tm=tm_p, tn=512, res=xp, name="sb_out")
            kp_rows.append(k.reshape(bsz, seq, SB_HEADS, SB_HEAD_DIM))
            vp_rows.append(v.reshape(bsz, seq, SB_HEADS, SB_HEAD_DIM))

            q, k, v, sz = _sb_in(xs, norm_g[i], *ws, tm=tm_s)
            shp = (dbsz, dseq, SB_WIDTH)
            pad = ((0, 0), (0, LANES - dseq), (0, 0))
            y = _sb_decode(q.reshape(shp), jnp.pad(k.reshape(shp), pad), jnp.pad(v.reshape(shp), pad),
                           cache_sb_k[j].reshape(dbsz, past, SB_WIDTH),
                           cache_sb_v[j].reshape(dbsz, past, SB_WIDTH), sz.reshape(shp), tt)
            xs = _matmul(y.reshape(dbsz * dseq, SB_WIDTH), w_out, F32, tm=tm_s, tn=512, res=xs, name="sb_out")
            ks_rows.append(k.reshape(dbsz, dseq, SB_HEADS, SB_HEAD_DIM))
            vs_rows.append(v.reshape(dbsz, dseq, SB_HEADS, SB_HEAD_DIM))

    y_prompt = _rmsnorm(xp, final_norm_g, F32, 512).reshape(bsz, seq, d)
    y_sample = _rmsnorm(xs, final_norm_g, F32, tm_s).reshape(dbsz, dseq, d)
    return (y_prompt, y_sample, jnp.stack(kp_rows), jnp.stack(vp_rows), jnp.stack(ks_rows),
            jnp.stack(vs_rows), jnp.stack(gm_v_rows))
```

```python
import functools

import jax
import jax.numpy as jnp
from jax import lax
from jax.experimental import pallas as pl
from jax.experimental.pallas import tpu as pltpu

D_MODEL = 2048
DEPTH = 2
CHUNK = 64
N_MIXERS = 2
GM_WIDTH = 2 * D_MODEL
GM_BLOCK = 128
GM_GROUPS = 16
GM_GROUP_DIM = GM_WIDTH // GM_GROUPS
SB_HEADS = 16
SB_HEAD_DIM = D_MODEL // SB_HEADS
SB_WIDTH = SB_HEADS * SB_HEAD_DIM
NORM_EPS = 1e-6
LN_EPS = 1e-5

LANES = 128
VMEM_LIMIT_BYTES = 56 * 1024 * 1024
MM_BLOCK_BYTES = 8 * 1024 * 1024

F32 = jnp.float32
BF16 = jnp.bfloat16


def _params(n_grid_dims):
    return pltpu.CompilerParams(
        dimension_semantics=("arbitrary",) * n_grid_dims,
        vmem_limit_bytes=VMEM_LIMIT_BYTES,
    )


def _gelu(x):
    c = 0.7978845608028654
    return x * (0.5 * (1.0 + jnp.tanh(c * (x + 0.044715 * (x * x * x)))))


def _silu(x):
    return x * (1.0 / (1.0 + jnp.exp(-x)))


def _rmsnorm_kernel(x_ref, g_ref, o_ref):
    x = x_ref[...]
    ms = jnp.mean(x * x, axis=-1, keepdims=True)
    o_ref[...] = (x * lax.rsqrt(ms + NORM_EPS) * g_ref[...]).astype(o_ref.dtype)


def _rmsnorm(x, g, out_dtype, tm):
    m, d = x.shape
    return pl.pallas_call(
        _rmsnorm_kernel,
        grid=(m // tm,),
        in_specs=[pl.BlockSpec((tm, d), lambda i: (i, 0)),
                  pl.BlockSpec((1, d), lambda i: (0, 0))],
        out_specs=pl.BlockSpec((tm, d), lambda i: (i, 0)),
        out_shape=jax.ShapeDtypeStruct((m, d), out_dtype),
        compiler_params=_params(1),
        name="rmsnorm",
    )(x, g.reshape(1, d))


def _mm_kernel(*refs, act, scale, has_res):
    if has_res:
        x_ref, w_ref, r_ref, o_ref = refs
    else:
        x_ref, w_ref, o_ref = refs
    acc = jnp.dot(x_ref[...], w_ref[...].astype(BF16), preferred_element_type=F32)
    if act == "gelu":
        acc = _gelu(acc)
    elif act == "silu":
        acc = _silu(acc)
    if scale is not None:
        acc = acc * scale
    if has_res:
        acc = acc + r_ref[...]
    o_ref[...] = acc.astype(o_ref.dtype)


def _matmul(x, w, out_dtype, *, tm, tn, col0=0, n=None, act=None, scale=None, res=None, name="matmul"):
    m, k = x.shape
    n = w.shape[1] if n is None else n
    assert col0 % tn == 0 and n % tn == 0 and m % tm == 0
    jb = col0 // tn
    in_specs = [pl.BlockSpec((tm, k), lambda i, j: (i, 0)),
                pl.BlockSpec((k, tn), lambda i, j: (0, j + jb))]
    args = [x, w]
    if res is not None:
        in_specs.append(pl.BlockSpec((tm, tn), lambda i, j: (i, j)))
        args.append(res)
    return pl.pallas_call(
        functools.partial(_mm_kernel, act=act, scale=scale, has_res=res is not None),
        grid=(m // tm, n // tn),
        in_specs=in_specs,
        out_specs=pl.BlockSpec((tm, tn), lambda i, j: (i, j)),
        out_shape=jax.ShapeDtypeStruct((m, n), out_dtype),
        compiler_params=_params(2),
        name=name,
    )(*args)


def _gm_mix_kernel(u_ref, gv_ref, sz_ref, wm_ref, bt_ref, lg_ref, lb_ref, *out_refs, emit_vn):
    if emit_vn:
        y_ref, vn_ref = out_refs
    else:
        (y_ref,) = out_refs
    tm = u_ref.shape[0]
    inv_w = 1.0 / GM_WIDTH
    s1 = jnp.zeros((tm, 1), F32)
    for g in range(GM_GROUPS):
        cols = slice(g * GM_GROUP_DIM, (g + 1) * GM_GROUP_DIM)
        s1 = s1 + jnp.sum(gv_ref[:, cols].astype(F32), axis=-1, keepdims=True)
    mu = s1 * inv_w
    s2 = jnp.zeros((tm, 1), F32)
    for g in range(GM_GROUPS):
        cols = slice(g * GM_GROUP_DIM, (g + 1) * GM_GROUP_DIM)
        xc = gv_ref[:, cols].astype(F32) - mu
        s2 = s2 + jnp.sum(xc * xc, axis=-1, keepdims=True)
    rstd = lax.rsqrt(s2 * inv_w + LN_EPS)
    for g in range(GM_GROUPS):
        cols = slice(g * GM_GROUP_DIM, (g + 1) * GM_GROUP_DIM)
        vn = (gv_ref[:, cols].astype(F32) - mu) * rstd * lg_ref[:, cols] + lb_ref[:, cols]
        if emit_vn:
            vn_ref[:, cols] = vn
        vnb = vn.astype(BF16)
        wg = wm_ref[g]
        bias = bt_ref[:, g:g + 1]
        for r in range(tm // GM_BLOCK):
            rows = slice(r * GM_BLOCK, (r + 1) * GM_BLOCK)
            mixed = jnp.dot(wg, vnb[rows], preferred_element_type=F32) + bias
            y = u_ref[rows, cols].astype(F32) * mixed * sz_ref[rows, cols].astype(F32)
            y_ref[rows, cols] = y.astype(y_ref.dtype)


def _gm_mix(ug, sz, wm, bt, ln_g, ln_b, *, tm, emit_vn):
    m = sz.shape[0]
    row_blk = lambda c: pl.BlockSpec((tm, GM_WIDTH), lambda i, c=c: (i, c))
    full = lambda shape: pl.BlockSpec(shape, lambda i: (0,) * len(shape))
    out_shape = [jax.ShapeDtypeStruct((m, GM_WIDTH), BF16)]
    out_specs = [row_blk(0)]
    if emit_vn:
        out_shape.append(jax.ShapeDtypeStruct((m, GM_WIDTH), F32))
        out_specs.append(row_blk(0))
    return pl.pallas_call(
        functools.partial(_gm_mix_kernel, emit_vn=emit_vn),
        grid=(m // tm,),
        in_specs=[row_blk(0), row_blk(1), row_blk(0),
                  full((GM_GROUPS, GM_BLOCK, GM_BLOCK)), full((GM_BLOCK, GM_GROUPS)),
                  full((1, GM_WIDTH)), full((1, GM_WIDTH))],
        out_specs=out_specs,
        out_shape=out_shape,
        compiler_params=_params(1),
        name="gm_mix",
    )(ug, ug, sz, wm, bt, ln_g.reshape(1, GM_WIDTH), ln_b.reshape(1, GM_WIDTH))


SB_BLOCK = 256
SB_TQ = 256
SB_HEADS_PER_STEP = 2
SB_DEAD_LOG = -105.0


def _sb_chunk(s, strict, carry, tri):
    r = s.shape[0]
    bw = tri.shape[0]
    nb = s.shape[1] // bw
    sbs = [s[:, b * bw:(b + 1) * bw] for b in range(nb)]
    parts = []
    for b in range(nb):
        lf = jnp.minimum(-sbs[b], 0.0) - jnp.log(1.0 + jnp.exp(-jnp.abs(sbs[b])))
        if strict is not None and b == nb - 1:
            lf = jnp.where(strict, lf, 0.0)
        hi = lf.astype(BF16)
        parts += [hi, (lf - hi.astype(F32)).astype(BF16)]
    c_all = jnp.dot(jnp.concatenate(parts, axis=0), tri, preferred_element_type=F32)
    ps = [None] * nb
    for b in reversed(range(nb)):
        c = c_all[2 * b * r:(2 * b + 1) * r] + c_all[(2 * b + 1) * r:(2 * b + 2) * r]
        p = jnp.exp(sbs[b] + (c + carry))
        if strict is not None and b == nb - 1:
            p = jnp.where(strict, p, 0.0)
        ps[b] = p.astype(BF16)
        carry = carry + c[:, 0:1]
    return ps, carry


def _qk(q, k):
    return lax.dot_general(q, k, (((1,), (1,)), ((), ())), preferred_element_type=F32)


def _pv(ps, v):
    bw = ps[0].shape[1]
    acc = None
    for b, p in enumerate(ps):
        d = jnp.dot(p, v[b * bw:(b + 1) * bw], preferred_element_type=F32)
        acc = d if acc is None else acc + d
    return acc


def _head_cols(h):
    return slice(h * SB_HEAD_DIM, (h + 1) * SB_HEAD_DIM)


def _sb_prompt_kernel(q_ref, k_ref, v_ref, sz_ref, tri_ref, o_ref, kb_ref, vb_ref):
    tq = SB_TQ
    heads = range(SB_HEADS_PER_STEP)
    kb_ref[...] = k_ref[0].astype(BF16)
    vb_ref[...] = v_ref[0].astype(BF16)
    kmax = []
    for h in heads:
        kf = kb_ref[:, _head_cols(h)].astype(F32)
        kmax.append(jnp.sqrt(jnp.max(jnp.sum(kf * kf, axis=-1, keepdims=True), axis=0, keepdims=True)))

    tri = tri_ref[...]
    row = lax.broadcasted_iota(jnp.int32, (tq, SB_BLOCK), 0)
    col = lax.broadcasted_iota(jnp.int32, (tq, SB_BLOCK), 1)
    strict = col < row
    zero_carry = jnp.zeros((tq, 1), F32)

    def chunk(h, q, k0, width, mask, carry):
        s = _qk(q, kb_ref[pl.ds(k0, width), _head_cols(h)])
        ps, carry = _sb_chunk(s, mask, carry, tri)
        return _pv(ps, vb_ref[pl.ds(k0, width), _head_cols(h)]), carry

    def store(h, rows, acc):
        o_ref[0, rows, _head_cols(h)] = (acc * sz_ref[0, rows, _head_cols(h)].astype(F32)).astype(o_ref.dtype)

    for h in heads:
        acc, _ = chunk(h, q_ref[0, 0:tq, _head_cols(h)], 0, tq, strict, zero_carry)
        store(h, slice(0, tq), acc)

    def q_block(qi, _):
        rows = pl.ds(pl.multiple_of(qi * tq, tq), tq)
        qs = [q_ref[0, rows, _head_cols(h)] for h in heads]
        bounds = []
        for h in heads:
            qf = qs[h].astype(F32)
            bounds.append(jnp.sqrt(jnp.sum(qf * qf, axis=-1, keepdims=True)) * kmax[h] * 1.001)

        def alive(carries):
            worst = carries[0] + bounds[0]
            for h in heads[1:]:
                worst = jnp.maximum(worst, carries[h] + bounds[h])
            return (jnp.max(worst) > SB_DEAD_LOG).astype(jnp.int32)

        k0 = pl.multiple_of((qi - 1) * SB_BLOCK, SB_BLOCK)
        first = [chunk(h, qs[h], k0, SB_BLOCK + tq, strict, zero_carry) for h in heads]
        accs = tuple(f[0] for f in first)
        carries = tuple(f[1] for f in first)

        def cond(st):
            return jnp.logical_and(st[0] >= 0, st[1] > 0)

        def body(st):
            kb_idx, _, carries, accs = st
            k0 = pl.multiple_of(kb_idx * SB_BLOCK, SB_BLOCK)
            nxt = [chunk(h, qs[h], k0, SB_BLOCK, None, carries[h]) for h in heads]
            carries = tuple(n[1] for n in nxt)
            accs = tuple(a + n[0] for a, n in zip(accs, nxt))
            return kb_idx - 1, alive(carries), carries, accs

        st = lax.while_loop(cond, body, (qi - 2, alive(carries), carries, accs))
        for h in heads:
            store(h, rows, st[3][h])
        return 0

    lax.fori_loop(1, q_ref.shape[1] // tq, q_block, 0)


def _sb_prompt(q, k, v, sz, tri):
    b, l, _ = q.shape
    assert SB_TQ == SB_BLOCK and l % SB_TQ == 0
    gw = SB_HEADS_PER_STEP * SB_HEAD_DIM
    blk = pl.BlockSpec((1, l, gw), lambda bi, g: (bi, 0, g))
    return pl.pallas_call(
        _sb_prompt_kernel,
        grid=(b, SB_HEADS // SB_HEADS_PER_STEP),
        in_specs=[blk, blk, blk, blk, pl.BlockSpec((SB_BLOCK, SB_BLOCK), lambda bi, g: (0, 0))],
        out_specs=blk,
        out_shape=jax.ShapeDtypeStruct(q.shape, BF16),
        scratch_shapes=[pltpu.VMEM((l, gw), BF16), pltpu.VMEM((l, gw), BF16)],
        compiler_params=_params(2),
        name="sb_prompt",
    )(q, k, v, sz, tri)


def _sb_decode_kernel(q_ref, kn_ref, vn_ref, sz_ref, tri_ref, kc_ref, vc_ref, o_ref):
    nh = SB_HEADS
    past = kc_ref.shape[0] // nh
    tq = q_ref.shape[1]
    tri = tri_ref[...]
    row = lax.broadcasted_iota(jnp.int32, (tq, LANES), 0)
    col = lax.broadcasted_iota(jnp.int32, (tq, LANES), 1)
    pad = jnp.zeros((LANES - tq, SB_HEAD_DIM), BF16)
    for h in range(nh):
        cols = _head_cols(h)
        q = q_ref[0, :, cols]
        kn = jnp.concatenate([kn_ref[0, :, cols].astype(BF16), pad], axis=0)
        vn = jnp.concatenate([vn_ref[0, :, cols].astype(BF16), pad], axis=0)
        ps, carry = _sb_chunk(_qk(q, kn), col < row, jnp.zeros((tq, 1), F32), tri[:LANES, :LANES])
        acc = _pv(ps, vn)
        head_rows = pl.ds(h, past, stride=nh)
        ps, _ = _sb_chunk(_qk(q, kc_ref[head_rows, :].astype(BF16)), None, carry, tri)
        acc = acc + _pv(ps, vc_ref[head_rows, :].astype(BF16))
        o_ref[0, :, cols] = (acc * sz_ref[0, :, cols].astype(F32)).astype(o_ref.dtype)


def _sb_decode(q, k_new, v_new, k_cache, v_cache, sz, tri):
    b, t, w = q.shape
    _, p, nh, dh = k_cache.shape
    assert nh == SB_HEADS and dh == SB_HEAD_DIM
    blk = pl.BlockSpec((1, t, w), lambda bi: (bi, 0, 0))
    blk_cache = pl.BlockSpec((None, p * nh, dh), lambda bi: (bi, 0, 0))
    return pl.pallas_call(
        _sb_decode_kernel,
        grid=(b,),
        in_specs=[blk, blk, blk, blk, pl.BlockSpec((SB_BLOCK, SB_BLOCK), lambda bi: (0, 0)),
                  blk_cache, blk_cache],
        out_specs=blk,
        out_shape=jax.ShapeDtypeStruct(q.shape, BF16),
        compiler_params=_params(1),
        name="sb_decode",
    )(q, k_new, v_new, sz, tri, k_cache.reshape(b, p * nh, dh), v_cache.reshape(b, p * nh, dh))


def _chunk_causal(n):
    pos = jnp.arange(n)
    return (pos[None, :] // CHUNK) <= (pos[:, None] // CHUNK)


def _mm_tiles(m, k):
    tm = min(m, MM_BLOCK_BYTES // (2 * k))
    tn = MM_BLOCK_BYTES // (4 * k) if tm <= 256 else 512
    return dict(tm=tm, tn=tn)


def _gm_layer(x, seq_len, norm_g, w_in, ln_g, ln_b, w_s, b_s, w_out, *, tm_mix, emit_vn):
    act_dtype = F32 if emit_vn else BF16
    m, d = x.shape
    h = _rmsnorm(x, norm_g, BF16, min(m, 1024))
    t_in = _mm_tiles(m, d)
    ug = _matmul(h, w_in, act_dtype, **t_in, n=2 * GM_WIDTH, act="gelu", name="gm_in_uv")
    sz = _matmul(h, w_in, act_dtype, **t_in, col0=2 * GM_WIDTH, n=GM_WIDTH, act="silu", name="gm_in_z")
    blk = min(seq_len, GM_BLOCK)
    w = w_s[:, :blk, :blk] * _chunk_causal(blk).astype(w_s.dtype)
    reps = GM_BLOCK // blk
    wm = jnp.einsum("ab,gts->gatbs", jnp.eye(reps, dtype=w.dtype), w).reshape(
        GM_GROUPS, GM_BLOCK, GM_BLOCK).astype(BF16)
    bt = jnp.tile(b_s[:, :blk].T, (reps, 1))
    outs = _gm_mix(ug, sz, wm, bt, ln_g, ln_b, tm=tm_mix, emit_vn=emit_vn)
    x_new = _matmul(outs[0], w_out, F32, **_mm_tiles(m, GM_WIDTH), res=x, name="gm_out")
    return x_new, (outs[1] if emit_vn else None)


def _sb_in(x, norm_g, w_in):
    m, d = x.shape
    h = _rmsnorm(x, norm_g, BF16, min(m, 1024))
    t = _mm_tiles(m, d)
    w = SB_WIDTH
    q = _matmul(h, w_in, BF16, **t, n=w, scale=SB_HEAD_DIM ** -0.5, name="sb_in_q")
    k = _matmul(h, w_in, F32, **t, col0=w, n=w, name="sb_in_k")
    v = _matmul(h, w_in, F32, **t, col0=2 * w, n=w, name="sb_in_v")
    sz = _matmul(h, w_in, BF16, **t, col0=3 * w, n=w, act="silu", name="sb_in_z")
    return q, k, v, sz


def kernel(x_prompt, x_sample, cache_sb_k, cache_sb_v, norm_g, final_norm_g, gm_w_in, gm_ln_g, gm_ln_b,
           gm_w_s, gm_b_s, gm_w_out, sb_w_in, sb_w_out):
    bsz, seq, d = x_prompt.shape
    dbsz, dseq, _ = x_sample.shape
    xp = x_prompt.reshape(bsz * seq, d)
    xs = x_sample.reshape(dbsz * dseq, d)

    tt = (jnp.arange(SB_BLOCK)[:, None] >= jnp.arange(SB_BLOCK)[None, :]).astype(BF16)

    gm_v_rows, kp_rows, vp_rows, ks_rows, vs_rows = [], [], [], [], []
    for i in range(DEPTH):
        j = i // N_MIXERS
        if i % N_MIXERS == 0:
            common = (norm_g[i], gm_w_in[j], gm_ln_g[j], gm_ln_b[j], gm_w_s[j], gm_b_s[j], gm_w_out[j])
            xp, _ = _gm_layer(xp, seq, *common, tm_mix=256, emit_vn=False)
            xs, v_new = _gm_layer(xs, dseq, *common, tm_mix=dbsz * dseq, emit_vn=True)
            gm_v_rows.append(v_new.reshape(dbsz, dseq, GM_WIDTH))
        else:
            q, k, v, sz = _sb_in(xp, norm_g[i], sb_w_in[j])
            shp = (bsz, seq, SB_WIDTH)
            y = _sb_prompt(q.reshape(shp), k.reshape(shp), v.reshape(shp), sz.reshape(shp), tt)
            xp = _matmul(y.reshape(bsz * seq, SB_WIDTH), sb_w_out[j], F32, **_mm_tiles(bsz * seq, SB_WIDTH),
                         res=xp, name="sb_out")
            kp_rows.append(k.reshape(bsz, seq, SB_HEADS, SB_HEAD_DIM))
            vp_rows.append(v.reshape(bsz, seq, SB_HEADS, SB_HEAD_DIM))

            q, k, v, sz = _sb_in(xs, norm_g[i], sb_w_in[j])
            shp = (dbsz, dseq, SB_WIDTH)
            y = _sb_decode(q.reshape(shp), k.reshape(shp), v.reshape(shp), cache_sb_k[j], cache_sb_v[j],
                           sz.reshape(shp), tt)
            xs = _matmul(y.reshape(dbsz * dseq, SB_WIDTH), sb_w_out[j], F32, **_mm_tiles(dbsz * dseq, SB_WIDTH),
                         res=xs, name="sb_out")
            ks_rows.append(k.reshape(dbsz, dseq, SB_HEADS, SB_HEAD_DIM))
            vs_rows.append(v.reshape(dbsz, dseq, SB_HEADS, SB_HEAD_DIM))

    y_prompt = _rmsnorm(xp, final_norm_g, F32, 512).reshape(bsz, seq, d)
    y_sample = _rmsnorm(xs, final_norm_g, F32, dbsz * dseq).reshape(dbsz, dseq, d)
    return (y_prompt, y_sample, jnp.stack(kp_rows), jnp.stack(vp_rows), jnp.stack(ks_rows),
            jnp.stack(vs_rows), jnp.stack(gm_v_rows))
```

```python
import functools

import jax
import jax.numpy as jnp
from jax import lax
from jax.experimental import pallas as pl
from jax.experimental.pallas import tpu as pltpu

D_MODEL = 2048
DEPTH = 2
CHUNK = 64
N_MIXERS = 2
GM_WIDTH = 2 * D_MODEL
GM_BLOCK = 128
GM_GROUPS = 16
GM_GROUP_DIM = GM_WIDTH // GM_GROUPS
SB_HEADS = 16
SB_HEAD_DIM = D_MODEL // SB_HEADS
SB_WIDTH = SB_HEADS * SB_HEAD_DIM
NORM_EPS = 1e-6
LN_EPS = 1e-5

LANES = 128
VMEM_LIMIT_BYTES = 56 * 1024 * 1024
MM_BLOCK_BYTES = 8 * 1024 * 1024

F32 = jnp.float32
BF16 = jnp.bfloat16


def _params(n_grid_dims):
    return pltpu.CompilerParams(
        dimension_semantics=("arbitrary",) * n_grid_dims,
        vmem_limit_bytes=VMEM_LIMIT_BYTES,
    )


def _gelu(x):
    c = 0.7978845608028654
    return x * (0.5 * (1.0 + jnp.tanh(c * (x + 0.044715 * (x * x * x)))))


def _silu(x):
    return x * (1.0 / (1.0 + jnp.exp(-x)))


def _rmsnorm_kernel(x_ref, g_ref, o_ref):
    x = x_ref[...]
    ms = jnp.mean(x * x, axis=-1, keepdims=True)
    o_ref[...] = (x * lax.rsqrt(ms + NORM_EPS) * g_ref[...]).astype(o_ref.dtype)


def _rmsnorm(x, g, out_dtype, tm):
    m, d = x.shape
    return pl.pallas_call(
        _rmsnorm_kernel,
        grid=(m // tm,),
        in_specs=[pl.BlockSpec((tm, d), lambda i: (i, 0)),
                  pl.BlockSpec((1, d), lambda i: (0, 0))],
        out_specs=pl.BlockSpec((tm, d), lambda i: (i, 0)),
        out_shape=jax.ShapeDtypeStruct((m, d), out_dtype),
        compiler_params=_params(1),
        name="rmsnorm",
    )(x, g.reshape(1, d))


def _mm_kernel(*refs, act, scale, has_res):
    if has_res:
        x_ref, w_ref, r_ref, o_ref = refs
    else:
        x_ref, w_ref, o_ref = refs
    acc = jnp.dot(x_ref[...], w_ref[...].astype(BF16), preferred_element_type=F32)
    if act == "gelu":
        acc = _gelu(acc)
    elif act == "silu":
        acc = _silu(acc)
    if scale is not None:
        acc = acc * scale
    if has_res:
        acc = acc + r_ref[...]
    o_ref[...] = acc.astype(o_ref.dtype)


def _matmul(x, w, out_dtype, *, tm, tn, col0=0, n=None, act=None, scale=None, res=None, name="matmul"):
    m, k = x.shape
    n = w.shape[1] if n is None else n
    assert col0 % tn == 0 and n % tn == 0 and m % tm == 0
    jb = col0 // tn
    in_specs = [pl.BlockSpec((tm, k), lambda i, j: (i, 0)),
                pl.BlockSpec((k, tn), lambda i, j: (0, j + jb))]
    args = [x, w]
    if res is not None:
        in_specs.append(pl.BlockSpec((tm, tn), lambda i, j: (i, j)))
        args.append(res)
    return pl.pallas_call(
        functools.partial(_mm_kernel, act=act, scale=scale, has_res=res is not None),
        grid=(m // tm, n // tn),
        in_specs=in_specs,
        out_specs=pl.BlockSpec((tm, tn), lambda i, j: (i, j)),
        out_shape=jax.ShapeDtypeStruct((m, n), out_dtype),
        compiler_params=_params(2),
        name=name,
    )(*args)


HEAD_TILE = 8


def _mm_heads_kernel(x_ref, w_ref, of_ref, ob_ref):
    tm = x_ref.shape[0]
    acc = jnp.dot(x_ref[...], w_ref[...].astype(BF16), preferred_element_type=F32)
    ob_ref[...] = acc.astype(ob_ref.dtype)
    rows = of_ref.reshape(tm * HEAD_TILE, SB_HEAD_DIM)
    for h in range(HEAD_TILE):
        rows[pl.ds(h, tm, stride=HEAD_TILE), :] = acc[:, _head_cols(h)]


def _matmul_heads(x, w, *, tm, col0, name):
    m, k = x.shape
    tn = HEAD_TILE * SB_HEAD_DIM
    assert col0 % tn == 0 and m % tm == 0
    jb = col0 // tn
    return pl.pallas_call(
        _mm_heads_kernel,
        grid=(m // tm, SB_WIDTH // tn),
        in_specs=[pl.BlockSpec((tm, k), lambda i, j: (i, 0)),
                  pl.BlockSpec((k, tn), lambda i, j: (0, j + jb))],
        out_specs=[pl.BlockSpec((tm, HEAD_TILE, SB_HEAD_DIM), lambda i, j: (i, j, 0)),
                   pl.BlockSpec((tm, tn), lambda i, j: (i, j))],
        out_shape=[jax.ShapeDtypeStruct((m, SB_HEADS, SB_HEAD_DIM), F32),
                   jax.ShapeDtypeStruct((m, SB_WIDTH), BF16)],
        compiler_params=_params(2),
        name=name,
    )(x, w)


def _mm_res_norm_kernel(x_ref, w_ref, r_ref, g_ref, o_ref):
    acc = jnp.dot(x_ref[...], w_ref[...], preferred_element_type=F32) + r_ref[...]
    ms = jnp.mean(acc * acc, axis=-1, keepdims=True)
    o_ref[...] = acc * lax.rsqrt(ms + NORM_EPS) * g_ref[...]


def _matmul_res_norm(x, w, res, g, *, tm, name):
    m, k = x.shape
    n = w.shape[1]
    row = lambda cols: pl.BlockSpec((tm, cols), lambda i: (i, 0))
    return pl.pallas_call(
        _mm_res_norm_kernel,
        grid=(m // tm,),
        in_specs=[row(k), pl.BlockSpec((k, n), lambda i: (0, 0)), row(n),
                  pl.BlockSpec((1, n), lambda i: (0, 0))],
        out_specs=row(n),
        out_shape=jax.ShapeDtypeStruct((m, n), F32),
        compiler_params=_params(1),
        name=name,
    )(x, w, res, g.reshape(1, n))


def _gm_mix_kernel(u_ref, gv_ref, sz_ref, wm_ref, bt_ref, lg_ref, lb_ref, *out_refs, emit_vn):
    if emit_vn:
        y_ref, vn_ref = out_refs
    else:
        (y_ref,) = out_refs
    tm = u_ref.shape[0]
    inv_w = 1.0 / GM_WIDTH
    s1 = jnp.zeros((tm, 1), F32)
    for g in range(GM_GROUPS):
        cols = slice(g * GM_GROUP_DIM, (g + 1) * GM_GROUP_DIM)
        s1 = s1 + jnp.sum(gv_ref[:, cols].astype(F32), axis=-1, keepdims=True)
    mu = s1 * inv_w
    s2 = jnp.zeros((tm, 1), F32)
    for g in range(GM_GROUPS):
        cols = slice(g * GM_GROUP_DIM, (g + 1) * GM_GROUP_DIM)
        xc = gv_ref[:, cols].astype(F32) - mu
        s2 = s2 + jnp.sum(xc * xc, axis=-1, keepdims=True)
    rstd = lax.rsqrt(s2 * inv_w + LN_EPS)
    for g in range(GM_GROUPS):
        cols = slice(g * GM_GROUP_DIM, (g + 1) * GM_GROUP_DIM)
        vn = (gv_ref[:, cols].astype(F32) - mu) * rstd * lg_ref[:, cols] + lb_ref[:, cols]
        if emit_vn:
            vn_ref[:, cols] = vn
        vnb = vn.astype(BF16)
        wg = wm_ref[g]
        bias = bt_ref[:, g:g + 1]
        for r in range(tm // GM_BLOCK):
            rows = slice(r * GM_BLOCK, (r + 1) * GM_BLOCK)
            mixed = jnp.dot(wg, vnb[rows], preferred_element_type=F32) + bias
            y = u_ref[rows, cols].astype(F32) * mixed * sz_ref[rows, cols].astype(F32)
            y_ref[rows, cols] = y.astype(y_ref.dtype)


def _gm_mix(ug, sz, wm, bt, ln_g, ln_b, *, tm, emit_vn):
    m = sz.shape[0]
    row_blk = lambda c: pl.BlockSpec((tm, GM_WIDTH), lambda i, c=c: (i, c))
    full = lambda shape: pl.BlockSpec(shape, lambda i: (0,) * len(shape))
    out_shape = [jax.ShapeDtypeStruct((m, GM_WIDTH), BF16)]
    out_specs = [row_blk(0)]
    if emit_vn:
        out_shape.append(jax.ShapeDtypeStruct((m, GM_WIDTH), F32))
        out_specs.append(row_blk(0))
    return pl.pallas_call(
        functools.partial(_gm_mix_kernel, emit_vn=emit_vn),
        grid=(m // tm,),
        in_specs=[row_blk(0), row_blk(1), row_blk(0),
                  full((GM_GROUPS, GM_BLOCK, GM_BLOCK)), full((GM_BLOCK, GM_GROUPS)),
                  full((1, GM_WIDTH)), full((1, GM_WIDTH))],
        out_specs=out_specs,
        out_shape=out_shape,
        compiler_params=_params(1),
        name="gm_mix",
    )(ug, ug, sz, wm, bt, ln_g.reshape(1, GM_WIDTH), ln_b.reshape(1, GM_WIDTH))


SB_BLOCK = 256
SB_TQ = 256
SB_HEADS_PER_STEP = 2
SB_DEAD_LOG = -105.0


def _sb_chunk(s, strict, carry, tri):
    r = s.shape[0]
    bw = tri.shape[0]
    nb = s.shape[1] // bw
    sbs = [s[:, b * bw:(b + 1) * bw] for b in range(nb)]
    parts = []
    for b in range(nb):
        lf = jnp.minimum(-sbs[b], 0.0) - jnp.log(1.0 + jnp.exp(-jnp.abs(sbs[b])))
        if strict is not None and b == nb - 1:
            lf = jnp.where(strict, lf, 0.0)
        hi = lf.astype(BF16)
        parts += [hi, (lf - hi.astype(F32)).astype(BF16)]
    c_all = jnp.dot(jnp.concatenate(parts, axis=0), tri, preferred_element_type=F32)
    ps = [None] * nb
    for b in reversed(range(nb)):
        c = c_all[2 * b * r:(2 * b + 1) * r] + c_all[(2 * b + 1) * r:(2 * b + 2) * r]
        p = jnp.exp(sbs[b] + (c + carry))
        if strict is not None and b == nb - 1:
            p = jnp.where(strict, p, 0.0)
        ps[b] = p.astype(BF16)
        carry = carry + c[:, 0:1]
    return ps, carry


def _qk(q, k):
    return lax.dot_general(q, k, (((1,), (1,)), ((), ())), preferred_element_type=F32)


def _pv(ps, v):
    bw = ps[0].shape[1]
    acc = None
    for b, p in enumerate(ps):
        d = jnp.dot(p, v[b * bw:(b + 1) * bw], preferred_element_type=F32)
        acc = d if acc is None else acc + d
    return acc


def _head_cols(h):
    return slice(h * SB_HEAD_DIM, (h + 1) * SB_HEAD_DIM)


def _sb_prompt_kernel(q_ref, k_ref, v_ref, sz_ref, tri_ref, o_ref):
    tq = SB_TQ
    heads = range(SB_HEADS_PER_STEP)
    kb_ref, vb_ref = k_ref.at[0], v_ref.at[0]
    kmax = []
    for h in heads:
        kf = kb_ref[:, _head_cols(h)].astype(F32)
        kmax.append(jnp.sqrt(jnp.max(jnp.sum(kf * kf, axis=-1, keepdims=True), axis=0, keepdims=True)))

    tri = tri_ref[...]
    row = lax.broadcasted_iota(jnp.int32, (tq, SB_BLOCK), 0)
    col = lax.broadcasted_iota(jnp.int32, (tq, SB_BLOCK), 1)
    strict = col < row
    zero_carry = jnp.zeros((tq, 1), F32)

    def chunk(h, q, k0, width, mask, carry):
        s = _qk(q, kb_ref[pl.ds(k0, width), _head_cols(h)])
        ps, carry = _sb_chunk(s, mask, carry, tri)
        return _pv(ps, vb_ref[pl.ds(k0, width), _head_cols(h)]), carry

    def store(h, rows, acc):
        o_ref[0, rows, _head_cols(h)] = (acc * sz_ref[0, rows, _head_cols(h)].astype(F32)).astype(o_ref.dtype)

    for h in heads:
        acc, _ = chunk(h, q_ref[0, 0:tq, _head_cols(h)], 0, tq, strict, zero_carry)
        store(h, slice(0, tq), acc)

    def q_block(qi, _):
        rows = pl.ds(pl.multiple_of(qi * tq, tq), tq)
        qs = [q_ref[0, rows, _head_cols(h)] for h in heads]
        bounds = []
        for h in heads:
            qf = qs[h].astype(F32)
            bounds.append(jnp.sqrt(jnp.sum(qf * qf, axis=-1, keepdims=True)) * kmax[h] * 1.001)

        def alive(carries):
            worst = carries[0] + bounds[0]
            for h in heads[1:]:
                worst = jnp.maximum(worst, carries[h] + bounds[h])
            return (jnp.max(worst) > SB_DEAD_LOG).astype(jnp.int32)

        k0 = pl.multiple_of((qi - 1) * SB_BLOCK, SB_BLOCK)
        first = [chunk(h, qs[h], k0, SB_BLOCK + tq, strict, zero_carry) for h in heads]
        accs = tuple(f[0] for f in first)
        carries = tuple(f[1] for f in first)

        def cond(st):
            return jnp.logical_and(st[0] >= 0, st[1] > 0)

        def body(st):
            kb_idx, _, carries, accs = st
            k0 = pl.multiple_of(kb_idx * SB_BLOCK, SB_BLOCK)
            nxt = [chunk(h, qs[h], k0, SB_BLOCK, None, carries[h]) for h in heads]
            carries = tuple(n[1] for n in nxt)
            accs = tuple(a + n[0] for a, n in zip(accs, nxt))
            return kb_idx - 1, alive(carries), carries, accs

        st = lax.while_loop(cond, body, (qi - 2, alive(carries), carries, accs))
        for h in heads:
            store(h, rows, st[3][h])
        return 0

    lax.fori_loop(1, q_ref.shape[1] // tq, q_block, 0)


def _sb_prompt(q, k, v, sz, tri):
    b, l, _ = q.shape
    assert SB_TQ == SB_BLOCK and l % SB_TQ == 0
    gw = SB_HEADS_PER_STEP * SB_HEAD_DIM
    blk = pl.BlockSpec((1, l, gw), lambda bi, g: (bi, 0, g))
    return pl.pallas_call(
        _sb_prompt_kernel,
        grid=(b, SB_HEADS // SB_HEADS_PER_STEP),
        in_specs=[blk, blk, blk, blk, pl.BlockSpec((SB_BLOCK, SB_BLOCK), lambda bi, g: (0, 0))],
        out_specs=blk,
        out_shape=jax.ShapeDtypeStruct(q.shape, BF16),
        compiler_params=_params(2),
        name="sb_prompt",
    )(q, k, v, sz, tri)


def _sb_decode_kernel(q_ref, kn_ref, vn_ref, sz_ref, tri_ref, kc_ref, vc_ref, o_ref):
    nh = SB_HEADS
    past = kc_ref.shape[0] // nh
    tq = q_ref.shape[1]
    tri = tri_ref[...]
    row = lax.broadcasted_iota(jnp.int32, (tq, LANES), 0)
    col = lax.broadcasted_iota(jnp.int32, (tq, LANES), 1)
    pad = jnp.zeros((LANES - tq, SB_HEAD_DIM), BF16)
    for h in range(nh):
        cols = _head_cols(h)
        q = q_ref[0, :, cols]
        kn = jnp.concatenate([kn_ref[0, :, cols], pad], axis=0)
        vn = jnp.concatenate([vn_ref[0, :, cols], pad], axis=0)
        ps, carry = _sb_chunk(_qk(q, kn), col < row, jnp.zeros((tq, 1), F32), tri[:LANES, :LANES])
        acc = _pv(ps, vn)
        head_rows = pl.ds(h, past, stride=nh)
        ps, _ = _sb_chunk(_qk(q, kc_ref[head_rows, :].astype(BF16)), None, carry, tri)
        acc = acc + _pv(ps, vc_ref[head_rows, :].astype(BF16))
        o_ref[0, :, cols] = (acc * sz_ref[0, :, cols].astype(F32)).astype(o_ref.dtype)


def _sb_decode(q, k_new, v_new, k_cache, v_cache, sz, tri):
    b, t, w = q.shape
    _, p, nh, dh = k_cache.shape
    assert nh == SB_HEADS and dh == SB_HEAD_DIM
    blk = pl.BlockSpec((1, t, w), lambda bi: (bi, 0, 0))
    blk_cache = pl.BlockSpec((None, p * nh, dh), lambda bi: (bi, 0, 0))
    return pl.pallas_call(
        _sb_decode_kernel,
        grid=(b,),
        in_specs=[blk, blk, blk, blk, pl.BlockSpec((SB_BLOCK, SB_BLOCK), lambda bi: (0, 0)),
                  blk_cache, blk_cache],
        out_specs=blk,
        out_shape=jax.ShapeDtypeStruct(q.shape, BF16),
        compiler_params=_params(1),
        name="sb_decode",
    )(q, k_new, v_new, sz, tri, k_cache.reshape(b, p * nh, dh), v_cache.reshape(b, p * nh, dh))


def _chunk_causal(n):
    pos = jnp.arange(n)
    return (pos[None, :] // CHUNK) <= (pos[:, None] // CHUNK)


def _mm_tiles(m, k):
    return dict(tm=min(m, 1024), tn=MM_BLOCK_BYTES // (4 * k))


def _gm_layer(x, seq_len, norm_g, w_in, ln_g, ln_b, w_s, b_s, w_out, *, tm_mix, emit_vn):
    act_dtype = F32 if emit_vn else BF16
    m, d = x.shape
    h = _rmsnorm(x, norm_g, BF16, min(m, 1024))
    t_in = _mm_tiles(m, d)
    ug = _matmul(h, w_in, act_dtype, **t_in, n=2 * GM_WIDTH, act="gelu", name="gm_in_uv")
    sz = _matmul(h, w_in, act_dtype, **t_in, col0=2 * GM_WIDTH, n=GM_WIDTH, act="silu", name="gm_in_z")
    blk = min(seq_len, GM_BLOCK)
    w = w_s[:, :blk, :blk] * _chunk_causal(blk).astype(w_s.dtype)
    reps = GM_BLOCK // blk
    wm = jnp.einsum("ab,gts->gatbs", jnp.eye(reps, dtype=w.dtype), w).reshape(
        GM_GROUPS, GM_BLOCK, GM_BLOCK).astype(BF16)
    bt = jnp.tile(b_s[:, :blk].T, (reps, 1))
    outs = _gm_mix(ug, sz, wm, bt, ln_g, ln_b, tm=tm_mix, emit_vn=emit_vn)
    x_new = _matmul(outs[0], w_out, F32, **_mm_tiles(m, GM_WIDTH), res=x, name="gm_out")
    return x_new, (outs[1] if emit_vn else None)


def _sb_in(x, norm_g, w_in):
    m, d = x.shape
    h = _rmsnorm(x, norm_g, BF16, min(m, 1024))
    t = _mm_tiles(m, d)
    w = SB_WIDTH
    q = _matmul(h, w_in, BF16, **t, n=w, scale=SB_HEAD_DIM ** -0.5, name="sb_in_q")
    k, kb = _matmul_heads(h, w_in, tm=min(m, 1024), col0=w, name="sb_in_k")
    v, vb = _matmul_heads(h, w_in, tm=min(m, 1024), col0=2 * w, name="sb_in_v")
    sz = _matmul(h, w_in, BF16, **t, col0=3 * w, n=w, act="silu", name="sb_in_z")
    return q, k, v, kb, vb, sz


def kernel(x_prompt, x_sample, cache_sb_k, cache_sb_v, norm_g, final_norm_g, gm_w_in, gm_ln_g, gm_ln_b,
           gm_w_s, gm_b_s, gm_w_out, sb_w_in, sb_w_out):
    bsz, seq, d = x_prompt.shape
    dbsz, dseq, _ = x_sample.shape
    xp = x_prompt.reshape(bsz * seq, d)
    xs = x_sample.reshape(dbsz * dseq, d)

    tt = (jnp.arange(SB_BLOCK)[:, None] >= jnp.arange(SB_BLOCK)[None, :]).astype(BF16)

    gm_v_rows, kp_rows, vp_rows, ks_rows, vs_rows = [], [], [], [], []
    for i in range(DEPTH):
        j = i // N_MIXERS
        if i % N_MIXERS == 0:
            common = (norm_g[i], gm_w_in[j], gm_ln_g[j], gm_ln_b[j], gm_w_s[j], gm_b_s[j], gm_w_out[j])
            xp, _ = _gm_layer(xp, seq, *common, tm_mix=256, emit_vn=False)
            xs, v_new = _gm_layer(xs, dseq, *common, tm_mix=dbsz * dseq, emit_vn=True)
            gm_v_rows.append(v_new.reshape(dbsz, dseq, GM_WIDTH))
        else:
            last = i == DEPTH - 1
            w_out = sb_w_out[j].astype(BF16) if last else sb_w_out[j]

            def out_proj(y, x):
                m = x.shape[0]
                if last:
                    return _matmul_res_norm(y, w_out, x, final_norm_g, tm=min(m, 512), name="sb_out_norm")
                return _matmul(y, w_out, F32, **_mm_tiles(m, SB_WIDTH), res=x, name="sb_out")

            q, k, v, kb, vb, sz = _sb_in(xp, norm_g[i], sb_w_in[j])
            shp = (bsz, seq, SB_WIDTH)
            y = _sb_prompt(q.reshape(shp), kb.reshape(shp), vb.reshape(shp), sz.reshape(shp), tt)
            xp = out_proj(y.reshape(bsz * seq, SB_WIDTH), xp)
            kp_rows.append(k.reshape(bsz, seq, SB_HEADS, SB_HEAD_DIM))
            vp_rows.append(v.reshape(bsz, seq, SB_HEADS, SB_HEAD_DIM))

            q, k, v, kb, vb, sz = _sb_in(xs, norm_g[i], sb_w_in[j])
            shp = (dbsz, dseq, SB_WIDTH)
            y = _sb_decode(q.reshape(shp), kb.reshape(shp), vb.reshape(shp), cache_sb_k[j], cache_sb_v[j],
                           sz.reshape(shp), tt)
            xs = out_proj(y.reshape(dbsz * dseq, SB_WIDTH), xs)
            ks_rows.append(k.reshape(dbsz, dseq, SB_HEADS, SB_HEAD_DIM))
            vs_rows.append(v.reshape(dbsz, dseq, SB_HEADS, SB_HEAD_DIM))

    if DEPTH % N_MIXERS == 0:
        y_prompt, y_sample = xp.reshape(bsz, seq, d), xs.reshape(dbsz, dseq, d)
    else:
        y_prompt = _rmsnorm(xp, final_norm_g, F32, 512).reshape(bsz, seq, d)
        y_sample = _rmsnorm(xs, final_norm_g, F32, dbsz * dseq).reshape(dbsz, dseq, d)
    return (y_prompt, y_sample, jnp.stack(kp_rows), jnp.stack(vp_rows), jnp.stack(ks_rows),
            jnp.stack(vs_rows), jnp.stack(gm_v_rows))
```

```python
import functools

import jax
import jax.numpy as jnp
from jax import lax
from jax.experimental import pallas as pl
from jax.experimental.pallas import tpu as pltpu

D_MODEL = 2048
DEPTH = 2
CHUNK = 64
N_MIXERS = 2
GM_WIDTH = 2 * D_MODEL
GM_BLOCK = 128
GM_GROUPS = 16
GM_GROUP_DIM = GM_WIDTH // GM_GROUPS
SB_HEADS = 16
SB_HEAD_DIM = D_MODEL // SB_HEADS
SB_WIDTH = SB_HEADS * SB_HEAD_DIM
NORM_EPS = 1e-6
LN_EPS = 1e-5

LANES = 128
VMEM_LIMIT_BYTES = 56 * 1024 * 1024
MM_VMEM_BUDGET = 40 * 1024 * 1024

F32 = jnp.float32
BF16 = jnp.bfloat16


def _params(n_grid_dims):
    return pltpu.CompilerParams(
        dimension_semantics=("arbitrary",) * n_grid_dims,
        vmem_limit_bytes=VMEM_LIMIT_BYTES,
    )


def _gelu(x):
    c = 0.7978845608028654
    return x * (0.5 * (1.0 + jnp.tanh(c * (x + 0.044715 * (x * x * x)))))


def _silu(x):
    return x * (1.0 / (1.0 + jnp.exp(-x)))


def _rmsnorm_kernel(x_ref, g_ref, o_ref):
    x = x_ref[...]
    ms = jnp.mean(x * x, axis=-1, keepdims=True)
    o_ref[...] = (x * lax.rsqrt(ms + NORM_EPS) * g_ref[...]).astype(o_ref.dtype)


def _rmsnorm(x, g, out_dtype, tm):
    m, d = x.shape
    return pl.pallas_call(
        _rmsnorm_kernel,
        grid=(m // tm,),
        in_specs=[pl.BlockSpec((tm, d), lambda i: (i, 0)),
                  pl.BlockSpec((1, d), lambda i: (0, 0))],
        out_specs=pl.BlockSpec((tm, d), lambda i: (i, 0)),
        out_shape=jax.ShapeDtypeStruct((m, d), out_dtype),
        compiler_params=_params(1),
        name="rmsnorm",
    )(x, g.reshape(1, d))


def _mm_kernel(*refs, act, scale, has_res):
    if has_res:
        x_ref, w_ref, r_ref, o_ref = refs
    else:
        x_ref, w_ref, o_ref = refs
    acc = jnp.dot(x_ref[...], w_ref[...].astype(BF16), preferred_element_type=F32)
    if act == "gelu":
        acc = _gelu(acc)
    elif act == "silu":
        acc = _silu(acc)
    if scale is not None:
        acc = acc * scale
    if has_res:
        acc = acc + r_ref[...]
    o_ref[...] = acc.astype(o_ref.dtype)


def _matmul(x, w, out_dtype, *, tm, tn, col0=0, n=None, act=None, scale=None, res=None, name="matmul"):
    m, k = x.shape
    n = w.shape[1] if n is None else n
    assert col0 % tn == 0 and n % tn == 0 and m % tm == 0
    jb = col0 // tn
    in_specs = [pl.BlockSpec((tm, k), lambda i, j: (i, 0)),
                pl.BlockSpec((k, tn), lambda i, j: (0, j + jb))]
    args = [x, w]
    if res is not None:
        in_specs.append(pl.BlockSpec((tm, tn), lambda i, j: (i, j)))
        args.append(res)
    return pl.pallas_call(
        functools.partial(_mm_kernel, act=act, scale=scale, has_res=res is not None),
        grid=(m // tm, n // tn),
        in_specs=in_specs,
        out_specs=pl.BlockSpec((tm, tn), lambda i, j: (i, j)),
        out_shape=jax.ShapeDtypeStruct((m, n), out_dtype),
        compiler_params=_params(2),
        name=name,
    )(*args)


HEAD_TILE = 8


def _mm_heads_kernel(x_ref, w_ref, of_ref, ob_ref):
    tm = x_ref.shape[0]
    acc = jnp.dot(x_ref[...], w_ref[...].astype(BF16), preferred_element_type=F32)
    ob_ref[...] = acc.astype(ob_ref.dtype)
    rows = of_ref.reshape(tm * HEAD_TILE, SB_HEAD_DIM)
    for h in range(HEAD_TILE):
        rows[pl.ds(h, tm, stride=HEAD_TILE), :] = acc[:, _head_cols(h)]


def _matmul_heads(x, w, *, tm, col0, name):
    m, k = x.shape
    tn = HEAD_TILE * SB_HEAD_DIM
    assert col0 % tn == 0 and m % tm == 0
    jb = col0 // tn
    return pl.pallas_call(
        _mm_heads_kernel,
        grid=(m // tm, SB_WIDTH // tn),
        in_specs=[pl.BlockSpec((tm, k), lambda i, j: (i, 0)),
                  pl.BlockSpec((k, tn), lambda i, j: (0, j + jb))],
        out_specs=[pl.BlockSpec((tm, HEAD_TILE, SB_HEAD_DIM), lambda i, j: (i, j, 0)),
                   pl.BlockSpec((tm, tn), lambda i, j: (i, j))],
        out_shape=[jax.ShapeDtypeStruct((m, SB_HEADS, SB_HEAD_DIM), F32),
                   jax.ShapeDtypeStruct((m, SB_WIDTH), BF16)],
        compiler_params=_params(2),
        name=name,
    )(x, w)


def _mm_res_norm_kernel(x_ref, w_ref, r_ref, g_ref, o_ref):
    acc = jnp.dot(x_ref[...], w_ref[...], preferred_element_type=F32) + r_ref[...]
    ms = jnp.mean(acc * acc, axis=-1, keepdims=True)
    o_ref[...] = acc * lax.rsqrt(ms + NORM_EPS) * g_ref[...]


def _matmul_res_norm(x, w, res, g, *, tm, name):
    m, k = x.shape
    n = w.shape[1]
    row = lambda cols: pl.BlockSpec((tm, cols), lambda i: (i, 0))
    return pl.pallas_call(
        _mm_res_norm_kernel,
        grid=(m // tm,),
        in_specs=[row(k), pl.BlockSpec((k, n), lambda i: (0, 0)), row(n),
                  pl.BlockSpec((1, n), lambda i: (0, 0))],
        out_specs=row(n),
        out_shape=jax.ShapeDtypeStruct((m, n), F32),
        compiler_params=_params(1),
        name=name,
    )(x, w, res, g.reshape(1, n))


def _gm_mix_kernel(u_ref, gv_ref, sz_ref, wm_ref, bt_ref, lg_ref, lb_ref, *out_refs, emit_vn):
    if emit_vn:
        y_ref, vn_ref = out_refs
    else:
        (y_ref,) = out_refs
    tm = u_ref.shape[0]
    inv_w = 1.0 / GM_WIDTH
    s1 = jnp.zeros((tm, 1), F32)
    for g in range(GM_GROUPS):
        cols = slice(g * GM_GROUP_DIM, (g + 1) * GM_GROUP_DIM)
        s1 = s1 + jnp.sum(gv_ref[:, cols].astype(F32), axis=-1, keepdims=True)
    mu = s1 * inv_w
    s2 = jnp.zeros((tm, 1), F32)
    for g in range(GM_GROUPS):
        cols = slice(g * GM_GROUP_DIM, (g + 1) * GM_GROUP_DIM)
        xc = gv_ref[:, cols].astype(F32) - mu
        s2 = s2 + jnp.sum(xc * xc, axis=-1, keepdims=True)
    rstd = lax.rsqrt(s2 * inv_w + LN_EPS)
    for g in range(GM_GROUPS):
        cols = slice(g * GM_GROUP_DIM, (g + 1) * GM_GROUP_DIM)
        vn = (gv_ref[:, cols].astype(F32) - mu) * rstd * lg_ref[:, cols] + lb_ref[:, cols]
        if emit_vn:
            vn_ref[:, cols] = vn
        vnb = vn.astype(BF16)
        wg = wm_ref[g]
        bias = bt_ref[:, g:g + 1]
        for r in range(tm // GM_BLOCK):
            rows = slice(r * GM_BLOCK, (r + 1) * GM_BLOCK)
            mixed = jnp.dot(wg, vnb[rows], preferred_element_type=F32) + bias
            y = u_ref[rows, cols].astype(F32) * mixed * sz_ref[rows, cols].astype(F32)
            y_ref[rows, cols] = y.astype(y_ref.dtype)


def _gm_mix(ug, sz, wm, bt, ln_g, ln_b, *, tm, emit_vn):
    m = sz.shape[0]
    row_blk = lambda c: pl.BlockSpec((tm, GM_WIDTH), lambda i, c=c: (i, c))
    full = lambda shape: pl.BlockSpec(shape, lambda i: (0,) * len(shape))
    out_shape = [jax.ShapeDtypeStruct((m, GM_WIDTH), BF16)]
    out_specs = [row_blk(0)]
    if emit_vn:
        out_shape.append(jax.ShapeDtypeStruct((m, GM_WIDTH), F32))
        out_specs.append(row_blk(0))
    return pl.pallas_call(
        functools.partial(_gm_mix_kernel, emit_vn=emit_vn),
        grid=(m // tm,),
        in_specs=[row_blk(0), row_blk(1), row_blk(0),
                  full((GM_GROUPS, GM_BLOCK, GM_BLOCK)), full((GM_BLOCK, GM_GROUPS)),
                  full((1, GM_WIDTH)), full((1, GM_WIDTH))],
        out_specs=out_specs,
        out_shape=out_shape,
        compiler_params=_params(1),
        name="gm_mix",
    )(ug, ug, sz, wm, bt, ln_g.reshape(1, GM_WIDTH), ln_b.reshape(1, GM_WIDTH))


SB_BLOCK = 256
SB_TQ = 256
SB_HEADS_PER_STEP = 2
SB_DEAD_LOG = -105.0


def _sb_chunk(s, strict, carry, tri):
    r = s.shape[0]
    bw = tri.shape[0]
    nb = s.shape[1] // bw
    sbs = [s[:, b * bw:(b + 1) * bw] for b in range(nb)]
    parts = []
    for b in range(nb):
        lf = jnp.minimum(-sbs[b], 0.0) - jnp.log(1.0 + jnp.exp(-jnp.abs(sbs[b])))
        if strict is not None and b == nb - 1:
            lf = jnp.where(strict, lf, 0.0)
        parts.append(lf.astype(BF16))
    c_all = jnp.dot(jnp.concatenate(parts, axis=0), tri, preferred_element_type=F32)
    ps = [None] * nb
    for b in reversed(range(nb)):
        c = c_all[b * r:(b + 1) * r]
        p = jnp.exp(sbs[b] + (c + carry))
        if strict is not None and b == nb - 1:
            p = jnp.where(strict, p, 0.0)
        ps[b] = p.astype(BF16)
        carry = carry + c[:, 0:1]
    return ps, carry


def _qk(q, k):
    return lax.dot_general(q, k, (((1,), (1,)), ((), ())), preferred_element_type=F32)


def _pv(ps, v):
    bw = ps[0].shape[1]
    acc = None
    for b, p in enumerate(ps):
        d = jnp.dot(p, v[b * bw:(b + 1) * bw], preferred_element_type=F32)
        acc = d if acc is None else acc + d
    return acc


def _head_cols(h):
    return slice(h * SB_HEAD_DIM, (h + 1) * SB_HEAD_DIM)


def _sb_prompt_kernel(q_ref, k_ref, v_ref, sz_ref, tri_ref, o_ref):
    tq = SB_TQ
    heads = range(SB_HEADS_PER_STEP)
    kb_ref, vb_ref = k_ref.at[0], v_ref.at[0]
    kmax = []
    for h in heads:
        kf = kb_ref[:, _head_cols(h)].astype(F32)
        kmax.append(jnp.sqrt(jnp.max(jnp.sum(kf * kf, axis=-1, keepdims=True), axis=0, keepdims=True)))

    tri = tri_ref[...]
    row = lax.broadcasted_iota(jnp.int32, (tq, SB_BLOCK), 0)
    col = lax.broadcasted_iota(jnp.int32, (tq, SB_BLOCK), 1)
    strict = col < row
    zero_carry = jnp.zeros((tq, 1), F32)

    def chunk(h, q, k0, width, mask, carry):
        s = _qk(q, kb_ref[pl.ds(k0, width), _head_cols(h)])
        ps, carry = _sb_chunk(s, mask, carry, tri)
        return _pv(ps, vb_ref[pl.ds(k0, width), _head_cols(h)]), carry

    def store(h, rows, acc):
        o_ref[0, rows, _head_cols(h)] = (acc * sz_ref[0, rows, _head_cols(h)].astype(F32)).astype(o_ref.dtype)

    for h in heads:
        acc, _ = chunk(h, q_ref[0, 0:tq, _head_cols(h)], 0, tq, strict, zero_carry)
        store(h, slice(0, tq), acc)

    def q_block(qi, _):
        rows = pl.ds(pl.multiple_of(qi * tq, tq), tq)
        qs = [q_ref[0, rows, _head_cols(h)] for h in heads]
        bounds = []
        for h in heads:
            qf = qs[h].astype(F32)
            bounds.append(jnp.sqrt(jnp.sum(qf * qf, axis=-1, keepdims=True)) * kmax[h] * 1.001)

        def alive(carries):
            worst = carries[0] + bounds[0]
            for h in heads[1:]:
                worst = jnp.maximum(worst, carries[h] + bounds[h])
            return (jnp.max(worst) > SB_DEAD_LOG).astype(jnp.int32)

        k0 = pl.multiple_of((qi - 1) * SB_BLOCK, SB_BLOCK)
        first = [chunk(h, qs[h], k0, SB_BLOCK + tq, strict, zero_carry) for h in heads]
        accs = tuple(f[0] for f in first)
        carries = tuple(f[1] for f in first)

        def cond(st):
            return jnp.logical_and(st[0] >= 0, st[1] > 0)

        def body(st):
            kb_idx, _, carries, accs = st
            k0 = pl.multiple_of(kb_idx * SB_BLOCK, SB_BLOCK)
            nxt = [chunk(h, qs[h], k0, SB_BLOCK, None, carries[h]) for h in heads]
            carries = tuple(n[1] for n in nxt)
            accs = tuple(a + n[0] for a, n in zip(accs, nxt))
            return kb_idx - 1, alive(carries), carries, accs

        st = lax.while_loop(cond, body, (qi - 2, alive(carries), carries, accs))
        for h in heads:
            store(h, rows, st[3][h])
        return 0

    lax.fori_loop(1, q_ref.shape[1] // tq, q_block, 0)


def _sb_prompt(q, k, v, sz, tri):
    b, l, _ = q.shape
    assert SB_TQ == SB_BLOCK and l % SB_TQ == 0
    gw = SB_HEADS_PER_STEP * SB_HEAD_DIM
    blk = pl.BlockSpec((1, l, gw), lambda bi, g: (bi, 0, g))
    return pl.pallas_call(
        _sb_prompt_kernel,
        grid=(b, SB_HEADS // SB_HEADS_PER_STEP),
        in_specs=[blk, blk, blk, blk, pl.BlockSpec((SB_BLOCK, SB_BLOCK), lambda bi, g: (0, 0))],
        out_specs=blk,
        out_shape=jax.ShapeDtypeStruct(q.shape, BF16),
        compiler_params=_params(2),
        name="sb_prompt",
    )(q, k, v, sz, tri)


def _sb_decode_kernel(q_ref, kn_ref, vn_ref, sz_ref, tri_ref, kc_ref, vc_ref, o_ref):
    nh = SB_HEADS
    past = kc_ref.shape[0] // nh
    tq = q_ref.shape[1]
    tri = tri_ref[...]
    row = lax.broadcasted_iota(jnp.int32, (tq, LANES), 0)
    col = lax.broadcasted_iota(jnp.int32, (tq, LANES), 1)
    pad = jnp.zeros((LANES - tq, SB_HEAD_DIM), BF16)
    for h in range(nh):
        cols = _head_cols(h)
        q = q_ref[0, :, cols]
        kn = jnp.concatenate([kn_ref[0, :, cols], pad], axis=0)
        vn = jnp.concatenate([vn_ref[0, :, cols], pad], axis=0)
        ps, carry = _sb_chunk(_qk(q, kn), col < row, jnp.zeros((tq, 1), F32), tri[:LANES, :LANES])
        acc = _pv(ps, vn)
        head_rows = pl.ds(h, past, stride=nh)
        ps, _ = _sb_chunk(_qk(q, kc_ref[head_rows, :].astype(BF16)), None, carry, tri)
        acc = acc + _pv(ps, vc_ref[head_rows, :].astype(BF16))
        o_ref[0, :, cols] = (acc * sz_ref[0, :, cols].astype(F32)).astype(o_ref.dtype)


def _sb_decode(q, k_new, v_new, k_cache, v_cache, sz, tri):
    b, t, w = q.shape
    _, p, nh, dh = k_cache.shape
    assert nh == SB_HEADS and dh == SB_HEAD_DIM
    blk = pl.BlockSpec((1, t, w), lambda bi: (bi, 0, 0))
    blk_cache = pl.BlockSpec((None, p * nh, dh), lambda bi: (bi, 0, 0))
    return pl.pallas_call(
        _sb_decode_kernel,
        grid=(b,),
        in_specs=[blk, blk, blk, blk, pl.BlockSpec((SB_BLOCK, SB_BLOCK), lambda bi: (0, 0)),
                  blk_cache, blk_cache],
        out_specs=blk,
        out_shape=jax.ShapeDtypeStruct(q.shape, BF16),
        compiler_params=_params(1),
        name="sb_decode",
    )(q, k_new, v_new, sz, tri, k_cache.reshape(b, p * nh, dh), v_cache.reshape(b, p * nh, dh))


def _chunk_causal(n):
    pos = jnp.arange(n)
    return (pos[None, :] // CHUNK) <= (pos[:, None] // CHUNK)


def _mm_tiles(m, k, w_itemsize=4):
    tm = min(m, 1024)
    for tn in (1024, 512, 256):
        if 2 * (tm * k * 2 + k * tn * w_itemsize + 2 * tm * tn * 4) <= MM_VMEM_BUDGET:
            return dict(tm=tm, tn=tn)
    raise ValueError("matmul blocks do not fit VMEM")


def _gm_layer(x, seq_len, norm_g, w_in, ln_g, ln_b, w_s, b_s, w_out, *, tm_mix, emit_vn):
    act_dtype = F32 if emit_vn else BF16
    m, d = x.shape
    h = _rmsnorm(x, norm_g, BF16, min(m, 1024))
    t_in = _mm_tiles(m, d)
    ug = _matmul(h, w_in, act_dtype, **t_in, n=2 * GM_WIDTH, act="gelu", name="gm_in_uv")
    sz = _matmul(h, w_in, act_dtype, **t_in, col0=2 * GM_WIDTH, n=GM_WIDTH, act="silu", name="gm_in_z")
    blk = min(seq_len, GM_BLOCK)
    w = w_s[:, :blk, :blk] * _chunk_causal(blk).astype(w_s.dtype)
    reps = GM_BLOCK // blk
    wm = jnp.einsum("ab,gts->gatbs", jnp.eye(reps, dtype=w.dtype), w).reshape(
        GM_GROUPS, GM_BLOCK, GM_BLOCK).astype(BF16)
    bt = jnp.tile(b_s[:, :blk].T, (reps, 1))
    outs = _gm_mix(ug, sz, wm, bt, ln_g, ln_b, tm=tm_mix, emit_vn=emit_vn)
    x_new = _matmul(outs[0], w_out, F32, **_mm_tiles(m, GM_WIDTH, w_out.dtype.itemsize), res=x, name="gm_out")
    return x_new, (outs[1] if emit_vn else None)


def _sb_in(x, norm_g, w_in):
    m, d = x.shape
    h = _rmsnorm(x, norm_g, BF16, min(m, 1024))
    t = _mm_tiles(m, d)
    w = SB_WIDTH
    q = _matmul(h, w_in, BF16, **t, n=w, scale=SB_HEAD_DIM ** -0.5, name="sb_in_q")
    k, kb = _matmul_heads(h, w_in, tm=min(m, 1024), col0=w, name="sb_in_k")
    v, vb = _matmul_heads(h, w_in, tm=min(m, 1024), col0=2 * w, name="sb_in_v")
    sz = _matmul(h, w_in, BF16, **t, col0=3 * w, n=w, act="silu", name="sb_in_z")
    return q, k, v, kb, vb, sz


def kernel(x_prompt, x_sample, cache_sb_k, cache_sb_v, norm_g, final_norm_g, gm_w_in, gm_ln_g, gm_ln_b,
           gm_w_s, gm_b_s, gm_w_out, sb_w_in, sb_w_out):
    bsz, seq, d = x_prompt.shape
    dbsz, dseq, _ = x_sample.shape
    xp = x_prompt.reshape(bsz * seq, d)
    xs = x_sample.reshape(dbsz * dseq, d)

    tt = (jnp.arange(SB_BLOCK)[:, None] >= jnp.arange(SB_BLOCK)[None, :]).astype(BF16)

    gm_v_rows, kp_rows, vp_rows, ks_rows, vs_rows = [], [], [], [], []
    for i in range(DEPTH):
        j = i // N_MIXERS
        if i % N_MIXERS == 0:
            common = (norm_g[i], gm_w_in[j], gm_ln_g[j], gm_ln_b[j], gm_w_s[j], gm_b_s[j],
                      gm_w_out[j].astype(BF16))
            xp, _ = _gm_layer(xp, seq, *common, tm_mix=256, emit_vn=False)
            xs, v_new = _gm_layer(xs, dseq, *common, tm_mix=dbsz * dseq, emit_vn=True)
            gm_v_rows.append(v_new.reshape(dbsz, dseq, GM_WIDTH))
        else:
            last = i == DEPTH - 1
            w_out = sb_w_out[j].astype(BF16) if last else sb_w_out[j]

            def out_proj(y, x):
                m = x.shape[0]
                if last:
                    return _matmul_res_norm(y, w_out, x, final_norm_g, tm=min(m, 512), name="sb_out_norm")
                return _matmul(y, w_out, F32, **_mm_tiles(m, SB_WIDTH), res=x, name="sb_out")

            q, k, v, kb, vb, sz = _sb_in(xp, norm_g[i], sb_w_in[j])
            shp = (bsz, seq, SB_WIDTH)
            y = _sb_prompt(q.reshape(shp), kb.reshape(shp), vb.reshape(shp), sz.reshape(shp), tt)
            xp = out_proj(y.reshape(bsz * seq, SB_WIDTH), xp)
            kp_rows.append(k.reshape(bsz, seq, SB_HEADS, SB_HEAD_DIM))
            vp_rows.append(v.reshape(bsz, seq, SB_HEADS, SB_HEAD_DIM))

            q, k, v, kb, vb, sz = _sb_in(xs, norm_g[i], sb_w_in[j])
            shp = (dbsz, dseq, SB_WIDTH)
            y = _sb_decode(q.reshape(shp), kb.reshape(shp), vb.reshape(shp), cache_sb_k[j], cache_sb_v[j],
                           sz.reshape(shp), tt)
            xs = out_proj(y.reshape(dbsz * dseq, SB_WIDTH), xs)
            ks_rows.append(k.reshape(dbsz, dseq, SB_HEADS, SB_HEAD_DIM))
            vs_rows.append(v.reshape(dbsz, dseq, SB_HEADS, SB_HEAD_DIM))

    if DEPTH % N_MIXERS == 0:
        y_prompt, y_sample = xp.reshape(bsz, seq, d), xs.reshape(dbsz, dseq, d)
    else:
        y_prompt = _rmsnorm(xp, final_norm_g, F32, 512).reshape(bsz, seq, d)
        y_sample = _rmsnorm(xs, final_norm_g, F32, dbsz * dseq).reshape(dbsz, dseq, d)
    return (y_prompt, y_sample, jnp.stack(kp_rows), jnp.stack(vp_rows), jnp.stack(ks_rows),
            jnp.stack(vs_rows), jnp.stack(gm_v_rows))
```

```python
import functools

import jax
import jax.numpy as jnp
from jax import lax
from jax.experimental import pallas as pl
from jax.experimental.pallas import tpu as pltpu

D_MODEL = 2048
DEPTH = 2
CHUNK = 64
N_MIXERS = 2
GM_WIDTH = 2 * D_MODEL
GM_BLOCK = 128
GM_GROUPS = 16
GM_GROUP_DIM = GM_WIDTH // GM_GROUPS
SB_HEADS = 16
SB_HEAD_DIM = D_MODEL // SB_HEADS
SB_WIDTH = SB_HEADS * SB_HEAD_DIM
NORM_EPS = 1e-6
LN_EPS = 1e-5

LANES = 128
VMEM_LIMIT_BYTES = 56 * 1024 * 1024
MM_VMEM_BUDGET = 40 * 1024 * 1024
F32 = jnp.float32
BF16 = jnp.bfloat16


def _params(n_grid_dims):
    return pltpu.CompilerParams(
        dimension_semantics=("arbitrary",) * n_grid_dims,
        vmem_limit_bytes=VMEM_LIMIT_BYTES,
    )


def _gelu(x):
    c = 0.7978845608028654
    return x * (0.5 * (1.0 + jnp.tanh(c * (x + 0.044715 * (x * x * x)))))


def _silu(x):
    return x * (1.0 / (1.0 + jnp.exp(-x)))


def _rmsnorm_kernel(x_ref, g_ref, o_ref):
    x = x_ref[...]
    ms = jnp.mean(x * x, axis=-1, keepdims=True)
    o_ref[...] = (x * lax.rsqrt(ms + NORM_EPS) * g_ref[...]).astype(o_ref.dtype)


def _rmsnorm(x, g, out_dtype, tm):
    m, d = x.shape
    return pl.pallas_call(
        _rmsnorm_kernel,
        grid=(m // tm,),
        in_specs=[pl.BlockSpec((tm, d), lambda i: (i, 0)),
                  pl.BlockSpec((1, d), lambda i: (0, 0))],
        out_specs=pl.BlockSpec((tm, d), lambda i: (i, 0)),
        out_shape=jax.ShapeDtypeStruct((m, d), out_dtype),
        compiler_params=_params(1),
        name="rmsnorm",
    )(x, g.reshape(1, d))


MM_TM = 1024
HEAD_TILE = 8


def _row_groups(ms, tm_max):
    tms = [min(m, tm_max) for m in ms]
    counts = [m // tm for m, tm in zip(ms, tms)]
    assert all(m == tm * c for m, tm, c in zip(ms, tms, counts))
    starts = [sum(counts[:a]) for a in range(len(ms))]
    return tms, counts, starts


def _row_index(i, start, count):
    return jnp.clip(i - start, 0, count - 1)


def _on_group(i, a, counts, starts, fn):
    if len(counts) == 1:
        fn()
    else:
        pl.when(jnp.logical_and(i >= starts[a], i < starts[a] + counts[a]))(fn)


def _mm_kernel(*refs, groups, act, scale, has_res):
    n_arr = len(groups[0])
    x_refs, w_ref = refs[:n_arr], refs[n_arr]
    r_refs = refs[n_arr + 1:2 * n_arr + 1] if has_res else None
    o_refs = refs[-n_arr:]
    i = pl.program_id(1)

    def tile(a):
        acc = jnp.dot(x_refs[a][...], w_ref[...].astype(BF16), preferred_element_type=F32)
        if act == "gelu":
            acc = _gelu(acc)
        elif act == "silu":
            acc = _silu(acc)
        if scale is not None:
            acc = acc * scale
        if has_res:
            acc = acc + r_refs[a][...]
        o_refs[a][...] = acc.astype(o_refs[a].dtype)

    for a in range(n_arr):
        _on_group(i, a, *groups, functools.partial(tile, a))


def _matmul(xs, w, out_dtype, *, tn, col0=0, n=None, act=None, scale=None, res=None, name="matmul"):
    k = xs[0].shape[1]
    n = w.shape[1] if n is None else n
    assert col0 % tn == 0 and n % tn == 0
    jb = col0 // tn
    tms, counts, starts = _row_groups([x.shape[0] for x in xs], MM_TM)
    in_specs = [pl.BlockSpec((tm, k), lambda j, i, s=s, c=c: (_row_index(i, s, c), 0))
                for tm, c, s in zip(tms, counts, starts)]
    in_specs.append(pl.BlockSpec((k, tn), lambda j, i: (0, j + jb)))
    row_col = [pl.BlockSpec((tm, tn), lambda j, i, s=s, c=c: (_row_index(i, s, c), j))
               for tm, c, s in zip(tms, counts, starts)]
    args = [*xs, w]
    if res is not None:
        in_specs += row_col
        args += list(res)
    return pl.pallas_call(
        functools.partial(_mm_kernel, groups=(counts, starts), act=act, scale=scale, has_res=res is not None),
        grid=(n // tn, sum(counts)),
        in_specs=in_specs,
        out_specs=row_col,
        out_shape=[jax.ShapeDtypeStruct((x.shape[0], n), out_dtype) for x in xs],
        compiler_params=_params(2),
        name=name,
    )(*args)


def _mm_heads_kernel(*refs, groups):
    n_arr = len(groups[0])
    x_refs, w_ref = refs[:n_arr], refs[n_arr]
    of_refs, ob_refs = refs[n_arr + 1:2 * n_arr + 1], refs[2 * n_arr + 1:]
    i = pl.program_id(1)

    def tile(a):
        tm = x_refs[a].shape[0]
        acc = jnp.dot(x_refs[a][...], w_ref[...].astype(BF16), preferred_element_type=F32)
        ob_refs[a][...] = acc.astype(ob_refs[a].dtype)
        rows = of_refs[a].reshape(tm * HEAD_TILE, SB_HEAD_DIM)
        for h in range(HEAD_TILE):
            rows[pl.ds(h, tm, stride=HEAD_TILE), :] = acc[:, _head_cols(h)]

    for a in range(n_arr):
        _on_group(i, a, *groups, functools.partial(tile, a))


def _matmul_heads(xs, w, *, col0, name):
    k = xs[0].shape[1]
    tn = HEAD_TILE * SB_HEAD_DIM
    assert col0 % tn == 0
    jb = col0 // tn
    tms, counts, starts = _row_groups([x.shape[0] for x in xs], MM_TM)
    groups = list(zip(tms, counts, starts))
    in_specs = [pl.BlockSpec((tm, k), lambda j, i, s=s, c=c: (_row_index(i, s, c), 0)) for tm, c, s in groups]
    in_specs.append(pl.BlockSpec((k, tn), lambda j, i: (0, j + jb)))
    out_specs = [pl.BlockSpec((tm, HEAD_TILE, SB_HEAD_DIM), lambda j, i, s=s, c=c: (_row_index(i, s, c), j, 0))
                 for tm, c, s in groups]
    out_specs += [pl.BlockSpec((tm, tn), lambda j, i, s=s, c=c: (_row_index(i, s, c), j)) for tm, c, s in groups]
    outs = pl.pallas_call(
        functools.partial(_mm_heads_kernel, groups=(counts, starts)),
        grid=(SB_WIDTH // tn, sum(counts)),
        in_specs=in_specs,
        out_specs=out_specs,
        out_shape=[jax.ShapeDtypeStruct((x.shape[0], SB_HEADS, SB_HEAD_DIM), F32) for x in xs]
        + [jax.ShapeDtypeStruct((x.shape[0], SB_WIDTH), BF16) for x in xs],
        compiler_params=_params(2),
        name=name,
    )(*xs, w)
    return outs[:len(xs)], outs[len(xs):]


def _mm_res_norm_kernel(*refs, groups):
    n_arr = len(groups[0])
    x_refs, w_ref = refs[:n_arr], refs[n_arr]
    r_refs, g_ref, o_refs = refs[n_arr + 1:2 * n_arr + 1], refs[2 * n_arr + 1], refs[2 * n_arr + 2:]
    i = pl.program_id(0)

    def tile(a):
        acc = jnp.dot(x_refs[a][...], w_ref[...], preferred_element_type=F32) + r_refs[a][...]
        ms = jnp.mean(acc * acc, axis=-1, keepdims=True)
        o_refs[a][...] = acc * lax.rsqrt(ms + NORM_EPS) * g_ref[...]

    for a in range(n_arr):
        _on_group(i, a, *groups, functools.partial(tile, a))


def _matmul_res_norm(xs, w, res, g, *, tm, name):
    k, n = w.shape
    tms, counts, starts = _row_groups([x.shape[0] for x in xs], tm)
    row = lambda cols: [pl.BlockSpec((t, cols), lambda i, s=s, c=c: (_row_index(i, s, c), 0))
                        for t, c, s in zip(tms, counts, starts)]
    return pl.pallas_call(
        functools.partial(_mm_res_norm_kernel, groups=(counts, starts)),
        grid=(sum(counts),),
        in_specs=[*row(k), pl.BlockSpec((k, n), lambda i: (0, 0)), *row(n),
                  pl.BlockSpec((1, n), lambda i: (0, 0))],
        out_specs=row(n),
        out_shape=[jax.ShapeDtypeStruct((x.shape[0], n), F32) for x in xs],
        compiler_params=_params(1),
        name=name,
    )(*xs, w, *res, g.reshape(1, n))


def _gm_mix_kernel(u_ref, gv_ref, sz_ref, wm_ref, bt_ref, lg_ref, lb_ref, *out_refs, emit_vn):
    if emit_vn:
        y_ref, vn_ref = out_refs
    else:
        (y_ref,) = out_refs
    tm = u_ref.shape[0]
    inv_w = 1.0 / GM_WIDTH
    s1 = jnp.zeros((tm, 1), F32)
    for g in range(GM_GROUPS):
        cols = slice(g * GM_GROUP_DIM, (g + 1) * GM_GROUP_DIM)
        s1 = s1 + jnp.sum(gv_ref[:, cols].astype(F32), axis=-1, keepdims=True)
    mu = s1 * inv_w
    s2 = jnp.zeros((tm, 1), F32)
    for g in range(GM_GROUPS):
        cols = slice(g * GM_GROUP_DIM, (g + 1) * GM_GROUP_DIM)
        xc = gv_ref[:, cols].astype(F32) - mu
        s2 = s2 + jnp.sum(xc * xc, axis=-1, keepdims=True)
    rstd = lax.rsqrt(s2 * inv_w + LN_EPS)
    for g in range(GM_GROUPS):
        cols = slice(g * GM_GROUP_DIM, (g + 1) * GM_GROUP_DIM)
        vn = (gv_ref[:, cols].astype(F32) - mu) * rstd * lg_ref[:, cols] + lb_ref[:, cols]
        if emit_vn:
            vn_ref[:, cols] = vn
        vnb = vn.astype(BF16)
        wg = wm_ref[g]
        bias = bt_ref[:, g:g + 1]
        for r in range(tm // GM_BLOCK):
            rows = slice(r * GM_BLOCK, (r + 1) * GM_BLOCK)
            mixed = jnp.dot(wg, vnb[rows], preferred_element_type=F32) + bias
            y = u_ref[rows, cols].astype(F32) * mixed * sz_ref[rows, cols].astype(F32)
            y_ref[rows, cols] = y.astype(y_ref.dtype)


def _gm_mix(ug, sz, wm, bt, ln_g, ln_b, *, tm, emit_vn):
    m = sz.shape[0]
    row_blk = lambda c: pl.BlockSpec((tm, GM_WIDTH), lambda i, c=c: (i, c))
    full = lambda shape: pl.BlockSpec(shape, lambda i: (0,) * len(shape))
    out_shape = [jax.ShapeDtypeStruct((m, GM_WIDTH), BF16)]
    out_specs = [row_blk(0)]
    if emit_vn:
        out_shape.append(jax.ShapeDtypeStruct((m, GM_WIDTH), F32))
        out_specs.append(row_blk(0))
    return pl.pallas_call(
        functools.partial(_gm_mix_kernel, emit_vn=emit_vn),
        grid=(m // tm,),
        in_specs=[row_blk(0), row_blk(1), row_blk(0),
                  full((GM_GROUPS, GM_BLOCK, GM_BLOCK)), full((GM_BLOCK, GM_GROUPS)),
                  full((1, GM_WIDTH)), full((1, GM_WIDTH))],
        out_specs=out_specs,
        out_shape=out_shape,
        compiler_params=_params(1),
        name="gm_mix",
    )(ug, ug, sz, wm, bt, ln_g.reshape(1, GM_WIDTH), ln_b.reshape(1, GM_WIDTH))


SB_BLOCK = 256
SB_TQ = 256
SB_HEADS_PER_STEP = 2
SB_DEAD_LOG = -105.0


def _sb_chunk(s, strict, carry, tri):
    r = s.shape[0]
    bw = tri.shape[0]
    nb = s.shape[1] // bw
    sbs = [s[:, b * bw:(b + 1) * bw] for b in range(nb)]
    parts = []
    for b in range(nb):
        lf = jnp.minimum(-sbs[b], 0.0) - jnp.log(1.0 + jnp.exp(-jnp.abs(sbs[b])))
        if strict is not None and b == nb - 1:
            lf = jnp.where(strict, lf, 0.0)
        parts.append(lf.astype(BF16))
    c_all = jnp.dot(jnp.concatenate(parts, axis=0), tri, preferred_element_type=F32)
    ps = [None] * nb
    for b in reversed(range(nb)):
        c = c_all[b * r:(b + 1) * r]
        p = jnp.exp(sbs[b] + (c + carry))
        if strict is not None and b == nb - 1:
            p = jnp.where(strict, p, 0.0)
        ps[b] = p.astype(BF16)
        carry = carry + c[:, 0:1]
    return ps, carry


def _qk(q, k):
    return lax.dot_general(q, k, (((1,), (1,)), ((), ())), preferred_element_type=F32)


def _pv(ps, v):
    bw = ps[0].shape[1]
    acc = None
    for b, p in enumerate(ps):
        d = jnp.dot(p, v[b * bw:(b + 1) * bw], preferred_element_type=F32)
        acc = d if acc is None else acc + d
    return acc


def _head_cols(h):
    return slice(h * SB_HEAD_DIM, (h + 1) * SB_HEAD_DIM)


def _sb_prompt_kernel(q_ref, k_ref, v_ref, sz_ref, tri_ref, o_ref):
    tq = SB_TQ
    heads = range(SB_HEADS_PER_STEP)
    kb_ref, vb_ref = k_ref.at[0], v_ref.at[0]
    kmax = []
    for h in heads:
        kf = kb_ref[:, _head_cols(h)].astype(F32)
        kmax.append(jnp.sqrt(jnp.max(jnp.sum(kf * kf, axis=-1, keepdims=True), axis=0, keepdims=True)))

    tri = tri_ref[...]
    row = lax.broadcasted_iota(jnp.int32, (tq, SB_BLOCK), 0)
    col = lax.broadcasted_iota(jnp.int32, (tq, SB_BLOCK), 1)
    strict = col < row
    zero_carry = jnp.zeros((tq, 1), F32)

    def chunk(h, q, k0, width, mask, carry):
        s = _qk(q, kb_ref[pl.ds(k0, width), _head_cols(h)])
        ps, carry = _sb_chunk(s, mask, carry, tri)
        return _pv(ps, vb_ref[pl.ds(k0, width), _head_cols(h)]), carry

    def store(h, rows, acc):
        o_ref[0, rows, _head_cols(h)] = (acc * sz_ref[0, rows, _head_cols(h)].astype(F32)).astype(o_ref.dtype)

    for h in heads:
        acc, _ = chunk(h, q_ref[0, 0:tq, _head_cols(h)], 0, tq, strict, zero_carry)
        store(h, slice(0, tq), acc)

    def q_block(qi, _):
        rows = pl.ds(pl.multiple_of(qi * tq, tq), tq)
        qs = [q_ref[0, rows, _head_cols(h)] for h in heads]
        bounds = []
        for h in heads:
            qf = qs[h].astype(F32)
            bounds.append(jnp.sqrt(jnp.sum(qf * qf, axis=-1, keepdims=True)) * kmax[h] * 1.001)

        def alive(carries):
            worst = carries[0] + bounds[0]
            for h in heads[1:]:
                worst = jnp.maximum(worst, carries[h] + bounds[h])
            return (jnp.max(worst) > SB_DEAD_LOG).astype(jnp.int32)

        k0 = pl.multiple_of((qi - 1) * SB_BLOCK, SB_BLOCK)
        first = [chunk(h, qs[h], k0, SB_BLOCK + tq, strict, zero_carry) for h in heads]
        accs = tuple(f[0] for f in first)
        carries = tuple(f[1] for f in first)

        def cond(st):
            return jnp.logical_and(st[0] >= 0, st[1] > 0)

        def body(st):
            kb_idx, _, carries, accs = st
            k0 = pl.multiple_of(kb_idx * SB_BLOCK, SB_BLOCK)
            nxt = [chunk(h, qs[h], k0, SB_BLOCK, None, carries[h]) for h in heads]
            carries = tuple(n[1] for n in nxt)
            accs = tuple(a + n[0] for a, n in zip(accs, nxt))
            return kb_idx - 1, alive(carries), carries, accs

        st = lax.while_loop(cond, body, (qi - 2, alive(carries), carries, accs))
        for h in heads:
            store(h, rows, st[3][h])
        return 0

    lax.fori_loop(1, q_ref.shape[1] // tq, q_block, 0)


def _sb_prompt(q, k, v, sz, tri):
    b, l, _ = q.shape
    assert SB_TQ == SB_BLOCK and l % SB_TQ == 0
    gw = SB_HEADS_PER_STEP * SB_HEAD_DIM
    blk = pl.BlockSpec((1, l, gw), lambda bi, g: (bi, 0, g))
    return pl.pallas_call(
        _sb_prompt_kernel,
        grid=(b, SB_HEADS // SB_HEADS_PER_STEP),
        in_specs=[blk, blk, blk, blk, pl.BlockSpec((SB_BLOCK, SB_BLOCK), lambda bi, g: (0, 0))],
        out_specs=blk,
        out_shape=jax.ShapeDtypeStruct(q.shape, BF16),
        compiler_params=_params(2),
        name="sb_prompt",
    )(q, k, v, sz, tri)


def _sb_decode_kernel(q_ref, kn_ref, vn_ref, sz_ref, tri_ref, kc_ref, vc_ref, o_ref):
    nh = SB_HEADS
    past = kc_ref.shape[0] // nh
    tq = q_ref.shape[1]
    tri = tri_ref[...]
    row = lax.broadcasted_iota(jnp.int32, (tq, LANES), 0)
    col = lax.broadcasted_iota(jnp.int32, (tq, LANES), 1)
    pad = jnp.zeros((LANES - tq, SB_HEAD_DIM), BF16)
    for h in range(nh):
        cols = _head_cols(h)
        q = q_ref[0, :, cols]
        kn = jnp.concatenate([kn_ref[0, :, cols], pad], axis=0)
        vn = jnp.concatenate([vn_ref[0, :, cols], pad], axis=0)
        ps, carry = _sb_chunk(_qk(q, kn), col < row, jnp.zeros((tq, 1), F32), tri[:LANES, :LANES])
        acc = _pv(ps, vn)
        head_rows = pl.ds(h, past, stride=nh)
        ps, _ = _sb_chunk(_qk(q, kc_ref[head_rows, :].astype(BF16)), None, carry, tri)
        acc = acc + _pv(ps, vc_ref[head_rows, :].astype(BF16))
        o_ref[0, :, cols] = (acc * sz_ref[0, :, cols].astype(F32)).astype(o_ref.dtype)


def _sb_decode(q, k_new, v_new, k_cache, v_cache, sz, tri):
    b, t, w = q.shape
    _, p, nh, dh = k_cache.shape
    assert nh == SB_HEADS and dh == SB_HEAD_DIM
    blk = pl.BlockSpec((1, t, w), lambda bi: (bi, 0, 0))
    blk_cache = pl.BlockSpec((None, p * nh, dh), lambda bi: (bi, 0, 0))
    return pl.pallas_call(
        _sb_decode_kernel,
        grid=(b,),
        in_specs=[blk, blk, blk, blk, pl.BlockSpec((SB_BLOCK, SB_BLOCK), lambda bi: (0, 0)),
                  blk_cache, blk_cache],
        out_specs=blk,
        out_shape=jax.ShapeDtypeStruct(q.shape, BF16),
        compiler_params=_params(1),
        name="sb_decode",
    )(q, k_new, v_new, sz, tri, k_cache.reshape(b, p * nh, dh), v_cache.reshape(b, p * nh, dh))


def _chunk_causal(n):
    pos = jnp.arange(n)
    return (pos[None, :] // CHUNK) <= (pos[:, None] // CHUNK)


def _mm_tn(k, w_itemsize=4, out_bytes=8):
    for tn in (1024, 512, 256):
        if 2 * (MM_TM * k * 2 + k * tn * w_itemsize + MM_TM * tn * out_bytes) <= MM_VMEM_BUDGET:
            return tn
    raise ValueError("matmul blocks do not fit VMEM")


def _gm_mix_params(seq_len, w_s, b_s):
    blk = min(seq_len, GM_BLOCK)
    w = w_s[:, :blk, :blk] * _chunk_causal(blk).astype(w_s.dtype)
    reps = GM_BLOCK // blk
    wm = jnp.einsum("ab,gts->gatbs", jnp.eye(reps, dtype=w.dtype), w).reshape(
        GM_GROUPS, GM_BLOCK, GM_BLOCK).astype(BF16)
    return wm, jnp.tile(b_s[:, :blk].T, (reps, 1))


def _gm_layer(xs, seq_lens, norm_g, w_in, ln_g, ln_b, w_s, b_s, w_out):
    d = xs[0].shape[1]
    hs = [_rmsnorm(x, norm_g, BF16, min(x.shape[0], 1024)) for x in xs]
    tn = _mm_tn(d, out_bytes=2)
    ugs = _matmul(hs, w_in, BF16, tn=tn, n=2 * GM_WIDTH, act="gelu", name="gm_in_uv")
    szs = _matmul(hs, w_in, BF16, tn=tn, col0=2 * GM_WIDTH, n=GM_WIDTH, act="silu", name="gm_in_z")
    ys, v_rows = [], None
    for g, (ug, sz, seq_len) in enumerate(zip(ugs, szs, seq_lens)):
        emit_vn = g == len(xs) - 1
        outs = _gm_mix(ug, sz, *_gm_mix_params(seq_len, w_s, b_s), ln_g, ln_b,
                       tm=min(ug.shape[0], 256), emit_vn=emit_vn)
        ys.append(outs[0])
        v_rows = outs[1] if emit_vn else v_rows
    x_new = _matmul(ys, w_out, F32, tn=_mm_tn(GM_WIDTH, w_out.dtype.itemsize), res=xs, name="gm_out")
    return x_new, v_rows


def _sb_in(xs, norm_g, w_in):
    d = xs[0].shape[1]
    hs = [_rmsnorm(x, norm_g, BF16, min(x.shape[0], 1024)) for x in xs]
    tn = _mm_tn(d, out_bytes=2)
    w = SB_WIDTH
    q = _matmul(hs, w_in, BF16, tn=tn, n=w, scale=SB_HEAD_DIM ** -0.5, name="sb_in_q")
    k, kb = _matmul_heads(hs, w_in, col0=w, name="sb_in_k")
    v, vb = _matmul_heads(hs, w_in, col0=2 * w, name="sb_in_v")
    sz = _matmul(hs, w_in, BF16, tn=tn, col0=3 * w, n=w, act="silu", name="sb_in_z")
    return q, k, v, kb, vb, sz


def kernel(x_prompt, x_sample, cache_sb_k, cache_sb_v, norm_g, final_norm_g, gm_w_in, gm_ln_g, gm_ln_b,
           gm_w_s, gm_b_s, gm_w_out, sb_w_in, sb_w_out):
    bsz, seq, d = x_prompt.shape
    dbsz, dseq, _ = x_sample.shape
    xs = [x_prompt.reshape(bsz * seq, d), x_sample.reshape(dbsz * dseq, d)]

    tt = (jnp.arange(SB_BLOCK)[:, None] >= jnp.arange(SB_BLOCK)[None, :]).astype(BF16)

    gm_v_rows, kp_rows, vp_rows, ks_rows, vs_rows = [], [], [], [], []
    for i in range(DEPTH):
        j = i // N_MIXERS
        if i % N_MIXERS == 0:
            xs, v_new = _gm_layer(xs, (seq, dseq), norm_g[i], gm_w_in[j], gm_ln_g[j], gm_ln_b[j], gm_w_s[j],
                                  gm_b_s[j], gm_w_out[j].astype(BF16))
            gm_v_rows.append(v_new.reshape(dbsz, dseq, GM_WIDTH))
        else:
            q, k, v, kb, vb, sz = _sb_in(xs, norm_g[i], sb_w_in[j])
            shp_p, shp_s = (bsz, seq, SB_WIDTH), (dbsz, dseq, SB_WIDTH)
            yp = _sb_prompt(q[0].reshape(shp_p), kb[0].reshape(shp_p), vb[0].reshape(shp_p),
                            sz[0].reshape(shp_p), tt)
            ys = _sb_decode(q[1].reshape(shp_s), kb[1].reshape(shp_s), vb[1].reshape(shp_s), cache_sb_k[j],
                            cache_sb_v[j], sz[1].reshape(shp_s), tt)
            y = [yp.reshape(bsz * seq, SB_WIDTH), ys.reshape(dbsz * dseq, SB_WIDTH)]
            if i == DEPTH - 1:
                xs = _matmul_res_norm(y, sb_w_out[j].astype(BF16), xs, final_norm_g, tm=512, name="sb_out_norm")
            else:
                xs = _matmul(y, sb_w_out[j], F32, tn=_mm_tn(SB_WIDTH), res=xs, name="sb_out")
            kp_rows.append(k[0].reshape(bsz, seq, SB_HEADS, SB_HEAD_DIM))
            vp_rows.append(v[0].reshape(bsz, seq, SB_HEADS, SB_HEAD_DIM))
            ks_rows.append(k[1].reshape(dbsz, dseq, SB_HEADS, SB_HEAD_DIM))
            vs_rows.append(v[1].reshape(dbsz, dseq, SB_HEADS, SB_HEAD_DIM))

    if DEPTH % N_MIXERS != 0:
        xs = [_rmsnorm(x, final_norm_g, F32, min(x.shape[0], 512)) for x in xs]
    y_prompt, y_sample = xs[0].reshape(bsz, seq, d), xs[1].reshape(dbsz, dseq, d)
    return (y_prompt, y_sample, jnp.stack(kp_rows), jnp.stack(vp_rows), jnp.stack(ks_rows),
            jnp.stack(vs_rows), jnp.stack(gm_v_rows))
```

```python
import functools

import jax
import jax.numpy as jnp
from jax import lax
from jax.experimental import pallas as pl
from jax.experimental.pallas import tpu as pltpu

D_MODEL = 2048
DEPTH = 2
CHUNK = 64
N_MIXERS = 2
GM_WIDTH = 2 * D_MODEL
GM_BLOCK = 128
GM_GROUPS = 16
GM_GROUP_DIM = GM_WIDTH // GM_GROUPS
SB_HEADS = 16
SB_HEAD_DIM = D_MODEL // SB_HEADS
SB_WIDTH = SB_HEADS * SB_HEAD_DIM
NORM_EPS = 1e-6
LN_EPS = 1e-5

LANES = 128
VMEM_LIMIT_BYTES = 56 * 1024 * 1024
MM_VMEM_BUDGET = 40 * 1024 * 1024
F32 = jnp.float32
BF16 = jnp.bfloat16


def _params(n_grid_dims):
    return pltpu.CompilerParams(
        dimension_semantics=("arbitrary",) * n_grid_dims,
        vmem_limit_bytes=VMEM_LIMIT_BYTES,
    )


def _gelu(x):
    c = 0.7978845608028654
    return x * (0.5 * (1.0 + jnp.tanh(c * (x + 0.044715 * (x * x * x)))))


def _silu(x):
    return x * (1.0 / (1.0 + jnp.exp(-x)))


def _rmsnorm_kernel(x_ref, g_ref, o_ref):
    x = x_ref[...]
    ms = jnp.mean(x * x, axis=-1, keepdims=True)
    o_ref[...] = (x * lax.rsqrt(ms + NORM_EPS) * g_ref[...]).astype(o_ref.dtype)


def _rmsnorm(x, g, out_dtype, tm):
    m, d = x.shape
    return pl.pallas_call(
        _rmsnorm_kernel,
        grid=(m // tm,),
        in_specs=[pl.BlockSpec((tm, d), lambda i: (i, 0)),
                  pl.BlockSpec((1, d), lambda i: (0, 0))],
        out_specs=pl.BlockSpec((tm, d), lambda i: (i, 0)),
        out_shape=jax.ShapeDtypeStruct((m, d), out_dtype),
        compiler_params=_params(1),
        name="rmsnorm",
    )(x, g.reshape(1, d))


MM_TM = 1024
HEAD_TILE = 8


def _row_groups(ms, tm_max):
    tms = [min(m, tm_max) for m in ms]
    counts = [m // tm for m, tm in zip(ms, tms)]
    assert all(m == tm * c for m, tm, c in zip(ms, tms, counts))
    starts = [sum(counts[a + 1:]) for a in range(len(ms))]
    return tms, counts, starts


def _row_index(i, start, count):
    return jnp.clip(i - start, 0, count - 1)


def _on_group(i, a, counts, starts, fn):
    if len(counts) == 1:
        fn()
    else:
        pl.when(jnp.logical_and(i >= starts[a], i < starts[a] + counts[a]))(fn)


def _mm_kernel(*refs, groups, act, scale, has_res):
    n_arr = len(groups[0])
    x_refs, w_ref = refs[:n_arr], refs[n_arr]
    r_refs = refs[n_arr + 1:2 * n_arr + 1] if has_res else None
    o_refs = refs[-n_arr:]
    i = pl.program_id(1)

    def tile(a):
        acc = jnp.dot(x_refs[a][...], w_ref[...].astype(BF16), preferred_element_type=F32)
        if act == "gelu":
            acc = _gelu(acc)
        elif act == "silu":
            acc = _silu(acc)
        if scale is not None:
            acc = acc * scale
        if has_res:
            acc = acc + r_refs[a][...]
        o_refs[a][...] = acc.astype(o_refs[a].dtype)

    for a in range(n_arr):
        _on_group(i, a, *groups, functools.partial(tile, a))


def _matmul(xs, w, out_dtype, *, tn, col0=0, n=None, act=None, scale=None, res=None, name="matmul"):
    k = xs[0].shape[1]
    n = w.shape[1] if n is None else n
    assert col0 % tn == 0 and n % tn == 0
    jb = col0 // tn
    tms, counts, starts = _row_groups([x.shape[0] for x in xs], MM_TM)
    in_specs = [pl.BlockSpec((tm, k), lambda j, i, s=s, c=c: (_row_index(i, s, c), 0))
                for tm, c, s in zip(tms, counts, starts)]
    in_specs.append(pl.BlockSpec((k, tn), lambda j, i: (0, j + jb)))
    row_col = [pl.BlockSpec((tm, tn), lambda j, i, s=s, c=c: (_row_index(i, s, c), j))
               for tm, c, s in zip(tms, counts, starts)]
    args = [*xs, w]
    if res is not None:
        in_specs += row_col
        args += list(res)
    return pl.pallas_call(
        functools.partial(_mm_kernel, groups=(counts, starts), act=act, scale=scale, has_res=res is not None),
        grid=(n // tn, sum(counts)),
        in_specs=in_specs,
        out_specs=row_col,
        out_shape=[jax.ShapeDtypeStruct((x.shape[0], n), out_dtype) for x in xs],
        compiler_params=_params(2),
        name=name,
    )(*args)


def _mm_heads_kernel(*refs, groups):
    n_arr = len(groups[0])
    x_refs, w_ref = refs[:n_arr], refs[n_arr]
    of_refs, ob_refs = refs[n_arr + 1:2 * n_arr + 1], refs[2 * n_arr + 1:]
    i = pl.program_id(1)

    def tile(a):
        tm = x_refs[a].shape[0]
        acc = jnp.dot(x_refs[a][...], w_ref[...].astype(BF16), preferred_element_type=F32)
        ob_refs[a][...] = acc.astype(ob_refs[a].dtype)
        rows = of_refs[a].reshape(tm * HEAD_TILE, SB_HEAD_DIM)
        for h in range(HEAD_TILE):
            rows[pl.ds(h, tm, stride=HEAD_TILE), :] = acc[:, _head_cols(h)]

    for a in range(n_arr):
        _on_group(i, a, *groups, functools.partial(tile, a))


def _matmul_heads(xs, w, *, col0, name):
    k = xs[0].shape[1]
    tn = HEAD_TILE * SB_HEAD_DIM
    assert col0 % tn == 0
    jb = col0 // tn
    tms, counts, starts = _row_groups([x.shape[0] for x in xs], MM_TM)
    groups = list(zip(tms, counts, starts))
    in_specs = [pl.BlockSpec((tm, k), lambda j, i, s=s, c=c: (_row_index(i, s, c), 0)) for tm, c, s in groups]
    in_specs.append(pl.BlockSpec((k, tn), lambda j, i: (0, j + jb)))
    out_specs = [pl.BlockSpec((tm, HEAD_TILE, SB_HEAD_DIM), lambda j, i, s=s, c=c: (_row_index(i, s, c), j, 0))
                 for tm, c, s in groups]
    out_specs += [pl.BlockSpec((tm, tn), lambda j, i, s=s, c=c: (_row_index(i, s, c), j)) for tm, c, s in groups]
    outs = pl.pallas_call(
        functools.partial(_mm_heads_kernel, groups=(counts, starts)),
        grid=(SB_WIDTH // tn, sum(counts)),
        in_specs=in_specs,
        out_specs=out_specs,
        out_shape=[jax.ShapeDtypeStruct((x.shape[0], SB_HEADS, SB_HEAD_DIM), F32) for x in xs]
        + [jax.ShapeDtypeStruct((x.shape[0], SB_WIDTH), BF16) for x in xs],
        compiler_params=_params(2),
        name=name,
    )(*xs, w)
    return outs[:len(xs)], outs[len(xs):]


def _mm_res_norm_kernel(*refs, groups):
    n_arr = len(groups[0])
    x_refs, w_ref = refs[:n_arr], refs[n_arr]
    r_refs, g_ref, o_refs = refs[n_arr + 1:2 * n_arr + 1], refs[2 * n_arr + 1], refs[2 * n_arr + 2:]
    i = pl.program_id(0)

    def tile(a):
        acc = jnp.dot(x_refs[a][...], w_ref[...], preferred_element_type=F32) + r_refs[a][...]
        ms = jnp.mean(acc * acc, axis=-1, keepdims=True)
        o_refs[a][...] = acc * lax.rsqrt(ms + NORM_EPS) * g_ref[...]

    for a in range(n_arr):
        _on_group(i, a, *groups, functools.partial(tile, a))


def _matmul_res_norm(xs, w, res, g, *, tm, name):
    k, n = w.shape
    tms, counts, starts = _row_groups([x.shape[0] for x in xs], tm)
    row = lambda cols: [pl.BlockSpec((t, cols), lambda i, s=s, c=c: (_row_index(i, s, c), 0))
                        for t, c, s in zip(tms, counts, starts)]
    return pl.pallas_call(
        functools.partial(_mm_res_norm_kernel, groups=(counts, starts)),
        grid=(sum(counts),),
        in_specs=[*row(k), pl.BlockSpec((k, n), lambda i: (0, 0)), *row(n),
                  pl.BlockSpec((1, n), lambda i: (0, 0))],
        out_specs=row(n),
        out_shape=[jax.ShapeDtypeStruct((x.shape[0], n), F32) for x in xs],
        compiler_params=_params(1),
        name=name,
    )(*xs, w, *res, g.reshape(1, n))


def _gm_mix_kernel(u_ref, gv_ref, sz_ref, wm_ref, bt_ref, lg_ref, lb_ref, *out_refs, emit_vn):
    if emit_vn:
        y_ref, vn_ref = out_refs
    else:
        (y_ref,) = out_refs
    tm = u_ref.shape[0]
    inv_w = 1.0 / GM_WIDTH
    s1 = jnp.zeros((tm, 1), F32)
    for g in range(GM_GROUPS):
        cols = slice(g * GM_GROUP_DIM, (g + 1) * GM_GROUP_DIM)
        s1 = s1 + jnp.sum(gv_ref[:, cols].astype(F32), axis=-1, keepdims=True)
    mu = s1 * inv_w
    s2 = jnp.zeros((tm, 1), F32)
    for g in range(GM_GROUPS):
        cols = slice(g * GM_GROUP_DIM, (g + 1) * GM_GROUP_DIM)
        xc = gv_ref[:, cols].astype(F32) - mu
        s2 = s2 + jnp.sum(xc * xc, axis=-1, keepdims=True)
    rstd = lax.rsqrt(s2 * inv_w + LN_EPS)
    for g in range(GM_GROUPS):
        cols = slice(g * GM_GROUP_DIM, (g + 1) * GM_GROUP_DIM)
        vn = (gv_ref[:, cols].astype(F32) - mu) * rstd * lg_ref[:, cols] + lb_ref[:, cols]
        if emit_vn:
            vn_ref[:, cols] = vn
        vnb = vn.astype(BF16)
        wg = wm_ref[g]
        bias = bt_ref[:, g:g + 1]
        for r in range(tm // GM_BLOCK):
            rows = slice(r * GM_BLOCK, (r + 1) * GM_BLOCK)
            mixed = jnp.dot(wg, vnb[rows], preferred_element_type=F32) + bias
            y = u_ref[rows, cols].astype(F32) * mixed * sz_ref[rows, cols].astype(F32)
            y_ref[rows, cols] = y.astype(y_ref.dtype)


def _gm_mix(ug, sz, wm, bt, ln_g, ln_b, *, tm, emit_vn):
    m = sz.shape[0]
    row_blk = lambda c: pl.BlockSpec((tm, GM_WIDTH), lambda i, c=c: (i, c))
    full = lambda shape: pl.BlockSpec(shape, lambda i: (0,) * len(shape))
    out_shape = [jax.ShapeDtypeStruct((m, GM_WIDTH), BF16)]
    out_specs = [row_blk(0)]
    if emit_vn:
        out_shape.append(jax.ShapeDtypeStruct((m, GM_WIDTH), F32))
        out_specs.append(row_blk(0))
    return pl.pallas_call(
        functools.partial(_gm_mix_kernel, emit_vn=emit_vn),
        grid=(m // tm,),
        in_specs=[row_blk(0), row_blk(1), row_blk(0),
                  full((GM_GROUPS, GM_BLOCK, GM_BLOCK)), full((GM_BLOCK, GM_GROUPS)),
                  full((1, GM_WIDTH)), full((1, GM_WIDTH))],
        out_specs=out_specs,
        out_shape=out_shape,
        compiler_params=_params(1),
        name="gm_mix",
    )(ug, ug, sz, wm, bt, ln_g.reshape(1, GM_WIDTH), ln_b.reshape(1, GM_WIDTH))


SB_BLOCK = 256
SB_TQ = 256
SB_HEADS_PER_STEP = 4
SB_DEAD_LOG = -105.0


def _sb_chunk(s, strict, carry, tri):
    r = s.shape[0]
    bw = tri.shape[0]
    nb = s.shape[1] // bw
    sbs = [s[:, b * bw:(b + 1) * bw] for b in range(nb)]
    parts = []
    for b in range(nb):
        lf = jnp.minimum(-sbs[b], 0.0) - jnp.log(1.0 + jnp.exp(-jnp.abs(sbs[b])))
        if strict is not None and b == nb - 1:
            lf = jnp.where(strict, lf, 0.0)
        parts.append(lf.astype(BF16))
    c_all = jnp.dot(jnp.concatenate(parts, axis=0), tri, preferred_element_type=F32)
    ps = [None] * nb
    for b in reversed(range(nb)):
        c = c_all[b * r:(b + 1) * r]
        p = jnp.exp(sbs[b] + (c + carry))
        if strict is not None and b == nb - 1:
            p = jnp.where(strict, p, 0.0)
        ps[b] = p.astype(BF16)
        carry = carry + c[:, 0:1]
    return ps, carry


def _qk(q, k):
    return lax.dot_general(q, k, (((1,), (1,)), ((), ())), preferred_element_type=F32)


def _pv(ps, v):
    bw = ps[0].shape[1]
    acc = None
    for b, p in enumerate(ps):
        d = jnp.dot(p, v[b * bw:(b + 1) * bw], preferred_element_type=F32)
        acc = d if acc is None else acc + d
    return acc


def _head_cols(h):
    return slice(h * SB_HEAD_DIM, (h + 1) * SB_HEAD_DIM)


def _sb_prompt_kernel(q_ref, k_ref, v_ref, sz_ref, tri_ref, o_ref):
    tq = SB_TQ
    heads = range(SB_HEADS_PER_STEP)
    kb_ref, vb_ref = k_ref.at[0], v_ref.at[0]
    kmax = []
    for h in heads:
        kf = kb_ref[:, _head_cols(h)].astype(F32)
        kmax.append(jnp.sqrt(jnp.max(jnp.sum(kf * kf, axis=-1, keepdims=True), axis=0, keepdims=True)))

    tri = tri_ref[...]
    row = lax.broadcasted_iota(jnp.int32, (tq, SB_BLOCK), 0)
    col = lax.broadcasted_iota(jnp.int32, (tq, SB_BLOCK), 1)
    strict = col < row
    zero_carry = jnp.zeros((tq, 1), F32)

    def chunk(h, q, k0, width, mask, carry):
        s = _qk(q, kb_ref[pl.ds(k0, width), _head_cols(h)])
        ps, carry = _sb_chunk(s, mask, carry, tri)
        return _pv(ps, vb_ref[pl.ds(k0, width), _head_cols(h)]), carry

    def store(h, rows, acc):
        o_ref[0, rows, _head_cols(h)] = (acc * sz_ref[0, rows, _head_cols(h)].astype(F32)).astype(o_ref.dtype)

    for h in heads:
        acc, _ = chunk(h, q_ref[0, 0:tq, _head_cols(h)], 0, tq, strict, zero_carry)
        store(h, slice(0, tq), acc)

    def q_block(qi, _):
        rows = pl.ds(pl.multiple_of(qi * tq, tq), tq)
        qs = [q_ref[0, rows, _head_cols(h)] for h in heads]
        bounds = []
        for h in heads:
            qf = qs[h].astype(F32)
            bounds.append(jnp.sqrt(jnp.sum(qf * qf, axis=-1, keepdims=True)) * kmax[h] * 1.001)

        def alive(carries):
            worst = carries[0] + bounds[0]
            for h in heads[1:]:
                worst = jnp.maximum(worst, carries[h] + bounds[h])
            return (jnp.max(worst) > SB_DEAD_LOG).astype(jnp.int32)

        k0 = pl.multiple_of((qi - 1) * SB_BLOCK, SB_BLOCK)
        first = [chunk(h, qs[h], k0, SB_BLOCK + tq, strict, zero_carry) for h in heads]
        accs = tuple(f[0] for f in first)
        carries = tuple(f[1] for f in first)

        def cond(st):
            return jnp.logical_and(st[0] >= 0, st[1] > 0)

        def body(st):
            kb_idx, _, carries, accs = st
            k0 = pl.multiple_of(kb_idx * SB_BLOCK, SB_BLOCK)
            nxt = [chunk(h, qs[h], k0, SB_BLOCK, None, carries[h]) for h in heads]
            carries = tuple(n[1] for n in nxt)
            accs = tuple(a + n[0] for a, n in zip(accs, nxt))
            return kb_idx - 1, alive(carries), carries, accs

        st = lax.while_loop(cond, body, (qi - 2, alive(carries), carries, accs))
        for h in heads:
            store(h, rows, st[3][h])
        return 0

    lax.fori_loop(1, q_ref.shape[1] // tq, q_block, 0)


def _sb_prompt(q, k, v, sz, tri):
    b, l, _ = q.shape
    assert SB_TQ == SB_BLOCK and l % SB_TQ == 0
    gw = SB_HEADS_PER_STEP * SB_HEAD_DIM
    blk = pl.BlockSpec((1, l, gw), lambda bi, g: (bi, 0, g))
    return pl.pallas_call(
        _sb_prompt_kernel,
        grid=(b, SB_HEADS // SB_HEADS_PER_STEP),
        in_specs=[blk, blk, blk, blk, pl.BlockSpec((SB_BLOCK, SB_BLOCK), lambda bi, g: (0, 0))],
        out_specs=blk,
        out_shape=jax.ShapeDtypeStruct(q.shape, BF16),
        compiler_params=_params(2),
        name="sb_prompt",
    )(q, k, v, sz, tri)


def _sb_decode_kernel(q_ref, kn_ref, vn_ref, sz_ref, tri_ref, kc_ref, vc_ref, o_ref):
    nh = SB_HEADS
    past = kc_ref.shape[0] // nh
    tq = q_ref.shape[1]
    tri = tri_ref[...]
    row = lax.broadcasted_iota(jnp.int32, (tq, LANES), 0)
    col = lax.broadcasted_iota(jnp.int32, (tq, LANES), 1)
    pad = jnp.zeros((LANES - tq, SB_HEAD_DIM), BF16)
    for h in range(nh):
        cols = _head_cols(h)
        q = q_ref[0, :, cols]
        kn = jnp.concatenate([kn_ref[0, :, cols], pad], axis=0)
        vn = jnp.concatenate([vn_ref[0, :, cols], pad], axis=0)
        ps, carry = _sb_chunk(_qk(q, kn), col < row, jnp.zeros((tq, 1), F32), tri[:LANES, :LANES])
        acc = _pv(ps, vn)
        head_rows = pl.ds(h, past, stride=nh)
        ps, _ = _sb_chunk(_qk(q, kc_ref[head_rows, :].astype(BF16)), None, carry, tri)
        acc = acc + _pv(ps, vc_ref[head_rows, :].astype(BF16))
        o_ref[0, :, cols] = (acc * sz_ref[0, :, cols].astype(F32)).astype(o_ref.dtype)


def _sb_decode(q, k_new, v_new, k_cache, v_cache, sz, tri):
    b, t, w = q.shape
    _, p, nh, dh = k_cache.shape
    assert nh == SB_HEADS and dh == SB_HEAD_DIM
    blk = pl.BlockSpec((1, t, w), lambda bi: (bi, 0, 0))
    blk_cache = pl.BlockSpec((None, p * nh, dh), lambda bi: (bi, 0, 0))
    return pl.pallas_call(
        _sb_decode_kernel,
        grid=(b,),
        in_specs=[blk, blk, blk, blk, pl.BlockSpec((SB_BLOCK, SB_BLOCK), lambda bi: (0, 0)),
                  blk_cache, blk_cache],
        out_specs=blk,
        out_shape=jax.ShapeDtypeStruct(q.shape, BF16),
        compiler_params=_params(1),
        name="sb_decode",
    )(q, k_new, v_new, sz, tri, k_cache.reshape(b, p * nh, dh), v_cache.reshape(b, p * nh, dh))


def _chunk_causal(n):
    pos = jnp.arange(n)
    return (pos[None, :] // CHUNK) <= (pos[:, None] // CHUNK)


def _mm_tn(k, w_itemsize=4, out_bytes=8):
    for tn in (1024, 512, 256):
        if 2 * (MM_TM * k * 2 + k * tn * w_itemsize + MM_TM * tn * out_bytes) <= MM_VMEM_BUDGET:
            return tn
    raise ValueError("matmul blocks do not fit VMEM")


def _gm_mix_params(seq_len, w_s, b_s):
    blk = min(seq_len, GM_BLOCK)
    w = w_s[:, :blk, :blk] * _chunk_causal(blk).astype(w_s.dtype)
    reps = GM_BLOCK // blk
    wm = jnp.einsum("ab,gts->gatbs", jnp.eye(reps, dtype=w.dtype), w).reshape(
        GM_GROUPS, GM_BLOCK, GM_BLOCK).astype(BF16)
    return wm, jnp.tile(b_s[:, :blk].T, (reps, 1))


def _gm_layer(xs, seq_lens, norm_g, w_in, ln_g, ln_b, w_s, b_s, w_out):
    d = xs[0].shape[1]
    hs = [_rmsnorm(x, norm_g, BF16, min(x.shape[0], 1024)) for x in xs]
    tn = _mm_tn(d, out_bytes=2)
    ugs = _matmul(hs, w_in, BF16, tn=tn, n=2 * GM_WIDTH, act="gelu", name="gm_in_uv")
    szs = _matmul(hs, w_in, BF16, tn=tn, col0=2 * GM_WIDTH, n=GM_WIDTH, act="silu", name="gm_in_z")
    ys, v_rows = [], None
    for g, (ug, sz, seq_len) in enumerate(zip(ugs, szs, seq_lens)):
        emit_vn = g == len(xs) - 1
        outs = _gm_mix(ug, sz, *_gm_mix_params(seq_len, w_s, b_s), ln_g, ln_b,
                       tm=min(ug.shape[0], 256), emit_vn=emit_vn)
        ys.append(outs[0])
        v_rows = outs[1] if emit_vn else v_rows
    x_new = _matmul(ys, w_out, F32, tn=_mm_tn(GM_WIDTH, w_out.dtype.itemsize), res=xs, name="gm_out")
    return x_new, v_rows


def _sb_in(xs, norm_g, w_in):
    d = xs[0].shape[1]
    hs = [_rmsnorm(x, norm_g, BF16, min(x.shape[0], 1024)) for x in xs]
    tn = _mm_tn(d, out_bytes=2)
    w = SB_WIDTH
    q = _matmul(hs, w_in, BF16, tn=tn, n=w, scale=SB_HEAD_DIM ** -0.5, name="sb_in_q")
    k, kb = _matmul_heads(hs, w_in, col0=w, name="sb_in_k")
    v, vb = _matmul_heads(hs, w_in, col0=2 * w, name="sb_in_v")
    sz = _matmul(hs, w_in, BF16, tn=tn, col0=3 * w, n=w, act="silu", name="sb_in_z")
    return q, k, v, kb, vb, sz


def kernel(x_prompt, x_sample, cache_sb_k, cache_sb_v, norm_g, final_norm_g, gm_w_in, gm_ln_g, gm_ln_b,
           gm_w_s, gm_b_s, gm_w_out, sb_w_in, sb_w_out):
    bsz, seq, d = x_prompt.shape
    dbsz, dseq, _ = x_sample.shape
    xs = [x_prompt.reshape(bsz * seq, d), x_sample.reshape(dbsz * dseq, d)]

    tt = (jnp.arange(SB_BLOCK)[:, None] >= jnp.arange(SB_BLOCK)[None, :]).astype(BF16)

    gm_v_rows, kp_rows, vp_rows, ks_rows, vs_rows = [], [], [], [], []
    for i in range(DEPTH):
        j = i // N_MIXERS
        if i % N_MIXERS == 0:
            xs, v_new = _gm_layer(xs, (seq, dseq), norm_g[i], gm_w_in[j], gm_ln_g[j], gm_ln_b[j], gm_w_s[j],
                                  gm_b_s[j], gm_w_out[j].astype(BF16))
            gm_v_rows.append(v_new.reshape(dbsz, dseq, GM_WIDTH))
        else:
            q, k, v, kb, vb, sz = _sb_in(xs, norm_g[i], sb_w_in[j])
            shp_p, shp_s = (bsz, seq, SB_WIDTH), (dbsz, dseq, SB_WIDTH)
            yp = _sb_prompt(q[0].reshape(shp_p), kb[0].reshape(shp_p), vb[0].reshape(shp_p),
                            sz[0].reshape(shp_p), tt)
            ys = _sb_decode(q[1].reshape(shp_s), kb[1].reshape(shp_s), vb[1].reshape(shp_s), cache_sb_k[j],
                            cache_sb_v[j], sz[1].reshape(shp_s), tt)
            y = [yp.reshape(bsz * seq, SB_WIDTH), ys.reshape(dbsz * dseq, SB_WIDTH)]
            if i == DEPTH - 1:
                xs = _matmul_res_norm(y, sb_w_out[j].astype(BF16), xs, final_norm_g, tm=512, name="sb_out_norm")
            else:
                xs = _matmul(y, sb_w_out[j], F32, tn=_mm_tn(SB_WIDTH), res=xs, name="sb_out")
            kp_rows.append(k[0].reshape(bsz, seq, SB_HEADS, SB_HEAD_DIM))
            vp_rows.append(v[0].reshape(bsz, seq, SB_HEADS, SB_HEAD_DIM))
            ks_rows.append(k[1].reshape(dbsz, dseq, SB_HEADS, SB_HEAD_DIM))
            vs_rows.append(v[1].reshape(dbsz, dseq, SB_HEADS, SB_HEAD_DIM))

    if DEPTH % N_MIXERS != 0:
        xs = [_rmsnorm(x, final_norm_g, F32, min(x.shape[0], 512)) for x in xs]
    y_prompt, y_sample = xs[0].reshape(bsz, seq, d), xs[1].reshape(dbsz, dseq, d)
    return (y_prompt, y_sample, jnp.stack(kp_rows), jnp.stack(vp_rows), jnp.stack(ks_rows),
            jnp.stack(vs_rows), jnp.stack(gm_v_rows))
```

```python
import functools

import jax
import jax.numpy as jnp
from jax import lax
from jax.experimental import pallas as pl
from jax.experimental.pallas import tpu as pltpu

D_MODEL = 2048
DEPTH = 2
CHUNK = 64
N_MIXERS = 2
GM_WIDTH = 2 * D_MODEL
GM_BLOCK = 128
GM_GROUPS = 16
GM_GROUP_DIM = GM_WIDTH // GM_GROUPS
SB_HEADS = 16
SB_HEAD_DIM = D_MODEL // SB_HEADS
SB_WIDTH = SB_HEADS * SB_HEAD_DIM
NORM_EPS = 1e-6
LN_EPS = 1e-5

LANES = 128
VMEM_LIMIT_BYTES = 56 * 1024 * 1024
MM_VMEM_BUDGET = 40 * 1024 * 1024
F32 = jnp.float32
BF16 = jnp.bfloat16


def _params(n_grid_dims):
    return pltpu.CompilerParams(
        dimension_semantics=("arbitrary",) * n_grid_dims,
        vmem_limit_bytes=VMEM_LIMIT_BYTES,
    )


def _gelu(x):
    c = 0.7978845608028654
    h = 0.5 * x
    return h + h * jnp.tanh(x * (c + (c * 0.044715) * (x * x)))


def _silu(x):
    h = 0.5 * x
    return h + h * jnp.tanh(h)


def _rmsnorm_kernel(x_ref, g_ref, o_ref):
    x = x_ref[...]
    ms = jnp.mean(x * x, axis=-1, keepdims=True)
    o_ref[...] = (x * lax.rsqrt(ms + NORM_EPS) * g_ref[...]).astype(o_ref.dtype)


def _rmsnorm(x, g, out_dtype, tm):
    m, d = x.shape
    return pl.pallas_call(
        _rmsnorm_kernel,
        grid=(m // tm,),
        in_specs=[pl.BlockSpec((tm, d), lambda i: (i, 0)),
                  pl.BlockSpec((1, d), lambda i: (0, 0))],
        out_specs=pl.BlockSpec((tm, d), lambda i: (i, 0)),
        out_shape=jax.ShapeDtypeStruct((m, d), out_dtype),
        compiler_params=_params(1),
        name="rmsnorm",
    )(x, g.reshape(1, d))


MM_TM = 1024
HEAD_TILE = 8


def _row_groups(ms, tm_max):
    tms = [min(m, tm_max) for m in ms]
    counts = [m // tm for m, tm in zip(ms, tms)]
    assert all(m == tm * c for m, tm, c in zip(ms, tms, counts))
    starts = [sum(counts[a + 1:]) for a in range(len(ms))]
    return tms, counts, starts


def _row_index(i, start, count):
    return jnp.clip(i - start, 0, count - 1)


def _on_group(i, a, counts, starts, fn):
    if len(counts) == 1:
        fn()
    else:
        pl.when(jnp.logical_and(i >= starts[a], i < starts[a] + counts[a]))(fn)


def _mm_kernel(*refs, groups, act, scale, has_res):
    n_arr = len(groups[0])
    x_refs, w_ref = refs[:n_arr], refs[n_arr]
    r_refs = refs[n_arr + 1:2 * n_arr + 1] if has_res else None
    o_refs = refs[-n_arr:]
    i = pl.program_id(1)

    def tile(a):
        acc = jnp.dot(x_refs[a][...], w_ref[...].astype(BF16), preferred_element_type=F32)
        if act == "gelu":
            acc = _gelu(acc)
        elif act == "silu":
            acc = _silu(acc)
        if scale is not None:
            acc = acc * scale
        if has_res:
            acc = acc + r_refs[a][...]
        o_refs[a][...] = acc.astype(o_refs[a].dtype)

    for a in range(n_arr):
        _on_group(i, a, *groups, functools.partial(tile, a))


def _matmul(xs, w, out_dtype, *, tn, col0=0, n=None, act=None, scale=None, res=None, name="matmul"):
    k = xs[0].shape[1]
    n = w.shape[1] if n is None else n
    assert col0 % tn == 0 and n % tn == 0
    jb = col0 // tn
    tms, counts, starts = _row_groups([x.shape[0] for x in xs], MM_TM)
    in_specs = [pl.BlockSpec((tm, k), lambda j, i, s=s, c=c: (_row_index(i, s, c), 0))
                for tm, c, s in zip(tms, counts, starts)]
    in_specs.append(pl.BlockSpec((k, tn), lambda j, i: (0, j + jb)))
    row_col = [pl.BlockSpec((tm, tn), lambda j, i, s=s, c=c: (_row_index(i, s, c), j))
               for tm, c, s in zip(tms, counts, starts)]
    args = [*xs, w]
    if res is not None:
        in_specs += row_col
        args += list(res)
    return pl.pallas_call(
        functools.partial(_mm_kernel, groups=(counts, starts), act=act, scale=scale, has_res=res is not None),
        grid=(n // tn, sum(counts)),
        in_specs=in_specs,
        out_specs=row_col,
        out_shape=[jax.ShapeDtypeStruct((x.shape[0], n), out_dtype) for x in xs],
        compiler_params=_params(2),
        name=name,
    )(*args)


def _mm_heads_kernel(*refs, groups):
    n_arr = len(groups[0])
    x_refs, w_ref = refs[:n_arr], refs[n_arr]
    of_refs, ob_refs = refs[n_arr + 1:2 * n_arr + 1], refs[2 * n_arr + 1:]
    i = pl.program_id(1)

    def tile(a):
        tm = x_refs[a].shape[0]
        acc = jnp.dot(x_refs[a][...], w_ref[...].astype(BF16), preferred_element_type=F32)
        ob_refs[a][...] = acc.astype(ob_refs[a].dtype)
        rows = of_refs[a].reshape(tm * HEAD_TILE, SB_HEAD_DIM)
        for h in range(HEAD_TILE):
            rows[pl.ds(h, tm, stride=HEAD_TILE), :] = acc[:, _head_cols(h)]

    for a in range(n_arr):
        _on_group(i, a, *groups, functools.partial(tile, a))


def _matmul_heads(xs, w, *, col0, name):
    k = xs[0].shape[1]
    tn = HEAD_TILE * SB_HEAD_DIM
    assert col0 % tn == 0
    jb = col0 // tn
    tms, counts, starts = _row_groups([x.shape[0] for x in xs], MM_TM)
    groups = list(zip(tms, counts, starts))
    in_specs = [pl.BlockSpec((tm, k), lambda j, i, s=s, c=c: (_row_index(i, s, c), 0)) for tm, c, s in groups]
    in_specs.append(pl.BlockSpec((k, tn), lambda j, i: (0, j + jb)))
    out_specs = [pl.BlockSpec((tm, HEAD_TILE, SB_HEAD_DIM), lambda j, i, s=s, c=c: (_row_index(i, s, c), j, 0))
                 for tm, c, s in groups]
    out_specs += [pl.BlockSpec((tm, tn), lambda j, i, s=s, c=c: (_row_index(i, s, c), j)) for tm, c, s in groups]
    outs = pl.pallas_call(
        functools.partial(_mm_heads_kernel, groups=(counts, starts)),
        grid=(SB_WIDTH // tn, sum(counts)),
        in_specs=in_specs,
        out_specs=out_specs,
        out_shape=[jax.ShapeDtypeStruct((x.shape[0], SB_HEADS, SB_HEAD_DIM), F32) for x in xs]
        + [jax.ShapeDtypeStruct((x.shape[0], SB_WIDTH), BF16) for x in xs],
        compiler_params=_params(2),
        name=name,
    )(*xs, w)
    return outs[:len(xs)], outs[len(xs):]


def _mm_res_norm_kernel(*refs, groups):
    n_arr = len(groups[0])
    x_refs, w_ref = refs[:n_arr], refs[n_arr]
    r_refs, g_ref, o_refs = refs[n_arr + 1:2 * n_arr + 1], refs[2 * n_arr + 1], refs[2 * n_arr + 2:]
    i = pl.program_id(0)

    def tile(a):
        acc = jnp.dot(x_refs[a][...], w_ref[...], preferred_element_type=F32) + r_refs[a][...]
        ms = jnp.mean(acc * acc, axis=-1, keepdims=True)
        o_refs[a][...] = acc * lax.rsqrt(ms + NORM_EPS) * g_ref[...]

    for a in range(n_arr):
        _on_group(i, a, *groups, functools.partial(tile, a))


def _matmul_res_norm(xs, w, res, g, *, tm, name):
    k, n = w.shape
    tms, counts, starts = _row_groups([x.shape[0] for x in xs], tm)
    row = lambda cols: [pl.BlockSpec((t, cols), lambda i, s=s, c=c: (_row_index(i, s, c), 0))
                        for t, c, s in zip(tms, counts, starts)]
    return pl.pallas_call(
        functools.partial(_mm_res_norm_kernel, groups=(counts, starts)),
        grid=(sum(counts),),
        in_specs=[*row(k), pl.BlockSpec((k, n), lambda i: (0, 0)), *row(n),
                  pl.BlockSpec((1, n), lambda i: (0, 0))],
        out_specs=row(n),
        out_shape=[jax.ShapeDtypeStruct((x.shape[0], n), F32) for x in xs],
        compiler_params=_params(1),
        name=name,
    )(*xs, w, *res, g.reshape(1, n))


def _gm_mix_kernel(u_ref, gv_ref, sz_ref, wm_ref, bt_ref, lg_ref, lb_ref, *out_refs, emit_vn):
    if emit_vn:
        y_ref, vn_ref = out_refs
    else:
        (y_ref,) = out_refs
    tm = u_ref.shape[0]
    inv_w = 1.0 / GM_WIDTH
    s1 = jnp.zeros((tm, 1), F32)
    for g in range(GM_GROUPS):
        cols = slice(g * GM_GROUP_DIM, (g + 1) * GM_GROUP_DIM)
        s1 = s1 + jnp.sum(gv_ref[:, cols].astype(F32), axis=-1, keepdims=True)
    mu = s1 * inv_w
    s2 = jnp.zeros((tm, 1), F32)
    for g in range(GM_GROUPS):
        cols = slice(g * GM_GROUP_DIM, (g + 1) * GM_GROUP_DIM)
        xc = gv_ref[:, cols].astype(F32) - mu
        s2 = s2 + jnp.sum(xc * xc, axis=-1, keepdims=True)
    rstd = lax.rsqrt(s2 * inv_w + LN_EPS)
    for g in range(GM_GROUPS):
        cols = slice(g * GM_GROUP_DIM, (g + 1) * GM_GROUP_DIM)
        vn = (gv_ref[:, cols].astype(F32) - mu) * rstd * lg_ref[:, cols] + lb_ref[:, cols]
        if emit_vn:
            vn_ref[:, cols] = vn
        vnb = vn.astype(BF16)
        wg = wm_ref[g]
        bias = bt_ref[:, g:g + 1]
        for r in range(tm // GM_BLOCK):
            rows = slice(r * GM_BLOCK, (r + 1) * GM_BLOCK)
            mixed = jnp.dot(wg, vnb[rows], preferred_element_type=F32) + bias
            y_ref[rows, cols] = (u_ref[rows, cols] * mixed.astype(BF16)) * sz_ref[rows, cols]


def _gm_mix(ug, sz, wm, bt, ln_g, ln_b, *, tm, emit_vn):
    m = sz.shape[0]
    row_blk = lambda c: pl.BlockSpec((tm, GM_WIDTH), lambda i, c=c: (i, c))
    full = lambda shape: pl.BlockSpec(shape, lambda i: (0,) * len(shape))
    out_shape = [jax.ShapeDtypeStruct((m, GM_WIDTH), BF16)]
    out_specs = [row_blk(0)]
    if emit_vn:
        out_shape.append(jax.ShapeDtypeStruct((m, GM_WIDTH), F32))
        out_specs.append(row_blk(0))
    return pl.pallas_call(
        functools.partial(_gm_mix_kernel, emit_vn=emit_vn),
        grid=(m // tm,),
        in_specs=[row_blk(0), row_blk(1), row_blk(0),
                  full((GM_GROUPS, GM_BLOCK, GM_BLOCK)), full((GM_BLOCK, GM_GROUPS)),
                  full((1, GM_WIDTH)), full((1, GM_WIDTH))],
        out_specs=out_specs,
        out_shape=out_shape,
        compiler_params=_params(1),
        name="gm_mix",
    )(ug, ug, sz, wm, bt, ln_g.reshape(1, GM_WIDTH), ln_b.reshape(1, GM_WIDTH))


SB_BLOCK = 256
SB_TQ = 256
SB_HEADS_PER_STEP = 4
LOG2_E = 1.4426950408889634
SB_DEAD_LOG = -151.5


def _sb_chunk(s, strict, carry, tri):
    r = s.shape[0]
    bw = tri.shape[0]
    nb = s.shape[1] // bw
    sbs = [s[:, b * bw:(b + 1) * bw] for b in range(nb)]
    parts = []
    for b in range(nb):
        neg = -sbs[b]
        lf = jnp.minimum(neg, 0.0) - jnp.log2(1.0 + jnp.exp2(jnp.minimum(sbs[b], neg)))
        if strict is not None and b == nb - 1:
            lf = jnp.where(strict, lf, 0.0)
        parts.append(lf.astype(BF16))
    c_all = jnp.dot(jnp.concatenate(parts, axis=0), tri, preferred_element_type=F32)
    ps = [None] * nb
    for b in reversed(range(nb)):
        c = c_all[b * r:(b + 1) * r]
        p = jnp.exp2(sbs[b] + (c + carry))
        if strict is not None and b == nb - 1:
            p = jnp.where(strict, p, 0.0)
        ps[b] = p.astype(BF16)
        carry = carry + c[:, 0:1]
    return ps, carry


def _qk(q, k):
    return lax.dot_general(q, k, (((1,), (1,)), ((), ())), preferred_element_type=F32)


def _pv(ps, v):
    bw = ps[0].shape[1]
    acc = None
    for b, p in enumerate(ps):
        d = jnp.dot(p, v[b * bw:(b + 1) * bw], preferred_element_type=F32)
        acc = d if acc is None else acc + d
    return acc


def _head_cols(h):
    return slice(h * SB_HEAD_DIM, (h + 1) * SB_HEAD_DIM)


def _sb_prompt_kernel(q_ref, k_ref, v_ref, sz_ref, tri_ref, o_ref):
    tq = SB_TQ
    heads = range(SB_HEADS_PER_STEP)
    kb_ref, vb_ref = k_ref.at[0], v_ref.at[0]
    kmax = []
    for h in heads:
        kf = kb_ref[:, _head_cols(h)].astype(F32)
        kmax.append(jnp.sqrt(jnp.max(jnp.sum(kf * kf, axis=-1, keepdims=True), axis=0, keepdims=True)))

    tri = tri_ref[...]
    row = lax.broadcasted_iota(jnp.int32, (tq, SB_BLOCK), 0)
    col = lax.broadcasted_iota(jnp.int32, (tq, SB_BLOCK), 1)
    strict = col < row
    zero_carry = jnp.zeros((tq, 1), F32)

    def chunk(h, q, k0, width, mask, carry):
        s = _qk(q, kb_ref[pl.ds(k0, width), _head_cols(h)])
        ps, carry = _sb_chunk(s, mask, carry, tri)
        return _pv(ps, vb_ref[pl.ds(k0, width), _head_cols(h)]), carry

    def store(h, rows, acc):
        o_ref[0, rows, _head_cols(h)] = (acc * sz_ref[0, rows, _head_cols(h)].astype(F32)).astype(o_ref.dtype)

    for h in heads:
        acc, _ = chunk(h, q_ref[0, 0:tq, _head_cols(h)], 0, tq, strict, zero_carry)
        store(h, slice(0, tq), acc)

    def q_block(qi, _):
        rows = pl.ds(pl.multiple_of(qi * tq, tq), tq)
        qs = [q_ref[0, rows, _head_cols(h)] for h in heads]
        bounds = []
        for h in heads:
            qf = qs[h].astype(F32)
            bounds.append(jnp.sqrt(jnp.sum(qf * qf, axis=-1, keepdims=True)) * kmax[h] * 1.001)

        def alive(carries):
            worst = carries[0] + bounds[0]
            for h in heads[1:]:
                worst = jnp.maximum(worst, carries[h] + bounds[h])
            return (jnp.max(worst) > SB_DEAD_LOG).astype(jnp.int32)

        k0 = pl.multiple_of((qi - 1) * SB_BLOCK, SB_BLOCK)
        first = [chunk(h, qs[h], k0, SB_BLOCK + tq, strict, zero_carry) for h in heads]
        accs = tuple(f[0] for f in first)
        carries = tuple(f[1] for f in first)

        def cond(st):
            return jnp.logical_and(st[0] >= 0, st[1] > 0)

        def body(st):
            kb_idx, _, carries, accs = st
            k0 = pl.multiple_of(kb_idx * SB_BLOCK, SB_BLOCK)
            nxt = [chunk(h, qs[h], k0, SB_BLOCK, None, carries[h]) for h in heads]
            carries = tuple(n[1] for n in nxt)
            accs = tuple(a + n[0] for a, n in zip(accs, nxt))
            return kb_idx - 1, alive(carries), carries, accs

        st = lax.while_loop(cond, body, (qi - 2, alive(carries), carries, accs))
        for h in heads:
            store(h, rows, st[3][h])
        return 0

    lax.fori_loop(1, q_ref.shape[1] // tq, q_block, 0)


def _sb_prompt(q, k, v, sz, tri):
    b, l, _ = q.shape
    assert SB_TQ == SB_BLOCK and l % SB_TQ == 0
    gw = SB_HEADS_PER_STEP * SB_HEAD_DIM
    blk = pl.BlockSpec((1, l, gw), lambda bi, g: (bi, 0, g))
    return pl.pallas_call(
        _sb_prompt_kernel,
        grid=(b, SB_HEADS // SB_HEADS_PER_STEP),
        in_specs=[blk, blk, blk, blk, pl.BlockSpec((SB_BLOCK, SB_BLOCK), lambda bi, g: (0, 0))],
        out_specs=blk,
        out_shape=jax.ShapeDtypeStruct(q.shape, BF16),
        compiler_params=_params(2),
        name="sb_prompt",
    )(q, k, v, sz, tri)


def _sb_decode_kernel(q_ref, kn_ref, vn_ref, sz_ref, tri_ref, kc_ref, vc_ref, o_ref):
    nh = SB_HEADS
    past = kc_ref.shape[0] // nh
    tq = q_ref.shape[1]
    tri = tri_ref[...]
    row = lax.broadcasted_iota(jnp.int32, (tq, LANES), 0)
    col = lax.broadcasted_iota(jnp.int32, (tq, LANES), 1)
    pad = jnp.zeros((LANES - tq, SB_HEAD_DIM), BF16)
    for h in range(nh):
        cols = _head_cols(h)
        q = q_ref[0, :, cols]
        kn = jnp.concatenate([kn_ref[0, :, cols], pad], axis=0)
        vn = jnp.concatenate([vn_ref[0, :, cols], pad], axis=0)
        ps, carry = _sb_chunk(_qk(q, kn), col < row, jnp.zeros((tq, 1), F32), tri[:LANES, :LANES])
        acc = _pv(ps, vn)
        head_rows = pl.ds(h, past, stride=nh)
        ps, _ = _sb_chunk(_qk(q, kc_ref[head_rows, :].astype(BF16)), None, carry, tri)
        acc = acc + _pv(ps, vc_ref[head_rows, :].astype(BF16))
        o_ref[0, :, cols] = (acc * sz_ref[0, :, cols].astype(F32)).astype(o_ref.dtype)


def _sb_decode(q, k_new, v_new, k_cache, v_cache, sz, tri):
    b, t, w = q.shape
    _, p, nh, dh = k_cache.shape
    assert nh == SB_HEADS and dh == SB_HEAD_DIM
    blk = pl.BlockSpec((1, t, w), lambda bi: (bi, 0, 0))
    blk_cache = pl.BlockSpec((None, p * nh, dh), lambda bi: (bi, 0, 0))
    return pl.pallas_call(
        _sb_decode_kernel,
        grid=(b,),
        in_specs=[blk, blk, blk, blk, pl.BlockSpec((SB_BLOCK, SB_BLOCK), lambda bi: (0, 0)),
                  blk_cache, blk_cache],
        out_specs=blk,
        out_shape=jax.ShapeDtypeStruct(q.shape, BF16),
        compiler_params=_params(1),
        name="sb_decode",
    )(q, k_new, v_new, sz, tri, k_cache.reshape(b, p * nh, dh), v_cache.reshape(b, p * nh, dh))


def _chunk_causal(n):
    pos = jnp.arange(n)
    return (pos[None, :] // CHUNK) <= (pos[:, None] // CHUNK)


def _mm_tn(k, w_itemsize=4, out_bytes=8):
    for tn in (1024, 512, 256):
        if 2 * (MM_TM * k * 2 + k * tn * w_itemsize + MM_TM * tn * out_bytes) <= MM_VMEM_BUDGET:
            return tn
    raise ValueError("matmul blocks do not fit VMEM")


def _gm_mix_params(seq_len, w_s, b_s):
    blk = min(seq_len, GM_BLOCK)
    w = w_s[:, :blk, :blk] * _chunk_causal(blk).astype(w_s.dtype)
    reps = GM_BLOCK // blk
    wm = jnp.einsum("ab,gts->gatbs", jnp.eye(reps, dtype=w.dtype), w).reshape(
        GM_GROUPS, GM_BLOCK, GM_BLOCK).astype(BF16)
    return wm, jnp.tile(b_s[:, :blk].T, (reps, 1))


def _gm_layer(xs, seq_lens, norm_g, w_in, ln_g, ln_b, w_s, b_s, w_out):
    d = xs[0].shape[1]
    hs = [_rmsnorm(x, norm_g, BF16, min(x.shape[0], 1024)) for x in xs]
    tn = _mm_tn(d, out_bytes=2)
    ugs = _matmul(hs, w_in, BF16, tn=tn, n=2 * GM_WIDTH, act="gelu", name="gm_in_uv")
    szs = _matmul(hs, w_in, BF16, tn=tn, col0=2 * GM_WIDTH, n=GM_WIDTH, act="silu", name="gm_in_z")
    ys, v_rows = [], None
    for g, (ug, sz, seq_len) in enumerate(zip(ugs, szs, seq_lens)):
        emit_vn = g == len(xs) - 1
        outs = _gm_mix(ug, sz, *_gm_mix_params(seq_len, w_s, b_s), ln_g, ln_b,
                       tm=min(ug.shape[0], 256), emit_vn=emit_vn)
        ys.append(outs[0])
        v_rows = outs[1] if emit_vn else v_rows
    x_new = _matmul(ys, w_out, F32, tn=_mm_tn(GM_WIDTH, w_out.dtype.itemsize), res=xs, name="gm_out")
    return x_new, v_rows


def _sb_in(xs, norm_g, w_in):
    d = xs[0].shape[1]
    hs = [_rmsnorm(x, norm_g, BF16, min(x.shape[0], 1024)) for x in xs]
    tn = _mm_tn(d, out_bytes=2)
    w = SB_WIDTH
    q = _matmul(hs, w_in, BF16, tn=tn, n=w, scale=SB_HEAD_DIM ** -0.5 * LOG2_E, name="sb_in_q")
    k, kb = _matmul_heads(hs, w_in, col0=w, name="sb_in_k")
    v, vb = _matmul_heads(hs, w_in, col0=2 * w, name="sb_in_v")
    sz = _matmul(hs, w_in, BF16, tn=tn, col0=3 * w, n=w, act="silu", name="sb_in_z")
    return q, k, v, kb, vb, sz


def kernel(x_prompt, x_sample, cache_sb_k, cache_sb_v, norm_g, final_norm_g, gm_w_in, gm_ln_g, gm_ln_b,
           gm_w_s, gm_b_s, gm_w_out, sb_w_in, sb_w_out):
    bsz, seq, d = x_prompt.shape
    dbsz, dseq, _ = x_sample.shape
    xs = [x_prompt.reshape(bsz * seq, d), x_sample.reshape(dbsz * dseq, d)]

    tt = (jnp.arange(SB_BLOCK)[:, None] >= jnp.arange(SB_BLOCK)[None, :]).astype(BF16)

    gm_v_rows, kp_rows, vp_rows, ks_rows, vs_rows = [], [], [], [], []
    for i in range(DEPTH):
        j = i // N_MIXERS
        if i % N_MIXERS == 0:
            xs, v_new = _gm_layer(xs, (seq, dseq), norm_g[i], gm_w_in[j], gm_ln_g[j], gm_ln_b[j], gm_w_s[j],
                                  gm_b_s[j], gm_w_out[j].astype(BF16))
            gm_v_rows.append(v_new.reshape(dbsz, dseq, GM_WIDTH))
        else:
            q, k, v, kb, vb, sz = _sb_in(xs, norm_g[i], sb_w_in[j])
            shp_p, shp_s = (bsz, seq, SB_WIDTH), (dbsz, dseq, SB_WIDTH)
            yp = _sb_prompt(q[0].reshape(shp_p), kb[0].reshape(shp_p), vb[0].reshape(shp_p),
                            sz[0].reshape(shp_p), tt)
            ys = _sb_decode(q[1].reshape(shp_s), kb[1].reshape(shp_s), vb[1].reshape(shp_s), cache_sb_k[j],
                            cache_sb_v[j], sz[1].reshape(shp_s), tt)
            y = [yp.reshape(bsz * seq, SB_WIDTH), ys.reshape(dbsz * dseq, SB_WIDTH)]
            if i == DEPTH - 1:
                xs = _matmul_res_norm(y, sb_w_out[j].astype(BF16), xs, final_norm_g, tm=512, name="sb_out_norm")
            else:
                xs = _matmul(y, sb_w_out[j], F32, tn=_mm_tn(SB_WIDTH), res=xs, name="sb_out")
            kp_rows.append(k[0].reshape(bsz, seq, SB_HEADS, SB_HEAD_DIM))
            vp_rows.append(v[0].reshape(bsz, seq, SB_HEADS, SB_HEAD_DIM))
            ks_rows.append(k[1].reshape(dbsz, dseq, SB_HEADS, SB_HEAD_DIM))
            vs_rows.append(v[1].reshape(dbsz, dseq, SB_HEADS, SB_HEAD_DIM))

    if DEPTH % N_MIXERS != 0:
        xs = [_rmsnorm(x, final_norm_g, F32, min(x.shape[0], 512)) for x in xs]
    y_prompt, y_sample = xs[0].reshape(bsz, seq, d), xs[1].reshape(dbsz, dseq, d)
    return (y_prompt, y_sample, jnp.stack(kp_rows), jnp.stack(vp_rows), jnp.stack(ks_rows),
            jnp.stack(vs_rows), jnp.stack(gm_v_rows))
```

```python
import functools

import jax
import jax.numpy as jnp
from jax import lax
from jax.experimental import pallas as pl
from jax.experimental.pallas import tpu as pltpu

D_MODEL = 2048
DEPTH = 2
CHUNK = 64
N_MIXERS = 2
GM_WIDTH = 2 * D_MODEL
GM_BLOCK = 128
GM_GROUPS = 16
GM_GROUP_DIM = GM_WIDTH // GM_GROUPS
SB_HEADS = 16
SB_HEAD_DIM = D_MODEL // SB_HEADS
SB_WIDTH = SB_HEADS * SB_HEAD_DIM
NORM_EPS = 1e-6
LN_EPS = 1e-5

LANES = 128
VMEM_LIMIT_BYTES = 56 * 1024 * 1024
MM_VMEM_BUDGET = 40 * 1024 * 1024
F32 = jnp.float32
BF16 = jnp.bfloat16


def _params(n_grid_dims):
    return pltpu.CompilerParams(
        dimension_semantics=("arbitrary",) * n_grid_dims,
        vmem_limit_bytes=VMEM_LIMIT_BYTES,
    )


def _gelu(x):
    c = 0.7978845608028654
    h = 0.5 * x
    return h + h * jnp.tanh(x * (c + (c * 0.044715) * (x * x)))


def _silu(x):
    h = 0.5 * x
    return h + h * jnp.tanh(h)


def _rmsnorm_kernel(x_ref, g_ref, o_ref):
    x = x_ref[...]
    ms = jnp.mean(x * x, axis=-1, keepdims=True)
    o_ref[...] = (x * lax.rsqrt(ms + NORM_EPS) * g_ref[...]).astype(o_ref.dtype)


def _rmsnorm(x, g, out_dtype, tm):
    m, d = x.shape
    return pl.pallas_call(
        _rmsnorm_kernel,
        grid=(m // tm,),
        in_specs=[pl.BlockSpec((tm, d), lambda i: (i, 0)),
                  pl.BlockSpec((1, d), lambda i: (0, 0))],
        out_specs=pl.BlockSpec((tm, d), lambda i: (i, 0)),
        out_shape=jax.ShapeDtypeStruct((m, d), out_dtype),
        compiler_params=_params(1),
        name="rmsnorm",
    )(x, g.reshape(1, d))


MM_TM = 1024
GM_MIX_OUT_TM = 512
HEAD_TILE = 8


def _row_groups(ms, tm_max):
    tms = [min(m, tm_max) for m in ms]
    counts = [m // tm for m, tm in zip(ms, tms)]
    assert all(m == tm * c for m, tm, c in zip(ms, tms, counts))
    starts = [sum(counts[a + 1:]) for a in range(len(ms))]
    return tms, counts, starts


def _row_index(i, start, count):
    return jnp.clip(i - start, 0, count - 1)


def _on_group(i, a, counts, starts, fn):
    if len(counts) == 1:
        fn()
    else:
        pl.when(jnp.logical_and(i >= starts[a], i < starts[a] + counts[a]))(fn)


def _mm_kernel(*refs, groups, act, scale, has_res):
    n_arr = len(groups[0])
    x_refs, w_ref = refs[:n_arr], refs[n_arr]
    r_refs = refs[n_arr + 1:2 * n_arr + 1] if has_res else None
    o_refs = refs[-n_arr:]
    i = pl.program_id(1)

    def tile(a):
        acc = jnp.dot(x_refs[a][...], w_ref[...].astype(BF16), preferred_element_type=F32)
        if act == "gelu":
            acc = _gelu(acc)
        elif act == "silu":
            acc = _silu(acc)
        if scale is not None:
            acc = acc * scale
        if has_res:
            acc = acc + r_refs[a][...]
        o_refs[a][...] = acc.astype(o_refs[a].dtype)

    for a in range(n_arr):
        _on_group(i, a, *groups, functools.partial(tile, a))


def _matmul(xs, w, out_dtype, *, tn, col0=0, n=None, act=None, scale=None, res=None, name="matmul"):
    k = xs[0].shape[1]
    n = w.shape[1] if n is None else n
    assert col0 % tn == 0 and n % tn == 0
    jb = col0 // tn
    tms, counts, starts = _row_groups([x.shape[0] for x in xs], MM_TM)
    in_specs = [pl.BlockSpec((tm, k), lambda j, i, s=s, c=c: (_row_index(i, s, c), 0))
                for tm, c, s in zip(tms, counts, starts)]
    in_specs.append(pl.BlockSpec((k, tn), lambda j, i: (0, j + jb)))
    row_col = [pl.BlockSpec((tm, tn), lambda j, i, s=s, c=c: (_row_index(i, s, c), j))
               for tm, c, s in zip(tms, counts, starts)]
    args = [*xs, w]
    if res is not None:
        in_specs += row_col
        args += list(res)
    return pl.pallas_call(
        functools.partial(_mm_kernel, groups=(counts, starts), act=act, scale=scale, has_res=res is not None),
        grid=(n // tn, sum(counts)),
        in_specs=in_specs,
        out_specs=row_col,
        out_shape=[jax.ShapeDtypeStruct((x.shape[0], n), out_dtype) for x in xs],
        compiler_params=_params(2),
        name=name,
    )(*args)


def _mm_heads_kernel(*refs, groups):
    n_arr = len(groups[0])
    x_refs, w_ref = refs[:n_arr], refs[n_arr]
    of_refs, ob_refs = refs[n_arr + 1:2 * n_arr + 1], refs[2 * n_arr + 1:]
    i = pl.program_id(1)

    def tile(a):
        tm = x_refs[a].shape[0]
        acc = jnp.dot(x_refs[a][...], w_ref[...].astype(BF16), preferred_element_type=F32)
        ob_refs[a][...] = acc.astype(ob_refs[a].dtype)
        rows = of_refs[a].reshape(tm * HEAD_TILE, SB_HEAD_DIM)
        for h in range(HEAD_TILE):
            rows[pl.ds(h, tm, stride=HEAD_TILE), :] = acc[:, _head_cols(h)]

    for a in range(n_arr):
        _on_group(i, a, *groups, functools.partial(tile, a))


def _matmul_heads(xs, w, *, col0, name):
    k = xs[0].shape[1]
    tn = HEAD_TILE * SB_HEAD_DIM
    assert col0 % tn == 0
    jb = col0 // tn
    tms, counts, starts = _row_groups([x.shape[0] for x in xs], MM_TM)
    groups = list(zip(tms, counts, starts))
    in_specs = [pl.BlockSpec((tm, k), lambda j, i, s=s, c=c: (_row_index(i, s, c), 0)) for tm, c, s in groups]
    in_specs.append(pl.BlockSpec((k, tn), lambda j, i: (0, j + jb)))
    out_specs = [pl.BlockSpec((tm, HEAD_TILE, SB_HEAD_DIM), lambda j, i, s=s, c=c: (_row_index(i, s, c), j, 0))
                 for tm, c, s in groups]
    out_specs += [pl.BlockSpec((tm, tn), lambda j, i, s=s, c=c: (_row_index(i, s, c), j)) for tm, c, s in groups]
    outs = pl.pallas_call(
        functools.partial(_mm_heads_kernel, groups=(counts, starts)),
        grid=(SB_WIDTH // tn, sum(counts)),
        in_specs=in_specs,
        out_specs=out_specs,
        out_shape=[jax.ShapeDtypeStruct((x.shape[0], SB_HEADS, SB_HEAD_DIM), F32) for x in xs]
        + [jax.ShapeDtypeStruct((x.shape[0], SB_WIDTH), BF16) for x in xs],
        compiler_params=_params(2),
        name=name,
    )(*xs, w)
    return outs[:len(xs)], outs[len(xs):]


def _mm_res_norm_kernel(*refs, groups):
    n_arr = len(groups[0])
    x_refs, w_ref = refs[:n_arr], refs[n_arr]
    r_refs, g_ref, o_refs = refs[n_arr + 1:2 * n_arr + 1], refs[2 * n_arr + 1], refs[2 * n_arr + 2:]
    i = pl.program_id(0)

    def tile(a):
        acc = jnp.dot(x_refs[a][...], w_ref[...], preferred_element_type=F32) + r_refs[a][...]
        ms = jnp.mean(acc * acc, axis=-1, keepdims=True)
        o_refs[a][...] = acc * lax.rsqrt(ms + NORM_EPS) * g_ref[...]

    for a in range(n_arr):
        _on_group(i, a, *groups, functools.partial(tile, a))


def _matmul_res_norm(xs, w, res, g, *, tm, name):
    k, n = w.shape
    tms, counts, starts = _row_groups([x.shape[0] for x in xs], tm)
    row = lambda cols: [pl.BlockSpec((t, cols), lambda i, s=s, c=c: (_row_index(i, s, c), 0))
                        for t, c, s in zip(tms, counts, starts)]
    return pl.pallas_call(
        functools.partial(_mm_res_norm_kernel, groups=(counts, starts)),
        grid=(sum(counts),),
        in_specs=[*row(k), pl.BlockSpec((k, n), lambda i: (0, 0)), *row(n),
                  pl.BlockSpec((1, n), lambda i: (0, 0))],
        out_specs=row(n),
        out_shape=[jax.ShapeDtypeStruct((x.shape[0], n), F32) for x in xs],
        compiler_params=_params(1),
        name=name,
    )(*xs, w, *res, g.reshape(1, n))


def _gm_mix_rows(u_ref, gv_ref, sz_ref, wm_ref, bt_ref, lg_ref, lb_ref, y_ref, vn_ref):
    emit_vn = vn_ref is not None
    tm = u_ref.shape[0]
    inv_w = 1.0 / GM_WIDTH
    s1 = jnp.zeros((tm, 1), F32)
    for g in range(GM_GROUPS):
        cols = slice(g * GM_GROUP_DIM, (g + 1) * GM_GROUP_DIM)
        s1 = s1 + jnp.sum(gv_ref[:, cols].astype(F32), axis=-1, keepdims=True)
    mu = s1 * inv_w
    s2 = jnp.zeros((tm, 1), F32)
    for g in range(GM_GROUPS):
        cols = slice(g * GM_GROUP_DIM, (g + 1) * GM_GROUP_DIM)
        xc = gv_ref[:, cols].astype(F32) - mu
        s2 = s2 + jnp.sum(xc * xc, axis=-1, keepdims=True)
    rstd = lax.rsqrt(s2 * inv_w + LN_EPS)
    for g in range(GM_GROUPS):
        cols = slice(g * GM_GROUP_DIM, (g + 1) * GM_GROUP_DIM)
        vn = (gv_ref[:, cols].astype(F32) - mu) * rstd * lg_ref[:, cols] + lb_ref[:, cols]
        if emit_vn:
            vn_ref[:, cols] = vn
        vnb = vn.astype(BF16)
        wg = wm_ref[g]
        bias = bt_ref[:, g:g + 1]
        for r in range(tm // GM_BLOCK):
            rows = slice(r * GM_BLOCK, (r + 1) * GM_BLOCK)
            mixed = jnp.dot(wg, vnb[rows], preferred_element_type=F32) + bias
            y_ref[rows, cols] = (u_ref[rows, cols] * mixed.astype(BF16)) * sz_ref[rows, cols]


def _gm_mix_kernel(u_ref, gv_ref, sz_ref, wm_ref, bt_ref, lg_ref, lb_ref, y_ref, vn_ref=None):
    _gm_mix_rows(u_ref, gv_ref, sz_ref, wm_ref, bt_ref, lg_ref, lb_ref, y_ref, vn_ref)


def _gm_mix_out_kernel(u_ref, gv_ref, sz_ref, wm_ref, bt_ref, lg_ref, lb_ref, w_ref, r_ref, o_ref, y_ref):
    @pl.when(pl.program_id(1) == 0)
    def _():
        _gm_mix_rows(u_ref, gv_ref, sz_ref, wm_ref, bt_ref, lg_ref, lb_ref, y_ref, None)

    o_ref[...] = jnp.dot(y_ref[...], w_ref[...], preferred_element_type=F32) + r_ref[...]


def _gm_mix_out(ug, sz, wm, bt, ln_g, ln_b, w_out, res, *, tm, tn):
    m = sz.shape[0]
    n = w_out.shape[1]
    n_rows = m // tm
    ahead = lambda i, j: jnp.minimum(i + jnp.minimum(j, 1), n_rows - 1)
    row_blk = lambda c: pl.BlockSpec((tm, GM_WIDTH), lambda i, j, c=c: (ahead(i, j), c))
    full = lambda shape: pl.BlockSpec(shape, lambda i, j: (0,) * len(shape))
    tile = pl.BlockSpec((tm, tn), lambda i, j: (i, j))
    return pl.pallas_call(
        _gm_mix_out_kernel,
        grid=(n_rows, n // tn),
        in_specs=[row_blk(0), row_blk(1), row_blk(0),
                  full((GM_GROUPS, GM_BLOCK, GM_BLOCK)), full((GM_BLOCK, GM_GROUPS)),
                  full((1, GM_WIDTH)), full((1, GM_WIDTH)),
                  pl.BlockSpec((GM_WIDTH, tn), lambda i, j: (0, j)), tile],
        out_specs=tile,
        out_shape=jax.ShapeDtypeStruct((m, n), F32),
        scratch_shapes=[pltpu.VMEM((tm, GM_WIDTH), BF16)],
        compiler_params=_params(2),
        name="gm_mix_out",
    )(ug, ug, sz, wm, bt, ln_g.reshape(1, GM_WIDTH), ln_b.reshape(1, GM_WIDTH), w_out, res)


def _gm_mix(ug, sz, wm, bt, ln_g, ln_b, *, tm, emit_vn):
    m = sz.shape[0]
    row_blk = lambda c: pl.BlockSpec((tm, GM_WIDTH), lambda i, c=c: (i, c))
    full = lambda shape: pl.BlockSpec(shape, lambda i: (0,) * len(shape))
    out_shape = [jax.ShapeDtypeStruct((m, GM_WIDTH), BF16)]
    out_specs = [row_blk(0)]
    if emit_vn:
        out_shape.append(jax.ShapeDtypeStruct((m, GM_WIDTH), F32))
        out_specs.append(row_blk(0))
    return pl.pallas_call(
        _gm_mix_kernel,
        grid=(m // tm,),
        in_specs=[row_blk(0), row_blk(1), row_blk(0),
                  full((GM_GROUPS, GM_BLOCK, GM_BLOCK)), full((GM_BLOCK, GM_GROUPS)),
                  full((1, GM_WIDTH)), full((1, GM_WIDTH))],
        out_specs=out_specs,
        out_shape=out_shape,
        compiler_params=_params(1),
        name="gm_mix",
    )(ug, ug, sz, wm, bt, ln_g.reshape(1, GM_WIDTH), ln_b.reshape(1, GM_WIDTH))


SB_BLOCK = 256
SB_TQ = 256
SB_HEADS_PER_STEP = 4
LOG2_E = 1.4426950408889634
SB_DEAD_LOG = -151.5


def _sb_chunk(s, strict, carry, tri):
    r = s.shape[0]
    bw = tri.shape[0]
    nb = s.shape[1] // bw
    sbs = [s[:, b * bw:(b + 1) * bw] for b in range(nb)]
    parts = []
    for b in range(nb):
        neg = -sbs[b]
        lf = jnp.minimum(neg, 0.0) - jnp.log2(1.0 + jnp.exp2(jnp.minimum(sbs[b], neg)))
        if strict is not None and b == nb - 1:
            lf = jnp.where(strict, lf, 0.0)
        parts.append(lf.astype(BF16))
    c_all = jnp.dot(jnp.concatenate(parts, axis=0), tri, preferred_element_type=F32)
    ps = [None] * nb
    for b in reversed(range(nb)):
        c = c_all[b * r:(b + 1) * r]
        p = jnp.exp2(sbs[b] + (c + carry))
        if strict is not None and b == nb - 1:
            p = jnp.where(strict, p, 0.0)
        ps[b] = p.astype(BF16)
        carry = carry + c[:, 0:1]
    return ps, carry


def _qk(q, k):
    return lax.dot_general(q, k, (((1,), (1,)), ((), ())), preferred_element_type=F32)


def _pv(ps, v):
    bw = ps[0].shape[1]
    acc = None
    for b, p in enumerate(ps):
        d = jnp.dot(p, v[b * bw:(b + 1) * bw], preferred_element_type=F32)
        acc = d if acc is None else acc + d
    return acc


def _head_cols(h):
    return slice(h * SB_HEAD_DIM, (h + 1) * SB_HEAD_DIM)


def _sb_prompt_kernel(q_ref, k_ref, v_ref, sz_ref, tri_ref, o_ref):
    tq = SB_TQ
    heads = range(SB_HEADS_PER_STEP)
    kb_ref, vb_ref = k_ref.at[0], v_ref.at[0]
    kmax = []
    for h in heads:
        kf = kb_ref[:, _head_cols(h)].astype(F32)
        kmax.append(jnp.sqrt(jnp.max(jnp.sum(kf * kf, axis=-1, keepdims=True), axis=0, keepdims=True)))

    tri = tri_ref[...]
    row = lax.broadcasted_iota(jnp.int32, (tq, SB_BLOCK), 0)
    col = lax.broadcasted_iota(jnp.int32, (tq, SB_BLOCK), 1)
    strict = col < row
    zero_carry = jnp.zeros((tq, 1), F32)

    def chunk(h, q, k0, width, mask, carry):
        s = _qk(q, kb_ref[pl.ds(k0, width), _head_cols(h)])
        ps, carry = _sb_chunk(s, mask, carry, tri)
        return _pv(ps, vb_ref[pl.ds(k0, width), _head_cols(h)]), carry

    def store(h, rows, acc):
        o_ref[0, rows, _head_cols(h)] = (acc * sz_ref[0, rows, _head_cols(h)].astype(F32)).astype(o_ref.dtype)

    for h in heads:
        acc, _ = chunk(h, q_ref[0, 0:tq, _head_cols(h)], 0, tq, strict, zero_carry)
        store(h, slice(0, tq), acc)

    def q_block(qi, _):
        rows = pl.ds(pl.multiple_of(qi * tq, tq), tq)
        qs = [q_ref[0, rows, _head_cols(h)] for h in heads]
        bounds = []
        for h in heads:
            qf = qs[h].astype(F32)
            bounds.append(jnp.sqrt(jnp.sum(qf * qf, axis=-1, keepdims=True)) * kmax[h] * 1.001)

        def alive(carries):
            worst = carries[0] + bounds[0]
            for h in heads[1:]:
                worst = jnp.maximum(worst, carries[h] + bounds[h])
            return (jnp.max(worst) > SB_DEAD_LOG).astype(jnp.int32)

        k0 = pl.multiple_of((qi - 1) * SB_BLOCK, SB_BLOCK)
        first = [chunk(h, qs[h], k0, SB_BLOCK + tq, strict, zero_carry) for h in heads]
        accs = tuple(f[0] for f in first)
        carries = tuple(f[1] for f in first)

        def cond(st):
            return jnp.logical_and(st[0] >= 0, st[1] > 0)

        def body(st):
            kb_idx, _, carries, accs = st
            k0 = pl.multiple_of(kb_idx * SB_BLOCK, SB_BLOCK)
            nxt = [chunk(h, qs[h], k0, SB_BLOCK, None, carries[h]) for h in heads]
            carries = tuple(n[1] for n in nxt)
            accs = tuple(a + n[0] for a, n in zip(accs, nxt))
            return kb_idx - 1, alive(carries), carries, accs

        st = lax.while_loop(cond, body, (qi - 2, alive(carries), carries, accs))
        for h in heads:
            store(h, rows, st[3][h])
        return 0

    lax.fori_loop(1, q_ref.shape[1] // tq, q_block, 0)


def _sb_prompt(q, k, v, sz, tri):
    b, l, _ = q.shape
    assert SB_TQ == SB_BLOCK and l % SB_TQ == 0
    gw = SB_HEADS_PER_STEP * SB_HEAD_DIM
    blk = pl.BlockSpec((1, l, gw), lambda bi, g: (bi, 0, g))
    return pl.pallas_call(
        _sb_prompt_kernel,
        grid=(b, SB_HEADS // SB_HEADS_PER_STEP),
        in_specs=[blk, blk, blk, blk, pl.BlockSpec((SB_BLOCK, SB_BLOCK), lambda bi, g: (0, 0))],
        out_specs=blk,
        out_shape=jax.ShapeDtypeStruct(q.shape, BF16),
        compiler_params=_params(2),
        name="sb_prompt",
    )(q, k, v, sz, tri)


def _sb_decode_kernel(q_ref, kn_ref, vn_ref, sz_ref, tri_ref, kc_ref, vc_ref, o_ref):
    nh = SB_HEADS
    past = kc_ref.shape[0] // nh
    tq = q_ref.shape[1]
    tri = tri_ref[...]
    row = lax.broadcasted_iota(jnp.int32, (tq, LANES), 0)
    col = lax.broadcasted_iota(jnp.int32, (tq, LANES), 1)
    pad = jnp.zeros((LANES - tq, SB_HEAD_DIM), BF16)
    for h in range(nh):
        cols = _head_cols(h)
        q = q_ref[0, :, cols]
        kn = jnp.concatenate([kn_ref[0, :, cols], pad], axis=0)
        vn = jnp.concatenate([vn_ref[0, :, cols], pad], axis=0)
        ps, carry = _sb_chunk(_qk(q, kn), col < row, jnp.zeros((tq, 1), F32), tri[:LANES, :LANES])
        acc = _pv(ps, vn)
        head_rows = pl.ds(h, past, stride=nh)
        ps, _ = _sb_chunk(_qk(q, kc_ref[head_rows, :].astype(BF16)), None, carry, tri)
        acc = acc + _pv(ps, vc_ref[head_rows, :].astype(BF16))
        o_ref[0, :, cols] = (acc * sz_ref[0, :, cols].astype(F32)).astype(o_ref.dtype)


def _sb_decode(q, k_new, v_new, k_cache, v_cache, sz, tri):
    b, t, w = q.shape
    _, p, nh, dh = k_cache.shape
    assert nh == SB_HEADS and dh == SB_HEAD_DIM
    blk = pl.BlockSpec((1, t, w), lambda bi: (bi, 0, 0))
    blk_cache = pl.BlockSpec((None, p * nh, dh), lambda bi: (bi, 0, 0))
    return pl.pallas_call(
        _sb_decode_kernel,
        grid=(b,),
        in_specs=[blk, blk, blk, blk, pl.BlockSpec((SB_BLOCK, SB_BLOCK), lambda bi: (0, 0)),
                  blk_cache, blk_cache],
        out_specs=blk,
        out_shape=jax.ShapeDtypeStruct(q.shape, BF16),
        compiler_params=_params(1),
        name="sb_decode",
    )(q, k_new, v_new, sz, tri, k_cache.reshape(b, p * nh, dh), v_cache.reshape(b, p * nh, dh))


def _chunk_causal(n):
    pos = jnp.arange(n)
    return (pos[None, :] // CHUNK) <= (pos[:, None] // CHUNK)


def _mm_tn(k, w_itemsize=4, out_bytes=8):
    for tn in (1024, 512, 256):
        if 2 * (MM_TM * k * 2 + k * tn * w_itemsize + MM_TM * tn * out_bytes) <= MM_VMEM_BUDGET:
            return tn
    raise ValueError("matmul blocks do not fit VMEM")


def _gm_mix_params(seq_len, w_s, b_s):
    blk = min(seq_len, GM_BLOCK)
    w = w_s[:, :blk, :blk] * _chunk_causal(blk).astype(w_s.dtype)
    reps = GM_BLOCK // blk
    wm = jnp.einsum("ab,gts->gatbs", jnp.eye(reps, dtype=w.dtype), w).reshape(
        GM_GROUPS, GM_BLOCK, GM_BLOCK).astype(BF16)
    return wm, jnp.tile(b_s[:, :blk].T, (reps, 1))


def _gm_layer(xs, seq_lens, norm_g, w_in, ln_g, ln_b, w_s, b_s, w_out):
    d = xs[0].shape[1]
    hs = [_rmsnorm(x, norm_g, BF16, min(x.shape[0], 1024)) for x in xs]
    tn = _mm_tn(d, out_bytes=2)
    ugs = _matmul(hs, w_in, BF16, tn=tn, n=2 * GM_WIDTH, act="gelu", name="gm_in_uv")
    szs = _matmul(hs, w_in, BF16, tn=tn, col0=2 * GM_WIDTH, n=GM_WIDTH, act="silu", name="gm_in_z")
    tn_out = _mm_tn(GM_WIDTH, w_out.dtype.itemsize)
    x_new = []
    for g, (x, ug, sz, seq_len) in enumerate(zip(xs, ugs, szs, seq_lens)):
        mix = _gm_mix_params(seq_len, w_s, b_s)
        if g < len(xs) - 1:
            x_new.append(_gm_mix_out(ug, sz, *mix, ln_g, ln_b, w_out, x, tm=GM_MIX_OUT_TM, tn=tn_out))
        else:
            y, v_rows = _gm_mix(ug, sz, *mix, ln_g, ln_b, tm=min(ug.shape[0], 256), emit_vn=True)
            x_new += _matmul([y], w_out, F32, tn=tn_out, res=[x], name="gm_out")
    return x_new, v_rows


def _sb_in(xs, norm_g, w_in):
    d = xs[0].shape[1]
    hs = [_rmsnorm(x, norm_g, BF16, min(x.shape[0], 1024)) for x in xs]
    tn = _mm_tn(d, out_bytes=2)
    w = SB_WIDTH
    q = _matmul(hs, w_in, BF16, tn=tn, n=w, scale=SB_HEAD_DIM ** -0.5 * LOG2_E, name="sb_in_q")
    k, kb = _matmul_heads(hs, w_in, col0=w, name="sb_in_k")
    v, vb = _matmul_heads(hs, w_in, col0=2 * w, name="sb_in_v")
    sz = _matmul(hs, w_in, BF16, tn=tn, col0=3 * w, n=w, act="silu", name="sb_in_z")
    return q, k, v, kb, vb, sz


def kernel(x_prompt, x_sample, cache_sb_k, cache_sb_v, norm_g, final_norm_g, gm_w_in, gm_ln_g, gm_ln_b,
           gm_w_s, gm_b_s, gm_w_out, sb_w_in, sb_w_out):
    bsz, seq, d = x_prompt.shape
    dbsz, dseq, _ = x_sample.shape
    xs = [x_prompt.reshape(bsz * seq, d), x_sample.reshape(dbsz * dseq, d)]

    tt = (jnp.arange(SB_BLOCK)[:, None] >= jnp.arange(SB_BLOCK)[None, :]).astype(BF16)

    gm_v_rows, kp_rows, vp_rows, ks_rows, vs_rows = [], [], [], [], []
    for i in range(DEPTH):
        j = i // N_MIXERS
        if i % N_MIXERS == 0:
            xs, v_new = _gm_layer(xs, (seq, dseq), norm_g[i], gm_w_in[j], gm_ln_g[j], gm_ln_b[j], gm_w_s[j],
                                  gm_b_s[j], gm_w_out[j].astype(BF16))
            gm_v_rows.append(v_new.reshape(dbsz, dseq, GM_WIDTH))
        else:
            q, k, v, kb, vb, sz = _sb_in(xs, norm_g[i], sb_w_in[j])
            shp_p, shp_s = (bsz, seq, SB_WIDTH), (dbsz, dseq, SB_WIDTH)
            yp = _sb_prompt(q[0].reshape(shp_p), kb[0].reshape(shp_p), vb[0].reshape(shp_p),
                            sz[0].reshape(shp_p), tt)
            ys = _sb_decode(q[1].reshape(shp_s), kb[1].reshape(shp_s), vb[1].reshape(shp_s), cache_sb_k[j],
                            cache_sb_v[j], sz[1].reshape(shp_s), tt)
            y = [yp.reshape(bsz * seq, SB_WIDTH), ys.reshape(dbsz * dseq, SB_WIDTH)]
            if i == DEPTH - 1:
                xs = _matmul_res_norm(y, sb_w_out[j].astype(BF16), xs, final_norm_g, tm=512, name="sb_out_norm")
            else:
                xs = _matmul(y, sb_w_out[j], F32, tn=_mm_tn(SB_WIDTH), res=xs, name="sb_out")
            kp_rows.append(k[0].reshape(bsz, seq, SB_HEADS, SB_HEAD_DIM))
            vp_rows.append(v[0].reshape(bsz, seq, SB_HEADS, SB_HEAD_DIM))
            ks_rows.append(k[1].reshape(dbsz, dseq, SB_HEADS, SB_HEAD_DIM))
            vs_rows.append(v[1].reshape(dbsz, dseq, SB_HEADS, SB_HEAD_DIM))

    if DEPTH % N_MIXERS != 0:
        xs = [_rmsnorm(x, final_norm_g, F32, min(x.shape[0], 512)) for x in xs]
    y_prompt, y_sample = xs[0].reshape(bsz, seq, d), xs[1].reshape(dbsz, dseq, d)
    return (y_prompt, y_sample, jnp.stack(kp_rows), jnp.stack(vp_rows), jnp.stack(ks_rows),
            jnp.stack(vs_rows), jnp.stack(gm_v_rows))
```

```python
import functools

import jax
import jax.numpy as jnp
from jax import lax
from jax.experimental import pallas as pl
from jax.experimental.pallas import tpu as pltpu

D_MODEL = 2048
DEPTH = 2
CHUNK = 64
N_MIXERS = 2
GM_WIDTH = 2 * D_MODEL
GM_BLOCK = 128
GM_GROUPS = 16
GM_GROUP_DIM = GM_WIDTH // GM_GROUPS
SB_HEADS = 16
SB_HEAD_DIM = D_MODEL // SB_HEADS
SB_WIDTH = SB_HEADS * SB_HEAD_DIM
NORM_EPS = 1e-6
LN_EPS = 1e-5

LANES = 128
VMEM_LIMIT_BYTES = 56 * 1024 * 1024
MM_VMEM_BUDGET = 40 * 1024 * 1024
F32 = jnp.float32
BF16 = jnp.bfloat16


def _params(n_grid_dims):
    return pltpu.CompilerParams(
        dimension_semantics=("arbitrary",) * n_grid_dims,
        vmem_limit_bytes=VMEM_LIMIT_BYTES,
    )


def _gelu(x):
    c = 0.7978845608028654
    h = 0.5 * x
    return h + h * jnp.tanh(x * (c + (c * 0.044715) * (x * x)))


def _silu(x):
    h = 0.5 * x
    return h + h * jnp.tanh(h)


def _rmsnorm_kernel(x_ref, g_ref, o_ref):
    x = x_ref[...]
    ms = jnp.mean(x * x, axis=-1, keepdims=True)
    o_ref[...] = (x * lax.rsqrt(ms + NORM_EPS) * g_ref[...]).astype(o_ref.dtype)


def _rmsnorm(x, g, out_dtype, tm):
    m, d = x.shape
    return pl.pallas_call(
        _rmsnorm_kernel,
        grid=(m // tm,),
        in_specs=[pl.BlockSpec((tm, d), lambda i: (i, 0)),
                  pl.BlockSpec((1, d), lambda i: (0, 0))],
        out_specs=pl.BlockSpec((tm, d), lambda i: (i, 0)),
        out_shape=jax.ShapeDtypeStruct((m, d), out_dtype),
        compiler_params=_params(1),
        name="rmsnorm",
    )(x, g.reshape(1, d))


MM_TM = 1024
HEAD_TILE = 8


def _row_groups(ms, tm_max):
    tms = [min(m, tm_max) for m in ms]
    counts = [m // tm for m, tm in zip(ms, tms)]
    assert all(m == tm * c for m, tm, c in zip(ms, tms, counts))
    starts = [sum(counts[a + 1:]) for a in range(len(ms))]
    return tms, counts, starts


def _row_index(i, start, count):
    return jnp.clip(i - start, 0, count - 1)


def _on_group(i, a, counts, starts, fn):
    if len(counts) == 1:
        fn()
    else:
        pl.when(jnp.logical_and(i >= starts[a], i < starts[a] + counts[a]))(fn)


def _mm_kernel(*refs, groups, row_axis, act, scale, has_res):
    n_arr = len(groups[0])
    x_refs, w_ref = refs[:n_arr], refs[n_arr]
    r_refs = refs[n_arr + 1:2 * n_arr + 1] if has_res else None
    o_refs = refs[-n_arr:]
    i = pl.program_id(row_axis)

    def tile(a):
        acc = jnp.dot(x_refs[a][...], w_ref[...].astype(BF16), preferred_element_type=F32)
        if act == "gelu":
            acc = _gelu(acc)
        elif act == "silu":
            acc = _silu(acc)
        if scale is not None:
            acc = acc * scale
        if has_res:
            acc = acc + r_refs[a][...]
        o_refs[a][...] = acc.astype(o_refs[a].dtype)

    for a in range(n_arr):
        _on_group(i, a, *groups, functools.partial(tile, a))


def _matmul(xs, w, out_dtype, *, tn, col0=0, n=None, act=None, scale=None, res=None, rows_outer=False,
            name="matmul"):
    k = xs[0].shape[1]
    n = w.shape[1] if n is None else n
    assert col0 % tn == 0 and n % tn == 0
    jb = col0 // tn
    n_col = n // tn
    tms, counts, starts = _row_groups([x.shape[0] for x in xs], MM_TM)
    groups = list(zip(tms, counts, starts))
    if rows_outer:
        def spec(shape, fn):
            return pl.BlockSpec(shape, lambda i, j: fn(i, j))

        def col_index(i, j, s, c):
            return jnp.where(i < s, 0, jnp.where(i >= s + c, n_col - 1, j))
    else:
        def spec(shape, fn):
            return pl.BlockSpec(shape, lambda j, i: fn(i, j))

        def col_index(i, j, s, c):
            return j
    in_specs = [spec((tm, k), lambda i, j, s=s, c=c: (_row_index(i, s, c), 0)) for tm, c, s in groups]
    in_specs.append(spec((k, tn), lambda i, j: (0, j + jb)))
    row_col = [spec((tm, tn), lambda i, j, s=s, c=c: (_row_index(i, s, c), col_index(i, j, s, c)))
               for tm, c, s in groups]
    args = [*xs, w]
    if res is not None:
        in_specs += row_col
        args += list(res)
    return pl.pallas_call(
        functools.partial(_mm_kernel, groups=(counts, starts), row_axis=0 if rows_outer else 1, act=act,
                          scale=scale, has_res=res is not None),
        grid=(sum(counts), n_col) if rows_outer else (n_col, sum(counts)),
        in_specs=in_specs,
        out_specs=row_col,
        out_shape=[jax.ShapeDtypeStruct((x.shape[0], n), out_dtype) for x in xs],
        compiler_params=_params(2),
        name=name,
    )(*args)


def _mm_heads_kernel(*refs, groups):
    n_arr = len(groups[0])
    x_refs, w_ref = refs[:n_arr], refs[n_arr]
    of_refs, ob_refs = refs[n_arr + 1:2 * n_arr + 1], refs[2 * n_arr + 1:]
    i = pl.program_id(1)

    def tile(a):
        tm = x_refs[a].shape[0]
        acc = jnp.dot(x_refs[a][...], w_ref[...].astype(BF16), preferred_element_type=F32)
        ob_refs[a][...] = acc.astype(ob_refs[a].dtype)
        rows = of_refs[a].reshape(tm * HEAD_TILE, SB_HEAD_DIM)
        for h in range(HEAD_TILE):
            rows[pl.ds(h, tm, stride=HEAD_TILE), :] = acc[:, _head_cols(h)]

    for a in range(n_arr):
        _on_group(i, a, *groups, functools.partial(tile, a))


def _matmul_heads(xs, w, *, col0, name):
    k = xs[0].shape[1]
    tn = HEAD_TILE * SB_HEAD_DIM
    assert col0 % tn == 0
    jb = col0 // tn
    tms, counts, starts = _row_groups([x.shape[0] for x in xs], MM_TM)
    groups = list(zip(tms, counts, starts))
    in_specs = [pl.BlockSpec((tm, k), lambda j, i, s=s, c=c: (_row_index(i, s, c), 0)) for tm, c, s in groups]
    in_specs.append(pl.BlockSpec((k, tn), lambda j, i: (0, j + jb)))
    out_specs = [pl.BlockSpec((tm, HEAD_TILE, SB_HEAD_DIM), lambda j, i, s=s, c=c: (_row_index(i, s, c), j, 0))
                 for tm, c, s in groups]
    out_specs += [pl.BlockSpec((tm, tn), lambda j, i, s=s, c=c: (_row_index(i, s, c), j)) for tm, c, s in groups]
    outs = pl.pallas_call(
        functools.partial(_mm_heads_kernel, groups=(counts, starts)),
        grid=(SB_WIDTH // tn, sum(counts)),
        in_specs=in_specs,
        out_specs=out_specs,
        out_shape=[jax.ShapeDtypeStruct((x.shape[0], SB_HEADS, SB_HEAD_DIM), F32) for x in xs]
        + [jax.ShapeDtypeStruct((x.shape[0], SB_WIDTH), BF16) for x in xs],
        compiler_params=_params(2),
        name=name,
    )(*xs, w)
    return outs[:len(xs)], outs[len(xs):]


def _mm_res_norm_kernel(*refs, groups):
    n_arr = len(groups[0])
    x_refs, w_ref = refs[:n_arr], refs[n_arr]
    r_refs, g_ref, o_refs = refs[n_arr + 1:2 * n_arr + 1], refs[2 * n_arr + 1], refs[2 * n_arr + 2:]
    i = pl.program_id(0)

    def tile(a):
        acc = jnp.dot(x_refs[a][...], w_ref[...], preferred_element_type=F32) + r_refs[a][...]
        ms = jnp.mean(acc * acc, axis=-1, keepdims=True)
        o_refs[a][...] = acc * lax.rsqrt(ms + NORM_EPS) * g_ref[...]

    for a in range(n_arr):
        _on_group(i, a, *groups, functools.partial(tile, a))


def _matmul_res_norm(xs, w, res, g, *, tm, name):
    k, n = w.shape
    tms, counts, starts = _row_groups([x.shape[0] for x in xs], tm)
    row = lambda cols: [pl.BlockSpec((t, cols), lambda i, s=s, c=c: (_row_index(i, s, c), 0))
                        for t, c, s in zip(tms, counts, starts)]
    return pl.pallas_call(
        functools.partial(_mm_res_norm_kernel, groups=(counts, starts)),
        grid=(sum(counts),),
        in_specs=[*row(k), pl.BlockSpec((k, n), lambda i: (0, 0)), *row(n),
                  pl.BlockSpec((1, n), lambda i: (0, 0))],
        out_specs=row(n),
        out_shape=[jax.ShapeDtypeStruct((x.shape[0], n), F32) for x in xs],
        compiler_params=_params(1),
        name=name,
    )(*xs, w, *res, g.reshape(1, n))


def _gm_mix_kernel(u_ref, gv_ref, sz_ref, wm_ref, bt_ref, lg_ref, lb_ref, *out_refs, emit_vn):
    if emit_vn:
        y_ref, vn_ref = out_refs
    else:
        (y_ref,) = out_refs
    tm = u_ref.shape[0]
    inv_w = 1.0 / GM_WIDTH
    s1 = jnp.zeros((tm, 1), F32)
    for g in range(GM_GROUPS):
        cols = slice(g * GM_GROUP_DIM, (g + 1) * GM_GROUP_DIM)
        s1 = s1 + jnp.sum(gv_ref[:, cols].astype(F32), axis=-1, keepdims=True)
    mu = s1 * inv_w
    s2 = jnp.zeros((tm, 1), F32)
    for g in range(GM_GROUPS):
        cols = slice(g * GM_GROUP_DIM, (g + 1) * GM_GROUP_DIM)
        xc = gv_ref[:, cols].astype(F32) - mu
        s2 = s2 + jnp.sum(xc * xc, axis=-1, keepdims=True)
    rstd = lax.rsqrt(s2 * inv_w + LN_EPS)
    for g in range(GM_GROUPS):
        cols = slice(g * GM_GROUP_DIM, (g + 1) * GM_GROUP_DIM)
        vn = (gv_ref[:, cols].astype(F32) - mu) * rstd * lg_ref[:, cols] + lb_ref[:, cols]
        if emit_vn:
            vn_ref[:, cols] = vn
        vnb = vn.astype(BF16)
        wg = wm_ref[g]
        bias = bt_ref[:, g:g + 1]
        for r in range(tm // GM_BLOCK):
            rows = slice(r * GM_BLOCK, (r + 1) * GM_BLOCK)
            mixed = jnp.dot(wg, vnb[rows], preferred_element_type=F32) + bias
            y_ref[rows, cols] = (u_ref[rows, cols] * mixed.astype(BF16)) * sz_ref[rows, cols]


def _gm_mix(ug, sz, wm, bt, ln_g, ln_b, *, tm, emit_vn):
    m = sz.shape[0]
    row_blk = lambda c: pl.BlockSpec((tm, GM_WIDTH), lambda i, c=c: (i, c))
    full = lambda shape: pl.BlockSpec(shape, lambda i: (0,) * len(shape))
    out_shape = [jax.ShapeDtypeStruct((m, GM_WIDTH), BF16)]
    out_specs = [row_blk(0)]
    if emit_vn:
        out_shape.append(jax.ShapeDtypeStruct((m, GM_WIDTH), F32))
        out_specs.append(row_blk(0))
    return pl.pallas_call(
        functools.partial(_gm_mix_kernel, emit_vn=emit_vn),
        grid=(m // tm,),
        in_specs=[row_blk(0), row_blk(1), row_blk(0),
                  full((GM_GROUPS, GM_BLOCK, GM_BLOCK)), full((GM_BLOCK, GM_GROUPS)),
                  full((1, GM_WIDTH)), full((1, GM_WIDTH))],
        out_specs=out_specs,
        out_shape=out_shape,
        compiler_params=_params(1),
        name="gm_mix",
    )(ug, ug, sz, wm, bt, ln_g.reshape(1, GM_WIDTH), ln_b.reshape(1, GM_WIDTH))


SB_BLOCK = 256
SB_TQ = 256
SB_HEADS_PER_STEP = 4
LOG2_E = 1.4426950408889634
SB_DEAD_LOG = -151.5


def _sb_chunk(s, strict, carry, tri):
    r = s.shape[0]
    bw = tri.shape[0]
    nb = s.shape[1] // bw
    sbs = [s[:, b * bw:(b + 1) * bw] for b in range(nb)]
    parts = []
    for b in range(nb):
        neg = -sbs[b]
        lf = jnp.minimum(neg, 0.0) - jnp.log2(1.0 + jnp.exp2(jnp.minimum(sbs[b], neg)))
        if strict is not None and b == nb - 1:
            lf = jnp.where(strict, lf, 0.0)
        parts.append(lf.astype(BF16))
    c_all = jnp.dot(jnp.concatenate(parts, axis=0), tri, preferred_element_type=F32)
    ps = [None] * nb
    for b in reversed(range(nb)):
        c = c_all[b * r:(b + 1) * r]
        p = jnp.exp2(sbs[b] + (c + carry))
        if strict is not None and b == nb - 1:
            p = jnp.where(strict, p, 0.0)
        ps[b] = p.astype(BF16)
        carry = carry + c[:, 0:1]
    return ps, carry


def _qk(q, k):
    return lax.dot_general(q, k, (((1,), (1,)), ((), ())), preferred_element_type=F32)


def _pv(ps, v):
    bw = ps[0].shape[1]
    acc = None
    for b, p in enumerate(ps):
        d = jnp.dot(p, v[b * bw:(b + 1) * bw], preferred_element_type=F32)
        acc = d if acc is None else acc + d
    return acc


def _head_cols(h):
    return slice(h * SB_HEAD_DIM, (h + 1) * SB_HEAD_DIM)


def _sb_prompt_kernel(q_ref, k_ref, v_ref, sz_ref, tri_ref, o_ref):
    tq = SB_TQ
    heads = range(SB_HEADS_PER_STEP)
    kb_ref, vb_ref = k_ref.at[0], v_ref.at[0]
    kmax = []
    for h in heads:
        ka = jnp.abs(kb_ref[:, _head_cols(h)].astype(F32))
        kmax.append(jnp.max(jnp.max(ka, axis=0, keepdims=True), axis=1, keepdims=True))

    tri = tri_ref[...]
    row = lax.broadcasted_iota(jnp.int32, (tq, SB_BLOCK), 0)
    col = lax.broadcasted_iota(jnp.int32, (tq, SB_BLOCK), 1)
    strict = col < row
    zero_carry = jnp.zeros((tq, 1), F32)

    def chunk(h, q, k0, width, mask, carry):
        s = _qk(q, kb_ref[pl.ds(k0, width), _head_cols(h)])
        ps, carry = _sb_chunk(s, mask, carry, tri)
        return _pv(ps, vb_ref[pl.ds(k0, width), _head_cols(h)]), carry

    def store(h, rows, acc):
        o_ref[0, rows, _head_cols(h)] = (acc * sz_ref[0, rows, _head_cols(h)].astype(F32)).astype(o_ref.dtype)

    for h in heads:
        acc, _ = chunk(h, q_ref[0, 0:tq, _head_cols(h)], 0, tq, strict, zero_carry)
        store(h, slice(0, tq), acc)

    def q_block(qi, _):
        rows = pl.ds(pl.multiple_of(qi * tq, tq), tq)
        qs = [q_ref[0, rows, _head_cols(h)] for h in heads]
        bounds = []
        for h in heads:
            q_l1 = jnp.sum(jnp.abs(qs[h].astype(F32)), axis=-1, keepdims=True)
            bounds.append(q_l1 * kmax[h] * 1.001)

        def alive(carries):
            worst = carries[0] + bounds[0]
            for h in heads[1:]:
                worst = jnp.maximum(worst, carries[h] + bounds[h])
            return (jnp.max(worst) > SB_DEAD_LOG).astype(jnp.int32)

        k0 = pl.multiple_of((qi - 1) * SB_BLOCK, SB_BLOCK)
        first = [chunk(h, qs[h], k0, SB_BLOCK + tq, strict, zero_carry) for h in heads]
        accs = tuple(f[0] for f in first)
        carries = tuple(f[1] for f in first)

        def cond(st):
            return jnp.logical_and(st[0] >= 0, st[1] > 0)

        def body(st):
            kb_idx, _, carries, accs = st
            k0 = pl.multiple_of(kb_idx * SB_BLOCK, SB_BLOCK)
            nxt = [chunk(h, qs[h], k0, SB_BLOCK, None, carries[h]) for h in heads]
            carries = tuple(n[1] for n in nxt)
            accs = tuple(a + n[0] for a, n in zip(accs, nxt))
            return kb_idx - 1, alive(carries), carries, accs

        st = lax.while_loop(cond, body, (qi - 2, alive(carries), carries, accs))
        for h in heads:
            store(h, rows, st[3][h])
        return 0

    lax.fori_loop(1, q_ref.shape[1] // tq, q_block, 0)


def _sb_prompt(q, k, v, sz, tri):
    b, l, _ = q.shape
    assert SB_TQ == SB_BLOCK and l % SB_TQ == 0
    gw = SB_HEADS_PER_STEP * SB_HEAD_DIM
    blk = pl.BlockSpec((1, l, gw), lambda bi, g: (bi, 0, g))
    return pl.pallas_call(
        _sb_prompt_kernel,
        grid=(b, SB_HEADS // SB_HEADS_PER_STEP),
        in_specs=[blk, blk, blk, blk, pl.BlockSpec((SB_BLOCK, SB_BLOCK), lambda bi, g: (0, 0))],
        out_specs=blk,
        out_shape=jax.ShapeDtypeStruct(q.shape, BF16),
        compiler_params=_params(2),
        name="sb_prompt",
    )(q, k, v, sz, tri)


def _sb_decode_kernel(q_ref, kn_ref, vn_ref, sz_ref, tri_ref, kc_ref, vc_ref, o_ref):
    nh = SB_HEADS
    past = kc_ref.shape[0] // nh
    tq = q_ref.shape[1]
    tri = tri_ref[...]
    row = lax.broadcasted_iota(jnp.int32, (tq, LANES), 0)
    col = lax.broadcasted_iota(jnp.int32, (tq, LANES), 1)
    pad = jnp.zeros((LANES - tq, SB_HEAD_DIM), BF16)
    for h in range(nh):
        cols = _head_cols(h)
        q = q_ref[0, :, cols]
        kn = jnp.concatenate([kn_ref[0, :, cols], pad], axis=0)
        vn = jnp.concatenate([vn_ref[0, :, cols], pad], axis=0)
        ps, carry = _sb_chunk(_qk(q, kn), col < row, jnp.zeros((tq, 1), F32), tri[:LANES, :LANES])
        acc = _pv(ps, vn)
        head_rows = pl.ds(h, past, stride=nh)
        ps, _ = _sb_chunk(_qk(q, kc_ref[head_rows, :].astype(BF16)), None, carry, tri)
        acc = acc + _pv(ps, vc_ref[head_rows, :].astype(BF16))
        o_ref[0, :, cols] = (acc * sz_ref[0, :, cols].astype(F32)).astype(o_ref.dtype)


def _sb_decode(q, k_new, v_new, k_cache, v_cache, sz, tri):
    b, t, w = q.shape
    _, p, nh, dh = k_cache.shape
    assert nh == SB_HEADS and dh == SB_HEAD_DIM
    blk = pl.BlockSpec((1, t, w), lambda bi: (bi, 0, 0))
    blk_cache = pl.BlockSpec((None, p * nh, dh), lambda bi: (bi, 0, 0))
    return pl.pallas_call(
        _sb_decode_kernel,
        grid=(b,),
        in_specs=[blk, blk, blk, blk, pl.BlockSpec((SB_BLOCK, SB_BLOCK), lambda bi: (0, 0)),
                  blk_cache, blk_cache],
        out_specs=blk,
        out_shape=jax.ShapeDtypeStruct(q.shape, BF16),
        compiler_params=_params(1),
        name="sb_decode",
    )(q, k_new, v_new, sz, tri, k_cache.reshape(b, p * nh, dh), v_cache.reshape(b, p * nh, dh))


def _chunk_causal(n):
    pos = jnp.arange(n)
    return (pos[None, :] // CHUNK) <= (pos[:, None] // CHUNK)


def _mm_tn(k, w_itemsize=4, out_bytes=8):
    for tn in (1024, 512, 256):
        if 2 * (MM_TM * k * 2 + k * tn * w_itemsize + MM_TM * tn * out_bytes) <= MM_VMEM_BUDGET:
            return tn
    raise ValueError("matmul blocks do not fit VMEM")


def _gm_mix_params(seq_len, w_s, b_s):
    blk = min(seq_len, GM_BLOCK)
    w = w_s[:, :blk, :blk] * _chunk_causal(blk).astype(w_s.dtype)
    reps = GM_BLOCK // blk
    wm = jnp.einsum("ab,gts->gatbs", jnp.eye(reps, dtype=w.dtype), w).reshape(
        GM_GROUPS, GM_BLOCK, GM_BLOCK).astype(BF16)
    return wm, jnp.tile(b_s[:, :blk].T, (reps, 1))


def _gm_layer(xs, seq_lens, norm_g, w_in, ln_g, ln_b, w_s, b_s, w_out):
    d = xs[0].shape[1]
    hs = [_rmsnorm(x, norm_g, BF16, min(x.shape[0], 1024)) for x in xs]
    tn = _mm_tn(d, out_bytes=2)
    ugs = _matmul(hs, w_in, BF16, tn=tn, n=2 * GM_WIDTH, act="gelu", name="gm_in_uv")
    szs = _matmul(hs, w_in, BF16, tn=tn, col0=2 * GM_WIDTH, n=GM_WIDTH, act="silu", name="gm_in_z")
    ys, v_rows = [], None
    for g, (ug, sz, seq_len) in enumerate(zip(ugs, szs, seq_lens)):
        emit_vn = g == len(xs) - 1
        outs = _gm_mix(ug, sz, *_gm_mix_params(seq_len, w_s, b_s), ln_g, ln_b,
                       tm=min(ug.shape[0], 256), emit_vn=emit_vn)
        ys.append(outs[0])
        v_rows = outs[1] if emit_vn else v_rows
    x_new = _matmul(ys, w_out, F32, tn=_mm_tn(GM_WIDTH, w_out.dtype.itemsize), res=xs, rows_outer=True,
                    name="gm_out")
    return x_new, v_rows


def _sb_in(xs, norm_g, w_in):
    d = xs[0].shape[1]
    hs = [_rmsnorm(x, norm_g, BF16, min(x.shape[0], 1024)) for x in xs]
    tn = _mm_tn(d, out_bytes=2)
    w = SB_WIDTH
    q = _matmul(hs, w_in, BF16, tn=tn, n=w, scale=SB_HEAD_DIM ** -0.5 * LOG2_E, name="sb_in_q")
    k, kb = _matmul_heads(hs, w_in, col0=w, name="sb_in_k")
    v, vb = _matmul_heads(hs, w_in, col0=2 * w, name="sb_in_v")
    sz = _matmul(hs, w_in, BF16, tn=tn, col0=3 * w, n=w, act="silu", name="sb_in_z")
    return q, k, v, kb, vb, sz


def kernel(x_prompt, x_sample, cache_sb_k, cache_sb_v, norm_g, final_norm_g, gm_w_in, gm_ln_g, gm_ln_b,
           gm_w_s, gm_b_s, gm_w_out, sb_w_in, sb_w_out):
    bsz, seq, d = x_prompt.shape
    dbsz, dseq, _ = x_sample.shape
    xs = [x_prompt.reshape(bsz * seq, d), x_sample.reshape(dbsz * dseq, d)]

    tt = (jnp.arange(SB_BLOCK)[:, None] >= jnp.arange(SB_BLOCK)[None, :]).astype(BF16)

    gm_v_rows, kp_rows, vp_rows, ks_rows, vs_rows = [], [], [], [], []
    for i in range(DEPTH):
        j = i // N_MIXERS
        if i % N_MIXERS == 0:
            xs, v_new = _gm_layer(xs, (seq, dseq), norm_g[i], gm_w_in[j], gm_ln_g[j], gm_ln_b[j], gm_w_s[j],
                                  gm_b_s[j], gm_w_out[j].astype(BF16))
            gm_v_rows.append(v_new.reshape(dbsz, dseq, GM_WIDTH))
        else:
            q, k, v, kb, vb, sz = _sb_in(xs, norm_g[i], sb_w_in[j])
            shp_p, shp_s = (bsz, seq, SB_WIDTH), (dbsz, dseq, SB_WIDTH)
            yp = _sb_prompt(q[0].reshape(shp_p), kb[0].reshape(shp_p), vb[0].reshape(shp_p),
                            sz[0].reshape(shp_p), tt)
            ys = _sb_decode(q[1].reshape(shp_s), kb[1].reshape(shp_s), vb[1].reshape(shp_s), cache_sb_k[j],
                            cache_sb_v[j], sz[1].reshape(shp_s), tt)
            y = [yp.reshape(bsz * seq, SB_WIDTH), ys.reshape(dbsz * dseq, SB_WIDTH)]
            if i == DEPTH - 1:
                xs = _matmul_res_norm(y, sb_w_out[j].astype(BF16), xs, final_norm_g, tm=512, name="sb_out_norm")
            else:
                xs = _matmul(y, sb_w_out[j], F32, tn=_mm_tn(SB_WIDTH), res=xs, name="sb_out")
            kp_rows.append(k[0].reshape(bsz, seq, SB_HEADS, SB_HEAD_DIM))
            vp_rows.append(v[0].reshape(bsz, seq, SB_HEADS, SB_HEAD_DIM))
            ks_rows.append(k[1].reshape(dbsz, dseq, SB_HEADS, SB_HEAD_DIM))
            vs_rows.append(v[1].reshape(dbsz, dseq, SB_HEADS, SB_HEAD_DIM))

    if DEPTH % N_MIXERS != 0:
        xs = [_rmsnorm(x, final_norm_g, F32, min(x.shape[0], 512)) for x in xs]
    y_prompt, y_sample = xs[0].reshape(bsz, seq, d), xs[1].reshape(dbsz, dseq, d)
    return (y_prompt, y_sample, jnp.stack(kp_rows), jnp.stack(vp_rows), jnp.stack(ks_rows),
            jnp.stack(vs_rows), jnp.stack(gm_v_rows))
```

```python
import functools

import jax
import jax.numpy as jnp
from jax import lax
from jax.experimental import pallas as pl
from jax.experimental.pallas import tpu as pltpu

D_MODEL = 2048
DEPTH = 2
CHUNK = 64
N_MIXERS = 2
GM_WIDTH = 2 * D_MODEL
GM_BLOCK = 128
GM_GROUPS = 16
GM_GROUP_DIM = GM_WIDTH // GM_GROUPS
SB_HEADS = 16
SB_HEAD_DIM = D_MODEL // SB_HEADS
SB_WIDTH = SB_HEADS * SB_HEAD_DIM
NORM_EPS = 1e-6
LN_EPS = 1e-5

LANES = 128
VMEM_LIMIT_BYTES = 56 * 1024 * 1024
MM_VMEM_BUDGET = 40 * 1024 * 1024
F32 = jnp.float32
BF16 = jnp.bfloat16


def _params(n_grid_dims):
    return pltpu.CompilerParams(
        dimension_semantics=("arbitrary",) * n_grid_dims,
        vmem_limit_bytes=VMEM_LIMIT_BYTES,
    )


def _gelu(x):
    c = 0.7978845608028654
    h = 0.5 * x
    return h + h * jnp.tanh(x * (c + (c * 0.044715) * (x * x)))


def _silu(x):
    h = 0.5 * x
    return h + h * jnp.tanh(h)


def _rmsnorm_kernel(x_ref, g_ref, o_ref):
    x = x_ref[...]
    ms = jnp.mean(x * x, axis=-1, keepdims=True)
    o_ref[...] = (x * lax.rsqrt(ms + NORM_EPS) * g_ref[...]).astype(o_ref.dtype)


def _rmsnorm(x, g, out_dtype, tm):
    m, d = x.shape
    return pl.pallas_call(
        _rmsnorm_kernel,
        grid=(m // tm,),
        in_specs=[pl.BlockSpec((tm, d), lambda i: (i, 0)),
                  pl.BlockSpec((1, d), lambda i: (0, 0))],
        out_specs=pl.BlockSpec((tm, d), lambda i: (i, 0)),
        out_shape=jax.ShapeDtypeStruct((m, d), out_dtype),
        compiler_params=_params(1),
        name="rmsnorm",
    )(x, g.reshape(1, d))


MM_TM = 1024
GM_MIX_TM = 512
HEAD_TILE = 8


def _row_groups(ms, tm_max):
    tms = [min(m, tm_max) for m in ms]
    counts = [m // tm for m, tm in zip(ms, tms)]
    assert all(m == tm * c for m, tm, c in zip(ms, tms, counts))
    starts = [sum(counts[a + 1:]) for a in range(len(ms))]
    return tms, counts, starts


def _row_index(i, start, count):
    return jnp.clip(i - start, 0, count - 1)


def _on_group(i, a, counts, starts, fn):
    if len(counts) == 1:
        fn()
    else:
        pl.when(jnp.logical_and(i >= starts[a], i < starts[a] + counts[a]))(fn)


def _mm_kernel(*refs, groups, act, scale, has_res):
    n_arr = len(groups[0])
    x_refs, w_ref = refs[:n_arr], refs[n_arr]
    r_refs = refs[n_arr + 1:2 * n_arr + 1] if has_res else None
    o_refs = refs[-n_arr:]
    i = pl.program_id(1)

    def tile(a):
        acc = jnp.dot(x_refs[a][...], w_ref[...].astype(BF16), preferred_element_type=F32)
        if act == "gelu":
            acc = _gelu(acc)
        elif act == "silu":
            acc = _silu(acc)
        if scale is not None:
            acc = acc * scale
        if has_res:
            acc = acc + r_refs[a][...]
        o_refs[a][...] = acc.astype(o_refs[a].dtype)

    for a in range(n_arr):
        _on_group(i, a, *groups, functools.partial(tile, a))


def _matmul(xs, w, out_dtype, *, tn, col0=0, n=None, act=None, scale=None, res=None, name="matmul"):
    k = xs[0].shape[1]
    n = w.shape[1] if n is None else n
    assert col0 % tn == 0 and n % tn == 0
    jb = col0 // tn
    tms, counts, starts = _row_groups([x.shape[0] for x in xs], MM_TM)
    in_specs = [pl.BlockSpec((tm, k), lambda j, i, s=s, c=c: (_row_index(i, s, c), 0))
                for tm, c, s in zip(tms, counts, starts)]
    in_specs.append(pl.BlockSpec((k, tn), lambda j, i: (0, j + jb)))
    row_col = [pl.BlockSpec((tm, tn), lambda j, i, s=s, c=c: (_row_index(i, s, c), j))
               for tm, c, s in zip(tms, counts, starts)]
    args = [*xs, w]
    if res is not None:
        in_specs += row_col
        args += list(res)
    return pl.pallas_call(
        functools.partial(_mm_kernel, groups=(counts, starts), act=act, scale=scale, has_res=res is not None),
        grid=(n // tn, sum(counts)),
        in_specs=in_specs,
        out_specs=row_col,
        out_shape=[jax.ShapeDtypeStruct((x.shape[0], n), out_dtype) for x in xs],
        compiler_params=_params(2),
        name=name,
    )(*args)


def _mm_heads_kernel(*refs, groups):
    n_arr = len(groups[0])
    x_refs, w_ref = refs[:n_arr], refs[n_arr]
    of_refs, ob_refs = refs[n_arr + 1:2 * n_arr + 1], refs[2 * n_arr + 1:]
    i = pl.program_id(1)

    def tile(a):
        tm = x_refs[a].shape[0]
        acc = jnp.dot(x_refs[a][...], w_ref[...].astype(BF16), preferred_element_type=F32)
        ob_refs[a][...] = acc.astype(ob_refs[a].dtype)
        rows = of_refs[a].reshape(tm * HEAD_TILE, SB_HEAD_DIM)
        for h in range(HEAD_TILE):
            rows[pl.ds(h, tm, stride=HEAD_TILE), :] = acc[:, _head_cols(h)]

    for a in range(n_arr):
        _on_group(i, a, *groups, functools.partial(tile, a))


def _matmul_heads(xs, w, *, col0, name):
    k = xs[0].shape[1]
    tn = HEAD_TILE * SB_HEAD_DIM
    assert col0 % tn == 0
    jb = col0 // tn
    tms, counts, starts = _row_groups([x.shape[0] for x in xs], MM_TM)
    groups = list(zip(tms, counts, starts))
    in_specs = [pl.BlockSpec((tm, k), lambda j, i, s=s, c=c: (_row_index(i, s, c), 0)) for tm, c, s in groups]
    in_specs.append(pl.BlockSpec((k, tn), lambda j, i: (0, j + jb)))
    out_specs = [pl.BlockSpec((tm, HEAD_TILE, SB_HEAD_DIM), lambda j, i, s=s, c=c: (_row_index(i, s, c), j, 0))
                 for tm, c, s in groups]
    out_specs += [pl.BlockSpec((tm, tn), lambda j, i, s=s, c=c: (_row_index(i, s, c), j)) for tm, c, s in groups]
    outs = pl.pallas_call(
        functools.partial(_mm_heads_kernel, groups=(counts, starts)),
        grid=(SB_WIDTH // tn, sum(counts)),
        in_specs=in_specs,
        out_specs=out_specs,
        out_shape=[jax.ShapeDtypeStruct((x.shape[0], SB_HEADS, SB_HEAD_DIM), F32) for x in xs]
        + [jax.ShapeDtypeStruct((x.shape[0], SB_WIDTH), BF16) for x in xs],
        compiler_params=_params(2),
        name=name,
    )(*xs, w)
    return outs[:len(xs)], outs[len(xs):]


def _mm_res_norm_kernel(*refs, groups):
    n_arr = len(groups[0])
    x_refs, w_ref = refs[:n_arr], refs[n_arr]
    r_refs, g_ref, o_refs = refs[n_arr + 1:2 * n_arr + 1], refs[2 * n_arr + 1], refs[2 * n_arr + 2:]
    i = pl.program_id(0)

    def tile(a):
        acc = jnp.dot(x_refs[a][...], w_ref[...], preferred_element_type=F32) + r_refs[a][...]
        ms = jnp.mean(acc * acc, axis=-1, keepdims=True)
        o_refs[a][...] = acc * lax.rsqrt(ms + NORM_EPS) * g_ref[...]

    for a in range(n_arr):
        _on_group(i, a, *groups, functools.partial(tile, a))


def _matmul_res_norm(xs, w, res, g, *, tm, name):
    k, n = w.shape
    tms, counts, starts = _row_groups([x.shape[0] for x in xs], tm)
    row = lambda cols: [pl.BlockSpec((t, cols), lambda i, s=s, c=c: (_row_index(i, s, c), 0))
                        for t, c, s in zip(tms, counts, starts)]
    return pl.pallas_call(
        functools.partial(_mm_res_norm_kernel, groups=(counts, starts)),
        grid=(sum(counts),),
        in_specs=[*row(k), pl.BlockSpec((k, n), lambda i: (0, 0)), *row(n),
                  pl.BlockSpec((1, n), lambda i: (0, 0))],
        out_specs=row(n),
        out_shape=[jax.ShapeDtypeStruct((x.shape[0], n), F32) for x in xs],
        compiler_params=_params(1),
        name=name,
    )(*xs, w, *res, g.reshape(1, n))


def _gm_mix_kernel(u_ref, gv_ref, sz_ref, wm_ref, bt_ref, lg_ref, lb_ref, *out_refs, emit_vn):
    if emit_vn:
        y_ref, vn_ref = out_refs
    else:
        (y_ref,) = out_refs
    tm = u_ref.shape[0]
    inv_w = 1.0 / GM_WIDTH
    s1 = jnp.zeros((tm, 1), F32)
    for g in range(GM_GROUPS):
        cols = slice(g * GM_GROUP_DIM, (g + 1) * GM_GROUP_DIM)
        s1 = s1 + jnp.sum(gv_ref[:, cols].astype(F32), axis=-1, keepdims=True)
    mu = s1 * inv_w
    s2 = jnp.zeros((tm, 1), F32)
    for g in range(GM_GROUPS):
        cols = slice(g * GM_GROUP_DIM, (g + 1) * GM_GROUP_DIM)
        xc = gv_ref[:, cols].astype(F32) - mu
        s2 = s2 + jnp.sum(xc * xc, axis=-1, keepdims=True)
    rstd = lax.rsqrt(s2 * inv_w + LN_EPS)
    for g in range(GM_GROUPS):
        cols = slice(g * GM_GROUP_DIM, (g + 1) * GM_GROUP_DIM)
        vn = (gv_ref[:, cols].astype(F32) - mu) * rstd * lg_ref[:, cols] + lb_ref[:, cols]
        if emit_vn:
            vn_ref[:, cols] = vn
        vnb = vn.astype(BF16)
        wg = wm_ref[g]
        bias = bt_ref[:, g:g + 1]
        for r in range(tm // GM_BLOCK):
            rows = slice(r * GM_BLOCK, (r + 1) * GM_BLOCK)
            mixed = jnp.dot(wg, vnb[rows], preferred_element_type=F32) + bias
            y_ref[rows, cols] = (u_ref[rows, cols] * mixed.astype(BF16)) * sz_ref[rows, cols]


def _gm_mix(ug, sz, wm, bt, ln_g, ln_b, *, tm, emit_vn):
    m = sz.shape[0]
    row_blk = lambda c: pl.BlockSpec((tm, GM_WIDTH), lambda i, c=c: (i, c))
    full = lambda shape: pl.BlockSpec(shape, lambda i: (0,) * len(shape))
    out_shape = [jax.ShapeDtypeStruct((m, GM_WIDTH), BF16)]
    out_specs = [row_blk(0)]
    if emit_vn:
        out_shape.append(jax.ShapeDtypeStruct((m, GM_WIDTH), F32))
        out_specs.append(row_blk(0))
    return pl.pallas_call(
        functools.partial(_gm_mix_kernel, emit_vn=emit_vn),
        grid=(m // tm,),
        in_specs=[row_blk(0), row_blk(1), row_blk(0),
                  full((GM_GROUPS, GM_BLOCK, GM_BLOCK)), full((GM_BLOCK, GM_GROUPS)),
                  full((1, GM_WIDTH)), full((1, GM_WIDTH))],
        out_specs=out_specs,
        out_shape=out_shape,
        compiler_params=_params(1),
        name="gm_mix",
    )(ug, ug, sz, wm, bt, ln_g.reshape(1, GM_WIDTH), ln_b.reshape(1, GM_WIDTH))


SB_BLOCK = 256
SB_TQ = 256
SB_HEADS_PER_STEP = 8
LOG2_E = 1.4426950408889634
SB_DEAD_LOG = -151.5


def _sb_chunk(s, strict, carry, tri):
    r = s.shape[0]
    bw = tri.shape[0]
    nb = s.shape[1] // bw
    sbs = [s[:, b * bw:(b + 1) * bw] for b in range(nb)]
    parts = []
    for b in range(nb):
        neg = -sbs[b]
        lf = jnp.minimum(neg, 0.0) - jnp.log2(1.0 + jnp.exp2(jnp.minimum(sbs[b], neg)))
        if strict is not None and b == nb - 1:
            lf = jnp.where(strict, lf, 0.0)
        parts.append(lf.astype(BF16))
    c_all = jnp.dot(jnp.concatenate(parts, axis=0), tri, preferred_element_type=F32)
    ps = [None] * nb
    for b in reversed(range(nb)):
        c = c_all[b * r:(b + 1) * r]
        p = jnp.exp2(sbs[b] + (c + carry))
        if strict is not None and b == nb - 1:
            p = jnp.where(strict, p, 0.0)
        ps[b] = p.astype(BF16)
        carry = carry + c[:, 0:1]
    return ps, carry


def _qk(q, k):
    return lax.dot_general(q, k, (((1,), (1,)), ((), ())), preferred_element_type=F32)


def _pv(ps, v):
    bw = ps[0].shape[1]
    acc = None
    for b, p in enumerate(ps):
        d = jnp.dot(p, v[b * bw:(b + 1) * bw], preferred_element_type=F32)
        acc = d if acc is None else acc + d
    return acc


def _head_cols(h):
    return slice(h * SB_HEAD_DIM, (h + 1) * SB_HEAD_DIM)


def _sb_prompt_kernel(q_ref, k_ref, v_ref, sz_ref, tri_ref, o_ref):
    tq = SB_TQ
    heads = range(SB_HEADS_PER_STEP)
    kb_ref, vb_ref = k_ref.at[0], v_ref.at[0]
    kmax = []
    for h in heads:
        ka = jnp.abs(kb_ref[:, _head_cols(h)].astype(F32))
        kmax.append(jnp.max(jnp.max(ka, axis=0, keepdims=True), axis=1, keepdims=True))

    tri = tri_ref[...]
    row = lax.broadcasted_iota(jnp.int32, (tq, SB_BLOCK), 0)
    col = lax.broadcasted_iota(jnp.int32, (tq, SB_BLOCK), 1)
    strict = col < row
    zero_carry = jnp.zeros((tq, 1), F32)

    def chunk(h, q, k0, width, mask, carry):
        s = _qk(q, kb_ref[pl.ds(k0, width), _head_cols(h)])
        ps, carry = _sb_chunk(s, mask, carry, tri)
        return _pv(ps, vb_ref[pl.ds(k0, width), _head_cols(h)]), carry

    def store(h, rows, acc):
        o_ref[0, rows, _head_cols(h)] = (acc * sz_ref[0, rows, _head_cols(h)].astype(F32)).astype(o_ref.dtype)

    for h in heads:
        acc, _ = chunk(h, q_ref[0, 0:tq, _head_cols(h)], 0, tq, strict, zero_carry)
        store(h, slice(0, tq), acc)

    def q_block(qi, _):
        rows = pl.ds(pl.multiple_of(qi * tq, tq), tq)
        qs = [q_ref[0, rows, _head_cols(h)] for h in heads]
        bounds = []
        for h in heads:
            q_l1 = jnp.sum(jnp.abs(qs[h].astype(F32)), axis=-1, keepdims=True)
            bounds.append(q_l1 * kmax[h] * 1.001)

        def alive(carries):
            worst = carries[0] + bounds[0]
            for h in heads[1:]:
                worst = jnp.maximum(worst, carries[h] + bounds[h])
            return (jnp.max(worst) > SB_DEAD_LOG).astype(jnp.int32)

        k0 = pl.multiple_of((qi - 1) * SB_BLOCK, SB_BLOCK)
        first = [chunk(h, qs[h], k0, SB_BLOCK + tq, strict, zero_carry) for h in heads]
        accs = tuple(f[0] for f in first)
        carries = tuple(f[1] for f in first)

        def cond(st):
            return jnp.logical_and(st[0] >= 0, st[1] > 0)

        def body(st):
            kb_idx, _, carries, accs = st
            k0 = pl.multiple_of(kb_idx * SB_BLOCK, SB_BLOCK)
            nxt = [chunk(h, qs[h], k0, SB_BLOCK, None, carries[h]) for h in heads]
            carries = tuple(n[1] for n in nxt)
            accs = tuple(a + n[0] for a, n in zip(accs, nxt))
            return kb_idx - 1, alive(carries), carries, accs

        st = lax.while_loop(cond, body, (qi - 2, alive(carries), carries, accs))
        for h in heads:
            store(h, rows, st[3][h])
        return 0

    lax.fori_loop(1, q_ref.shape[1] // tq, q_block, 0)


def _sb_prompt(q, k, v, sz, tri):
    b, l, _ = q.shape
    assert SB_TQ == SB_BLOCK and l % SB_TQ == 0
    gw = SB_HEADS_PER_STEP * SB_HEAD_DIM
    blk = pl.BlockSpec((1, l, gw), lambda bi, g: (bi, 0, g))
    return pl.pallas_call(
        _sb_prompt_kernel,
        grid=(b, SB_HEADS // SB_HEADS_PER_STEP),
        in_specs=[blk, blk, blk, blk, pl.BlockSpec((SB_BLOCK, SB_BLOCK), lambda bi, g: (0, 0))],
        out_specs=blk,
        out_shape=jax.ShapeDtypeStruct(q.shape, BF16),
        compiler_params=_params(2),
        name="sb_prompt",
    )(q, k, v, sz, tri)


def _sb_decode_kernel(q_ref, kn_ref, vn_ref, sz_ref, tri_ref, kc_ref, vc_ref, o_ref):
    nh = SB_HEADS
    past = kc_ref.shape[0] // nh
    tq = q_ref.shape[1]
    tri = tri_ref[...]
    row = lax.broadcasted_iota(jnp.int32, (tq, LANES), 0)
    col = lax.broadcasted_iota(jnp.int32, (tq, LANES), 1)
    pad = jnp.zeros((LANES - tq, SB_HEAD_DIM), BF16)
    for h in range(nh):
        cols = _head_cols(h)
        q = q_ref[0, :, cols]
        kn = jnp.concatenate([kn_ref[0, :, cols], pad], axis=0)
        vn = jnp.concatenate([vn_ref[0, :, cols], pad], axis=0)
        ps, carry = _sb_chunk(_qk(q, kn), col < row, jnp.zeros((tq, 1), F32), tri[:LANES, :LANES])
        acc = _pv(ps, vn)
        head_rows = pl.ds(h, past, stride=nh)
        ps, _ = _sb_chunk(_qk(q, kc_ref[head_rows, :].astype(BF16)), None, carry, tri)
        acc = acc + _pv(ps, vc_ref[head_rows, :].astype(BF16))
        o_ref[0, :, cols] = (acc * sz_ref[0, :, cols].astype(F32)).astype(o_ref.dtype)


def _sb_decode(q, k_new, v_new, k_cache, v_cache, sz, tri):
    b, t, w = q.shape
    _, p, nh, dh = k_cache.shape
    assert nh == SB_HEADS and dh == SB_HEAD_DIM
    blk = pl.BlockSpec((1, t, w), lambda bi: (bi, 0, 0))
    blk_cache = pl.BlockSpec((None, p * nh, dh), lambda bi: (bi, 0, 0))
    return pl.pallas_call(
        _sb_decode_kernel,
        grid=(b,),
        in_specs=[blk, blk, blk, blk, pl.BlockSpec((SB_BLOCK, SB_BLOCK), lambda bi: (0, 0)),
                  blk_cache, blk_cache],
        out_specs=blk,
        out_shape=jax.ShapeDtypeStruct(q.shape, BF16),
        compiler_params=_params(1),
        name="sb_decode",
    )(q, k_new, v_new, sz, tri, k_cache.reshape(b, p * nh, dh), v_cache.reshape(b, p * nh, dh))


def _chunk_causal(n):
    pos = jnp.arange(n)
    return (pos[None, :] // CHUNK) <= (pos[:, None] // CHUNK)


def _mm_tn(k, w_itemsize=4, out_bytes=8):
    for tn in (1024, 512, 256):
        if 2 * (MM_TM * k * 2 + k * tn * w_itemsize + MM_TM * tn * out_bytes) <= MM_VMEM_BUDGET:
            return tn
    raise ValueError("matmul blocks do not fit VMEM")


def _gm_mix_params(seq_len, w_s, b_s):
    blk = min(seq_len, GM_BLOCK)
    w = w_s[:, :blk, :blk] * _chunk_causal(blk).astype(w_s.dtype)
    reps = GM_BLOCK // blk
    wm = jnp.einsum("ab,gts->gatbs", jnp.eye(reps, dtype=w.dtype), w).reshape(
        GM_GROUPS, GM_BLOCK, GM_BLOCK).astype(BF16)
    return wm, jnp.tile(b_s[:, :blk].T, (reps, 1))


def _gm_layer(xs, seq_lens, norm_g, w_in, ln_g, ln_b, w_s, b_s, w_out):
    d = xs[0].shape[1]
    hs = [_rmsnorm(x, norm_g, BF16, min(x.shape[0], 1024)) for x in xs]
    tn = _mm_tn(d, out_bytes=2)
    ugs = _matmul(hs, w_in, BF16, tn=tn, n=2 * GM_WIDTH, act="gelu", name="gm_in_uv")
    szs = _matmul(hs, w_in, BF16, tn=tn, col0=2 * GM_WIDTH, n=GM_WIDTH, act="silu", name="gm_in_z")
    ys, v_rows = [], None
    for g, (ug, sz, seq_len) in enumerate(zip(ugs, szs, seq_lens)):
        emit_vn = g == len(xs) - 1
        outs = _gm_mix(ug, sz, *_gm_mix_params(seq_len, w_s, b_s), ln_g, ln_b,
                       tm=min(ug.shape[0], GM_MIX_TM), emit_vn=emit_vn)
        ys.append(outs[0])
        v_rows = outs[1] if emit_vn else v_rows
    x_new = _matmul(ys, w_out, F32, tn=_mm_tn(GM_WIDTH, w_out.dtype.itemsize), res=xs, name="gm_out")
    return x_new, v_rows


def _sb_in(xs, norm_g, w_in):
    d = xs[0].shape[1]
    hs = [_rmsnorm(x, norm_g, BF16, min(x.shape[0], 1024)) for x in xs]
    tn = _mm_tn(d, out_bytes=2)
    w = SB_WIDTH
    q = _matmul(hs, w_in, BF16, tn=tn, n=w, scale=SB_HEAD_DIM ** -0.5 * LOG2_E, name="sb_in_q")
    k, kb = _matmul_heads(hs, w_in, col0=w, name="sb_in_k")
    v, vb = _matmul_heads(hs, w_in, col0=2 * w, name="sb_in_v")
    sz = _matmul(hs, w_in, BF16, tn=tn, col0=3 * w, n=w, act="silu", name="sb_in_z")
    return q, k, v, kb, vb, sz


def kernel(x_prompt, x_sample, cache_sb_k, cache_sb_v, norm_g, final_norm_g, gm_w_in, gm_ln_g, gm_ln_b,
           gm_w_s, gm_b_s, gm_w_out, sb_w_in, sb_w_out):
    bsz, seq, d = x_prompt.shape
    dbsz, dseq, _ = x_sample.shape
    xs = [x_prompt.reshape(bsz * seq, d), x_sample.reshape(dbsz * dseq, d)]

    tt = (jnp.arange(SB_BLOCK)[:, None] >= jnp.arange(SB_BLOCK)[None, :]).astype(BF16)

    gm_v_rows, kp_rows, vp_rows, ks_rows, vs_rows = [], [], [], [], []
    for i in range(DEPTH):
        j = i // N_MIXERS
        if i % N_MIXERS == 0:
            xs, v_new = _gm_layer(xs, (seq, dseq), norm_g[i], gm_w_in[j], gm_ln_g[j], gm_ln_b[j], gm_w_s[j],
                                  gm_b_s[j], gm_w_out[j].astype(BF16))
            gm_v_rows.append(v_new.reshape(dbsz, dseq, GM_WIDTH))
        else:
            q, k, v, kb, vb, sz = _sb_in(xs, norm_g[i], sb_w_in[j])
            shp_p, shp_s = (bsz, seq, SB_WIDTH), (dbsz, dseq, SB_WIDTH)
            yp = _sb_prompt(q[0].reshape(shp_p), kb[0].reshape(shp_p), vb[0].reshape(shp_p),
                            sz[0].reshape(shp_p), tt)
            ys = _sb_decode(q[1].reshape(shp_s), kb[1].reshape(shp_s), vb[1].reshape(shp_s), cache_sb_k[j],
                            cache_sb_v[j], sz[1].reshape(shp_s), tt)
            y = [yp.reshape(bsz * seq, SB_WIDTH), ys.reshape(dbsz * dseq, SB_WIDTH)]
            if i == DEPTH - 1:
                xs = _matmul_res_norm(y, sb_w_out[j].astype(BF16), xs, final_norm_g, tm=512, name="sb_out_norm")
            else:
                xs = _matmul(y, sb_w_out[j], F32, tn=_mm_tn(SB_WIDTH), res=xs, name="sb_out")
            kp_rows.append(k[0].reshape(bsz, seq, SB_HEADS, SB_HEAD_DIM))
            vp_rows.append(v[0].reshape(bsz, seq, SB_HEADS, SB_HEAD_DIM))
            ks_rows.append(k[1].reshape(dbsz, dseq, SB_HEADS, SB_HEAD_DIM))
            vs_rows.append(v[1].reshape(dbsz, dseq, SB_HEADS, SB_HEAD_DIM))

    if DEPTH % N_MIXERS != 0:
        xs = [_rmsnorm(x, final_norm_g, F32, min(x.shape[0], 512)) for x in xs]
    y_prompt, y_sample = xs[0].reshape(bsz, seq, d), xs[1].reshape(dbsz, dseq, d)
    return (y_prompt, y_sample, jnp.stack(kp_rows), jnp.stack(vp_rows), jnp.stack(ks_rows),
            jnp.stack(vs_rows), jnp.stack(gm_v_rows))
```

```python
import functools

import jax
import jax.numpy as jnp
from jax import lax
from jax.experimental import pallas as pl
from jax.experimental.pallas import tpu as pltpu

D_MODEL = 2048
DEPTH = 2
CHUNK = 64
N_MIXERS = 2
GM_WIDTH = 2 * D_MODEL
GM_BLOCK = 128
GM_GROUPS = 16
GM_GROUP_DIM = GM_WIDTH // GM_GROUPS
SB_HEADS = 16
SB_HEAD_DIM = D_MODEL // SB_HEADS
SB_WIDTH = SB_HEADS * SB_HEAD_DIM
NORM_EPS = 1e-6
LN_EPS = 1e-5

LANES = 128
VMEM_LIMIT_BYTES = 56 * 1024 * 1024
MM_VMEM_BUDGET = 40 * 1024 * 1024
F32 = jnp.float32
BF16 = jnp.bfloat16


def _params(n_grid_dims):
    return pltpu.CompilerParams(
        dimension_semantics=("arbitrary",) * n_grid_dims,
        vmem_limit_bytes=VMEM_LIMIT_BYTES,
    )


def _gelu(x):
    c = 0.7978845608028654
    h = 0.5 * x
    return h + h * jnp.tanh(x * (c + (c * 0.044715) * (x * x)))


def _silu(x):
    h = 0.5 * x
    return h + h * jnp.tanh(h)


def _rmsnorm_kernel(x_ref, g_ref, o_ref):
    x = x_ref[...]
    ms = jnp.mean(x * x, axis=-1, keepdims=True)
    o_ref[...] = (x * lax.rsqrt(ms + NORM_EPS) * g_ref[...]).astype(o_ref.dtype)


def _rmsnorm(x, g, out_dtype, tm):
    m, d = x.shape
    return pl.pallas_call(
        _rmsnorm_kernel,
        grid=(m // tm,),
        in_specs=[pl.BlockSpec((tm, d), lambda i: (i, 0)),
                  pl.BlockSpec((1, d), lambda i: (0, 0))],
        out_specs=pl.BlockSpec((tm, d), lambda i: (i, 0)),
        out_shape=jax.ShapeDtypeStruct((m, d), out_dtype),
        compiler_params=_params(1),
        name="rmsnorm",
    )(x, g.reshape(1, d))


MM_TM = 1024
GM_MIX_TM = 512
HEAD_TILE = 8


def _row_groups(ms, tm_max):
    tms = [min(m, tm_max) for m in ms]
    counts = [m // tm for m, tm in zip(ms, tms)]
    assert all(m == tm * c for m, tm, c in zip(ms, tms, counts))
    starts = [sum(counts[a + 1:]) for a in range(len(ms))]
    return tms, counts, starts


def _row_index(i, start, count):
    return jnp.clip(i - start, 0, count - 1)


def _on_group(i, a, counts, starts, fn):
    if len(counts) == 1:
        fn()
    else:
        pl.when(jnp.logical_and(i >= starts[a], i < starts[a] + counts[a]))(fn)


def _mm_kernel(*refs, groups, act, scale, has_res):
    n_arr = len(groups[0])
    x_refs, w_ref = refs[:n_arr], refs[n_arr]
    r_refs = refs[n_arr + 1:2 * n_arr + 1] if has_res else None
    o_refs = refs[-n_arr:]
    i = pl.program_id(1)

    def tile(a):
        acc = jnp.dot(x_refs[a][...], w_ref[...].astype(BF16), preferred_element_type=F32)
        if act == "gelu":
            acc = _gelu(acc)
        elif act == "silu":
            acc = _silu(acc)
        if scale is not None:
            acc = acc * scale
        if has_res:
            acc = acc + r_refs[a][...]
        o_refs[a][...] = acc.astype(o_refs[a].dtype)

    for a in range(n_arr):
        _on_group(i, a, *groups, functools.partial(tile, a))


def _matmul(xs, w, out_dtype, *, tn, col0=0, n=None, act=None, scale=None, res=None, name="matmul"):
    k = xs[0].shape[1]
    n = w.shape[1] if n is None else n
    assert col0 % tn == 0 and n % tn == 0
    jb = col0 // tn
    tms, counts, starts = _row_groups([x.shape[0] for x in xs], MM_TM)
    in_specs = [pl.BlockSpec((tm, k), lambda j, i, s=s, c=c: (_row_index(i, s, c), 0))
                for tm, c, s in zip(tms, counts, starts)]
    in_specs.append(pl.BlockSpec((k, tn), lambda j, i: (0, j + jb)))
    row_col = [pl.BlockSpec((tm, tn), lambda j, i, s=s, c=c: (_row_index(i, s, c), j))
               for tm, c, s in zip(tms, counts, starts)]
    args = [*xs, w]
    if res is not None:
        in_specs += row_col
        args += list(res)
    return pl.pallas_call(
        functools.partial(_mm_kernel, groups=(counts, starts), act=act, scale=scale, has_res=res is not None),
        grid=(n // tn, sum(counts)),
        in_specs=in_specs,
        out_specs=row_col,
        out_shape=[jax.ShapeDtypeStruct((x.shape[0], n), out_dtype) for x in xs],
        compiler_params=_params(2),
        name=name,
    )(*args)


def _mm_heads_kernel(*refs, groups):
    n_arr = len(groups[0])
    x_refs, w_ref = refs[:n_arr], refs[n_arr]
    of_refs, ob_refs = refs[n_arr + 1:2 * n_arr + 1], refs[2 * n_arr + 1:]
    i = pl.program_id(1)

    def tile(a):
        tm = x_refs[a].shape[0]
        acc = jnp.dot(x_refs[a][...], w_ref[...].astype(BF16), preferred_element_type=F32)
        ob_refs[a][...] = acc.astype(ob_refs[a].dtype)
        rows = of_refs[a].reshape(tm * HEAD_TILE, SB_HEAD_DIM)
        for h in range(HEAD_TILE):
            rows[pl.ds(h, tm, stride=HEAD_TILE), :] = acc[:, _head_cols(h)]

    for a in range(n_arr):
        _on_group(i, a, *groups, functools.partial(tile, a))


def _matmul_heads(xs, w, *, col0, name):
    k = xs[0].shape[1]
    tn = HEAD_TILE * SB_HEAD_DIM
    assert col0 % tn == 0
    jb = col0 // tn
    tms, counts, starts = _row_groups([x.shape[0] for x in xs], MM_TM)
    groups = list(zip(tms, counts, starts))
    in_specs = [pl.BlockSpec((tm, k), lambda j, i, s=s, c=c: (_row_index(i, s, c), 0)) for tm, c, s in groups]
    in_specs.append(pl.BlockSpec((k, tn), lambda j, i: (0, j + jb)))
    out_specs = [pl.BlockSpec((tm, HEAD_TILE, SB_HEAD_DIM), lambda j, i, s=s, c=c: (_row_index(i, s, c), j, 0))
                 for tm, c, s in groups]
    out_specs += [pl.BlockSpec((tm, tn), lambda j, i, s=s, c=c: (_row_index(i, s, c), j)) for tm, c, s in groups]
    outs = pl.pallas_call(
        functools.partial(_mm_heads_kernel, groups=(counts, starts)),
        grid=(SB_WIDTH // tn, sum(counts)),
        in_specs=in_specs,
        out_specs=out_specs,
        out_shape=[jax.ShapeDtypeStruct((x.shape[0], SB_HEADS, SB_HEAD_DIM), F32) for x in xs]
        + [jax.ShapeDtypeStruct((x.shape[0], SB_WIDTH), BF16) for x in xs],
        compiler_params=_params(2),
        name=name,
    )(*xs, w)
    return outs[:len(xs)], outs[len(xs):]


def _mm_res_norm_kernel(*refs, groups):
    n_arr = len(groups[0])
    x_refs, w_ref = refs[:n_arr], refs[n_arr]
    r_refs, g_ref, o_refs = refs[n_arr + 1:2 * n_arr + 1], refs[2 * n_arr + 1], refs[2 * n_arr + 2:]
    i = pl.program_id(0)

    def tile(a):
        acc = jnp.dot(x_refs[a][...], w_ref[...], preferred_element_type=F32) + r_refs[a][...]
        ms = jnp.mean(acc * acc, axis=-1, keepdims=True)
        o_refs[a][...] = acc * lax.rsqrt(ms + NORM_EPS) * g_ref[...]

    for a in range(n_arr):
        _on_group(i, a, *groups, functools.partial(tile, a))


def _matmul_res_norm(xs, w, res, g, *, tm, name):
    k, n = w.shape
    tms, counts, starts = _row_groups([x.shape[0] for x in xs], tm)
    row = lambda cols: [pl.BlockSpec((t, cols), lambda i, s=s, c=c: (_row_index(i, s, c), 0))
                        for t, c, s in zip(tms, counts, starts)]
    return pl.pallas_call(
        functools.partial(_mm_res_norm_kernel, groups=(counts, starts)),
        grid=(sum(counts),),
        in_specs=[*row(k), pl.BlockSpec((k, n), lambda i: (0, 0)), *row(n),
                  pl.BlockSpec((1, n), lambda i: (0, 0))],
        out_specs=row(n),
        out_shape=[jax.ShapeDtypeStruct((x.shape[0], n), F32) for x in xs],
        compiler_params=_params(1),
        name=name,
    )(*xs, w, *res, g.reshape(1, n))


def _gm_mix_kernel(u_ref, gv_ref, sz_ref, wm_ref, bt_ref, lg_ref, lb_ref, *out_refs, emit_vn):
    if emit_vn:
        y_ref, vn_ref = out_refs
    else:
        (y_ref,) = out_refs
    tm = u_ref.shape[0]
    inv_w = 1.0 / GM_WIDTH
    s1 = jnp.zeros((tm, 1), F32)
    for g in range(GM_GROUPS):
        cols = slice(g * GM_GROUP_DIM, (g + 1) * GM_GROUP_DIM)
        s1 = s1 + jnp.sum(gv_ref[:, cols].astype(F32), axis=-1, keepdims=True)
    mu = s1 * inv_w
    s2 = jnp.zeros((tm, 1), F32)
    for g in range(GM_GROUPS):
        cols = slice(g * GM_GROUP_DIM, (g + 1) * GM_GROUP_DIM)
        xc = gv_ref[:, cols].astype(F32) - mu
        s2 = s2 + jnp.sum(xc * xc, axis=-1, keepdims=True)
    rstd = lax.rsqrt(s2 * inv_w + LN_EPS)
    for g in range(GM_GROUPS):
        cols = slice(g * GM_GROUP_DIM, (g + 1) * GM_GROUP_DIM)
        vn = (gv_ref[:, cols].astype(F32) - mu) * rstd * lg_ref[:, cols] + lb_ref[:, cols]
        if emit_vn:
            vn_ref[:, cols] = vn
        vnb = vn.astype(BF16)
        wg = wm_ref[g]
        bias = bt_ref[:, g:g + 1]
        for r in range(tm // GM_BLOCK):
            rows = slice(r * GM_BLOCK, (r + 1) * GM_BLOCK)
            mixed = jnp.dot(wg, vnb[rows], preferred_element_type=F32) + bias
            y_ref[rows, cols] = (u_ref[rows, cols] * mixed.astype(BF16)) * sz_ref[rows, cols]


def _gm_mix(ug, sz, wm, bt, ln_g, ln_b, *, tm, emit_vn):
    m = sz.shape[0]
    row_blk = lambda c: pl.BlockSpec((tm, GM_WIDTH), lambda i, c=c: (i, c))
    full = lambda shape: pl.BlockSpec(shape, lambda i: (0,) * len(shape))
    out_shape = [jax.ShapeDtypeStruct((m, GM_WIDTH), BF16)]
    out_specs = [row_blk(0)]
    if emit_vn:
        out_shape.append(jax.ShapeDtypeStruct((m, GM_WIDTH), F32))
        out_specs.append(row_blk(0))
    return pl.pallas_call(
        functools.partial(_gm_mix_kernel, emit_vn=emit_vn),
        grid=(m // tm,),
        in_specs=[row_blk(0), row_blk(1), row_blk(0),
                  full((GM_GROUPS, GM_BLOCK, GM_BLOCK)), full((GM_BLOCK, GM_GROUPS)),
                  full((1, GM_WIDTH)), full((1, GM_WIDTH))],
        out_specs=out_specs,
        out_shape=out_shape,
        compiler_params=_params(1),
        name="gm_mix",
    )(ug, ug, sz, wm, bt, ln_g.reshape(1, GM_WIDTH), ln_b.reshape(1, GM_WIDTH))


SB_BLOCK = 256
SB_TQ = 256
SB_HEADS_PER_STEP = 8
LOG2_E = 1.4426950408889634
SB_DEAD_LOG = -151.5


def _sb_chunk(s, strict, carry, tri):
    r = s.shape[0]
    bw = tri.shape[0]
    nb = s.shape[1] // bw
    sbs = [s[:, b * bw:(b + 1) * bw] for b in range(nb)]
    parts = []
    for b in range(nb):
        neg = -sbs[b]
        lf = jnp.minimum(neg, 0.0) - jnp.log2(1.0 + jnp.exp2(jnp.minimum(sbs[b], neg)))
        if strict is not None and b == nb - 1:
            lf = jnp.where(strict, lf, 0.0)
        parts.append(lf.astype(BF16))
    c_all = jnp.dot(jnp.concatenate(parts, axis=0), tri, preferred_element_type=F32)
    ps = [None] * nb
    for b in reversed(range(nb)):
        c = c_all[b * r:(b + 1) * r]
        p = jnp.exp2(sbs[b] + (c + carry))
        if strict is not None and b == nb - 1:
            p = jnp.where(strict, p, 0.0)
        ps[b] = p.astype(BF16)
        carry = carry + c[:, 0:1]
    return ps, carry


def _qk(q, k):
    return lax.dot_general(q, k, (((1,), (1,)), ((), ())), preferred_element_type=F32)


def _pv(ps, v):
    bw = ps[0].shape[1]
    acc = None
    for b, p in enumerate(ps):
        d = jnp.dot(p, v[b * bw:(b + 1) * bw], preferred_element_type=F32)
        acc = d if acc is None else acc + d
    return acc


def _head_cols(h):
    return slice(h * SB_HEAD_DIM, (h + 1) * SB_HEAD_DIM)


def _sb_prompt_kernel(q_ref, k_ref, v_ref, sz_ref, tri_ref, o_ref):
    tq = SB_TQ
    heads = range(SB_HEADS_PER_STEP)
    kb_ref, vb_ref = k_ref.at[0], v_ref.at[0]
    kmax = []
    for h in heads:
        ka = jnp.abs(kb_ref[:, _head_cols(h)].astype(F32))
        kmax.append(jnp.max(jnp.max(ka, axis=0, keepdims=True), axis=1, keepdims=True))

    tri = tri_ref[...]
    row = lax.broadcasted_iota(jnp.int32, (tq, SB_BLOCK), 0)
    col = lax.broadcasted_iota(jnp.int32, (tq, SB_BLOCK), 1)
    strict = col < row
    zero_carry = jnp.zeros((tq, 1), F32)

    def chunk(h, q, k0, width, mask, carry):
        s = _qk(q, kb_ref[pl.ds(k0, width), _head_cols(h)])
        ps, carry = _sb_chunk(s, mask, carry, tri)
        return _pv(ps, vb_ref[pl.ds(k0, width), _head_cols(h)]), carry

    def store(h, rows, acc):
        o_ref[0, rows, _head_cols(h)] = (acc * sz_ref[0, rows, _head_cols(h)].astype(F32)).astype(o_ref.dtype)

    for h in heads:
        acc, _ = chunk(h, q_ref[0, 0:tq, _head_cols(h)], 0, tq, strict, zero_carry)
        store(h, slice(0, tq), acc)

    def q_block(qi, _):
        rows = pl.ds(pl.multiple_of(qi * tq, tq), tq)
        qs = [q_ref[0, rows, _head_cols(h)] for h in heads]
        bounds = []
        for h in heads:
            q_l1 = jnp.sum(jnp.abs(qs[h].astype(F32)), axis=-1, keepdims=True)
            bounds.append(q_l1 * kmax[h] * 1.001)

        def alive(carries):
            worst = carries[0] + bounds[0]
            for h in heads[1:]:
                worst = jnp.maximum(worst, carries[h] + bounds[h])
            return (jnp.max(worst) > SB_DEAD_LOG).astype(jnp.int32)

        k0 = pl.multiple_of((qi - 1) * SB_BLOCK, SB_BLOCK)
        first = [chunk(h, qs[h], k0, SB_BLOCK + tq, strict, zero_carry) for h in heads]
        accs = tuple(f[0] for f in first)
        carries = tuple(f[1] for f in first)

        def cond(st):
            return jnp.logical_and(st[0] >= 0, st[1] > 0)

        def body(st):
            kb_idx, _, carries, accs = st
            k0 = pl.multiple_of(kb_idx * SB_BLOCK, SB_BLOCK)
            nxt = [chunk(h, qs[h], k0, SB_BLOCK, None, carries[h]) for h in heads]
            carries = tuple(n[1] for n in nxt)
            accs = tuple(a + n[0] for a, n in zip(accs, nxt))
            return kb_idx - 1, alive(carries), carries, accs

        st = lax.while_loop(cond, body, (qi - 2, alive(carries), carries, accs))
        for h in heads:
            store(h, rows, st[3][h])
        return 0

    lax.fori_loop(1, q_ref.shape[1] // tq, q_block, 0)


def _sb_prompt(q, k, v, sz, tri):
    b, l, _ = q.shape
    assert SB_TQ == SB_BLOCK and l % SB_TQ == 0
    gw = SB_HEADS_PER_STEP * SB_HEAD_DIM
    blk = pl.BlockSpec((1, l, gw), lambda bi, g: (bi, 0, g))
    return pl.pallas_call(
        _sb_prompt_kernel,
        grid=(b, SB_HEADS // SB_HEADS_PER_STEP),
        in_specs=[blk, blk, blk, blk, pl.BlockSpec((SB_BLOCK, SB_BLOCK), lambda bi, g: (0, 0))],
        out_specs=blk,
        out_shape=jax.ShapeDtypeStruct(q.shape, BF16),
        compiler_params=_params(2),
        name="sb_prompt",
    )(q, k, v, sz, tri)


def _sb_decode_kernel(q_ref, kn_ref, vn_ref, sz_ref, tri_ref, kc_ref, vc_ref, o_ref):
    nh = SB_HEADS
    past = kc_ref.shape[0] // nh
    tq = q_ref.shape[1]
    tri = tri_ref[...]
    row = lax.broadcasted_iota(jnp.int32, (tq, LANES), 0)
    col = lax.broadcasted_iota(jnp.int32, (tq, LANES), 1)
    pad = jnp.zeros((LANES - tq, SB_HEAD_DIM), BF16)
    for h in range(nh):
        cols = _head_cols(h)
        q = q_ref[0, :, cols]
        kn = jnp.concatenate([kn_ref[0, :, cols], pad], axis=0)
        vn = jnp.concatenate([vn_ref[0, :, cols], pad], axis=0)
        ps, carry = _sb_chunk(_qk(q, kn), col < row, jnp.zeros((tq, 1), F32), tri[:LANES, :LANES])
        acc = _pv(ps, vn)
        head_rows = pl.ds(h, past, stride=nh)
        s_t = _qk(kc_ref[head_rows, :].astype(BF16), jnp.concatenate([q, pad], axis=0))
        ps, _ = _sb_chunk(s_t.T[:tq], None, carry, tri)
        acc = acc + _pv(ps, vc_ref[head_rows, :].astype(BF16))
        o_ref[0, :, cols] = (acc * sz_ref[0, :, cols].astype(F32)).astype(o_ref.dtype)


def _sb_decode(q, k_new, v_new, k_cache, v_cache, sz, tri):
    b, t, w = q.shape
    _, p, nh, dh = k_cache.shape
    assert nh == SB_HEADS and dh == SB_HEAD_DIM
    blk = pl.BlockSpec((1, t, w), lambda bi: (bi, 0, 0))
    blk_cache = pl.BlockSpec((None, p * nh, dh), lambda bi: (bi, 0, 0))
    return pl.pallas_call(
        _sb_decode_kernel,
        grid=(b,),
        in_specs=[blk, blk, blk, blk, pl.BlockSpec((SB_BLOCK, SB_BLOCK), lambda bi: (0, 0)),
                  blk_cache, blk_cache],
        out_specs=blk,
        out_shape=jax.ShapeDtypeStruct(q.shape, BF16),
        compiler_params=_params(1),
        name="sb_decode",
    )(q, k_new, v_new, sz, tri, k_cache.reshape(b, p * nh, dh), v_cache.reshape(b, p * nh, dh))


def _chunk_causal(n):
    pos = jnp.arange(n)
    return (pos[None, :] // CHUNK) <= (pos[:, None] // CHUNK)


def _mm_tn(k, w_itemsize=4, out_bytes=8):
    for tn in (1024, 512, 256):
        if 2 * (MM_TM * k * 2 + k * tn * w_itemsize + MM_TM * tn * out_bytes) <= MM_VMEM_BUDGET:
            return tn
    raise ValueError("matmul blocks do not fit VMEM")


def _gm_mix_params(seq_len, w_s, b_s):
    blk = min(seq_len, GM_BLOCK)
    w = w_s[:, :blk, :blk] * _chunk_causal(blk).astype(w_s.dtype)
    reps = GM_BLOCK // blk
    wm = jnp.einsum("ab,gts->gatbs", jnp.eye(reps, dtype=w.dtype), w).reshape(
        GM_GROUPS, GM_BLOCK, GM_BLOCK).astype(BF16)
    return wm, jnp.tile(b_s[:, :blk].T, (reps, 1))


def _gm_layer(xs, seq_lens, norm_g, w_in, ln_g, ln_b, w_s, b_s, w_out):
    d = xs[0].shape[1]
    hs = [_rmsnorm(x, norm_g, BF16, min(x.shape[0], 1024)) for x in xs]
    tn = _mm_tn(d, out_bytes=2)
    ugs = _matmul(hs, w_in, BF16, tn=tn, n=2 * GM_WIDTH, act="gelu", name="gm_in_uv")
    szs = _matmul(hs, w_in, BF16, tn=tn, col0=2 * GM_WIDTH, n=GM_WIDTH, act="silu", name="gm_in_z")
    ys, v_rows = [], None
    for g, (ug, sz, seq_len) in enumerate(zip(ugs, szs, seq_lens)):
        emit_vn = g == len(xs) - 1
        outs = _gm_mix(ug, sz, *_gm_mix_params(seq_len, w_s, b_s), ln_g, ln_b,
                       tm=min(ug.shape[0], GM_MIX_TM), emit_vn=emit_vn)
        ys.append(outs[0])
        v_rows = outs[1] if emit_vn else v_rows
    x_new = _matmul(ys, w_out, F32, tn=_mm_tn(GM_WIDTH, w_out.dtype.itemsize), res=xs, name="gm_out")
    return x_new, v_rows


def _sb_in(xs, norm_g, w_in):
    d = xs[0].shape[1]
    hs = [_rmsnorm(x, norm_g, BF16, min(x.shape[0], 1024)) for x in xs]
    tn = _mm_tn(d, out_bytes=2)
    w = SB_WIDTH
    q = _matmul(hs, w_in, BF16, tn=tn, n=w, scale=SB_HEAD_DIM ** -0.5 * LOG2_E, name="sb_in_q")
    k, kb = _matmul_heads(hs, w_in, col0=w, name="sb_in_k")
    v, vb = _matmul_heads(hs, w_in, col0=2 * w, name="sb_in_v")
    sz = _matmul(hs, w_in, BF16, tn=tn, col0=3 * w, n=w, act="silu", name="sb_in_z")
    return q, k, v, kb, vb, sz


def kernel(x_prompt, x_sample, cache_sb_k, cache_sb_v, norm_g, final_norm_g, gm_w_in, gm_ln_g, gm_ln_b,
           gm_w_s, gm_b_s, gm_w_out, sb_w_in, sb_w_out):
    bsz, seq, d = x_prompt.shape
    dbsz, dseq, _ = x_sample.shape
    xs = [x_prompt.reshape(bsz * seq, d), x_sample.reshape(dbsz * dseq, d)]

    tt = (jnp.arange(SB_BLOCK)[:, None] >= jnp.arange(SB_BLOCK)[None, :]).astype(BF16)

    gm_v_rows, kp_rows, vp_rows, ks_rows, vs_rows = [], [], [], [], []
    for i in range(DEPTH):
        j = i // N_MIXERS
        if i % N_MIXERS == 0:
            xs, v_new = _gm_layer(xs, (seq, dseq), norm_g[i], gm_w_in[j], gm_ln_g[j], gm_ln_b[j], gm_w_s[j],
                                  gm_b_s[j], gm_w_out[j].astype(BF16))
            gm_v_rows.append(v_new.reshape(dbsz, dseq, GM_WIDTH))
        else:
            q, k, v, kb, vb, sz = _sb_in(xs, norm_g[i], sb_w_in[j])
            shp_p, shp_s = (bsz, seq, SB_WIDTH), (dbsz, dseq, SB_WIDTH)
            yp = _sb_prompt(q[0].reshape(shp_p), kb[0].reshape(shp_p), vb[0].reshape(shp_p),
                            sz[0].reshape(shp_p), tt)
            ys = _sb_decode(q[1].reshape(shp_s), kb[1].reshape(shp_s), vb[1].reshape(shp_s), cache_sb_k[j],
                            cache_sb_v[j], sz[1].reshape(shp_s), tt)
            y = [yp.reshape(bsz * seq, SB_WIDTH), ys.reshape(dbsz * dseq, SB_WIDTH)]
            if i == DEPTH - 1:
                xs = _matmul_res_norm(y, sb_w_out[j].astype(BF16), xs, final_norm_g, tm=512, name="sb_out_norm")
            else:
                xs = _matmul(y, sb_w_out[j], F32, tn=_mm_tn(SB_WIDTH), res=xs, name="sb_out")
            kp_rows.append(k[0].reshape(bsz, seq, SB_HEADS, SB_HEAD_DIM))
            vp_rows.append(v[0].reshape(bsz, seq, SB_HEADS, SB_HEAD_DIM))
            ks_rows.append(k[1].reshape(dbsz, dseq, SB_HEADS, SB_HEAD_DIM))
            vs_rows.append(v[1].reshape(dbsz, dseq, SB_HEADS, SB_HEAD_DIM))

    if DEPTH % N_MIXERS != 0:
        xs = [_rmsnorm(x, final_norm_g, F32, min(x.shape[0], 512)) for x in xs]
    y_prompt, y_sample = xs[0].reshape(bsz, seq, d), xs[1].reshape(dbsz, dseq, d)
    return (y_prompt, y_sample, jnp.stack(kp_rows), jnp.stack(vp_rows), jnp.stack(ks_rows),
            jnp.stack(vs_rows), jnp.stack(gm_v_rows))
```

```python
import functools

import jax
import jax.numpy as jnp
from jax import lax
from jax.experimental import pallas as pl
from jax.experimental.pallas import tpu as pltpu

D_MODEL = 2048
DEPTH = 2
CHUNK = 64
N_MIXERS = 2
GM_WIDTH = 2 * D_MODEL
GM_BLOCK = 128
GM_GROUPS = 16
GM_GROUP_DIM = GM_WIDTH // GM_GROUPS
SB_HEADS = 16
SB_HEAD_DIM = D_MODEL // SB_HEADS
SB_WIDTH = SB_HEADS * SB_HEAD_DIM
NORM_EPS = 1e-6
LN_EPS = 1e-5

LANES = 128
VMEM_LIMIT_BYTES = 56 * 1024 * 1024
MM_VMEM_BUDGET = 40 * 1024 * 1024
F32 = jnp.float32
BF16 = jnp.bfloat16


def _params(n_grid_dims):
    return pltpu.CompilerParams(
        dimension_semantics=("arbitrary",) * n_grid_dims,
        vmem_limit_bytes=VMEM_LIMIT_BYTES,
    )


def _gelu(x):
    c = 0.7978845608028654
    h = 0.5 * x
    return h + h * jnp.tanh(x * (c + (c * 0.044715) * (x * x)))


def _silu(x):
    h = 0.5 * x
    return h + h * jnp.tanh(h)


def _rmsnorm_kernel(x_ref, g_ref, o_ref):
    x = x_ref[...]
    ms = jnp.mean(x * x, axis=-1, keepdims=True)
    o_ref[...] = (x * lax.rsqrt(ms + NORM_EPS) * g_ref[...]).astype(o_ref.dtype)


def _rmsnorm(x, g, out_dtype, tm):
    m, d = x.shape
    return pl.pallas_call(
        _rmsnorm_kernel,
        grid=(m // tm,),
        in_specs=[pl.BlockSpec((tm, d), lambda i: (i, 0)),
                  pl.BlockSpec((1, d), lambda i: (0, 0))],
        out_specs=pl.BlockSpec((tm, d), lambda i: (i, 0)),
        out_shape=jax.ShapeDtypeStruct((m, d), out_dtype),
        compiler_params=_params(1),
        name="rmsnorm",
    )(x, g.reshape(1, d))


MM_TM = 1024
GM_MIX_TM = 512
HEAD_TILE = 8


def _row_groups(ms, tm_max):
    tms = [min(m, tm_max) for m in ms]
    counts = [m // tm for m, tm in zip(ms, tms)]
    assert all(m == tm * c for m, tm, c in zip(ms, tms, counts))
    starts = [sum(counts[a + 1:]) for a in range(len(ms))]
    return tms, counts, starts


def _row_index(i, start, count):
    return jnp.clip(i - start, 0, count - 1)


def _on_group(i, a, counts, starts, fn):
    if len(counts) == 1:
        fn()
    else:
        pl.when(jnp.logical_and(i >= starts[a], i < starts[a] + counts[a]))(fn)


def _mm_kernel(*refs, groups, act, scale, has_res):
    n_arr = len(groups[0])
    x_refs, w_ref = refs[:n_arr], refs[n_arr]
    r_refs = refs[n_arr + 1:2 * n_arr + 1] if has_res else None
    o_refs = refs[-n_arr:]
    i = pl.program_id(1)

    def tile(a):
        acc = jnp.dot(x_refs[a][...], w_ref[...].astype(BF16), preferred_element_type=F32)
        if act == "gelu":
            acc = _gelu(acc)
        elif act == "silu":
            acc = _silu(acc)
        if scale is not None:
            acc = acc * scale
        if has_res:
            acc = acc + r_refs[a][...]
        o_refs[a][...] = acc.astype(o_refs[a].dtype)

    for a in range(n_arr):
        _on_group(i, a, *groups, functools.partial(tile, a))


def _matmul(xs, w, out_dtype, *, tn, col0=0, n=None, act=None, scale=None, res=None, name="matmul"):
    k = xs[0].shape[1]
    n = w.shape[1] if n is None else n
    assert col0 % tn == 0 and n % tn == 0
    jb = col0 // tn
    tms, counts, starts = _row_groups([x.shape[0] for x in xs], MM_TM)
    in_specs = [pl.BlockSpec((tm, k), lambda j, i, s=s, c=c: (_row_index(i, s, c), 0))
                for tm, c, s in zip(tms, counts, starts)]
    in_specs.append(pl.BlockSpec((k, tn), lambda j, i: (0, j + jb)))
    row_col = [pl.BlockSpec((tm, tn), lambda j, i, s=s, c=c: (_row_index(i, s, c), j))
               for tm, c, s in zip(tms, counts, starts)]
    args = [*xs, w]
    if res is not None:
        in_specs += row_col
        args += list(res)
    return pl.pallas_call(
        functools.partial(_mm_kernel, groups=(counts, starts), act=act, scale=scale, has_res=res is not None),
        grid=(n // tn, sum(counts)),
        in_specs=in_specs,
        out_specs=row_col,
        out_shape=[jax.ShapeDtypeStruct((x.shape[0], n), out_dtype) for x in xs],
        compiler_params=_params(2),
        name=name,
    )(*args)


def _mm_heads_kernel(*refs, groups):
    n_arr = len(groups[0])
    x_refs, w_ref = refs[:n_arr], refs[n_arr]
    of_refs, ob_refs = refs[n_arr + 1:2 * n_arr + 1], refs[2 * n_arr + 1:]
    i = pl.program_id(1)

    def tile(a):
        tm = x_refs[a].shape[0]
        acc = jnp.dot(x_refs[a][...], w_ref[...].astype(BF16), preferred_element_type=F32)
        ob_refs[a][...] = acc.astype(ob_refs[a].dtype)
        rows = of_refs[a].reshape(tm * HEAD_TILE, SB_HEAD_DIM)
        for h in range(HEAD_TILE):
            rows[pl.ds(h, tm, stride=HEAD_TILE), :] = acc[:, _head_cols(h)]

    for a in range(n_arr):
        _on_group(i, a, *groups, functools.partial(tile, a))


def _matmul_heads(xs, w, *, col0, name):
    k = xs[0].shape[1]
    tn = HEAD_TILE * SB_HEAD_DIM
    assert col0 % tn == 0
    jb = col0 // tn
    tms, counts, starts = _row_groups([x.shape[0] for x in xs], MM_TM)
    groups = list(zip(tms, counts, starts))
    in_specs = [pl.BlockSpec((tm, k), lambda j, i, s=s, c=c: (_row_index(i, s, c), 0)) for tm, c, s in groups]
    in_specs.append(pl.BlockSpec((k, tn), lambda j, i: (0, j + jb)))
    out_specs = [pl.BlockSpec((tm, HEAD_TILE, SB_HEAD_DIM), lambda j, i, s=s, c=c: (_row_index(i, s, c), j, 0))
                 for tm, c, s in groups]
    out_specs += [pl.BlockSpec((tm, tn), lambda j, i, s=s, c=c: (_row_index(i, s, c), j)) for tm, c, s in groups]
    outs = pl.pallas_call(
        functools.partial(_mm_heads_kernel, groups=(counts, starts)),
        grid=(SB_WIDTH // tn, sum(counts)),
        in_specs=in_specs,
        out_specs=out_specs,
        out_shape=[jax.ShapeDtypeStruct((x.shape[0], SB_HEADS, SB_HEAD_DIM), F32) for x in xs]
        + [jax.ShapeDtypeStruct((x.shape[0], SB_WIDTH), BF16) for x in xs],
        compiler_params=_params(2),
        name=name,
    )(*xs, w)
    return outs[:len(xs)], outs[len(xs):]


def _mm_res_norm_kernel(*refs, groups):
    n_arr = len(groups[0])
    x_refs, w_ref = refs[:n_arr], refs[n_arr]
    r_refs, g_ref, o_refs = refs[n_arr + 1:2 * n_arr + 1], refs[2 * n_arr + 1], refs[2 * n_arr + 2:]
    i = pl.program_id(0)

    def tile(a):
        acc = jnp.dot(x_refs[a][...], w_ref[...], preferred_element_type=F32) + r_refs[a][...]
        ms = jnp.mean(acc * acc, axis=-1, keepdims=True)
        o_refs[a][...] = acc * lax.rsqrt(ms + NORM_EPS) * g_ref[...]

    for a in range(n_arr):
        _on_group(i, a, *groups, functools.partial(tile, a))


def _matmul_res_norm(xs, w, res, g, *, tm, name):
    k, n = w.shape
    tms, counts, starts = _row_groups([x.shape[0] for x in xs], tm)
    row = lambda cols: [pl.BlockSpec((t, cols), lambda i, s=s, c=c: (_row_index(i, s, c), 0))
                        for t, c, s in zip(tms, counts, starts)]
    return pl.pallas_call(
        functools.partial(_mm_res_norm_kernel, groups=(counts, starts)),
        grid=(sum(counts),),
        in_specs=[*row(k), pl.BlockSpec((k, n), lambda i: (0, 0)), *row(n),
                  pl.BlockSpec((1, n), lambda i: (0, 0))],
        out_specs=row(n),
        out_shape=[jax.ShapeDtypeStruct((x.shape[0], n), F32) for x in xs],
        compiler_params=_params(1),
        name=name,
    )(*xs, w, *res, g.reshape(1, n))


def _gm_mix_kernel(u_ref, gv_ref, sz_ref, wm_ref, bt_ref, lg_ref, lb_ref, *out_refs, emit_vn):
    if emit_vn:
        y_ref, vn_ref = out_refs
    else:
        (y_ref,) = out_refs
    tm = u_ref.shape[0]
    inv_w = 1.0 / GM_WIDTH
    s1 = jnp.zeros((tm, 1), F32)
    for g in range(GM_GROUPS):
        cols = slice(g * GM_GROUP_DIM, (g + 1) * GM_GROUP_DIM)
        s1 = s1 + jnp.sum(gv_ref[:, cols].astype(F32), axis=-1, keepdims=True)
    mu = s1 * inv_w
    s2 = jnp.zeros((tm, 1), F32)
    for g in range(GM_GROUPS):
        cols = slice(g * GM_GROUP_DIM, (g + 1) * GM_GROUP_DIM)
        xc = gv_ref[:, cols].astype(F32) - mu
        s2 = s2 + jnp.sum(xc * xc, axis=-1, keepdims=True)
    rstd = lax.rsqrt(s2 * inv_w + LN_EPS)
    for g in range(GM_GROUPS):
        cols = slice(g * GM_GROUP_DIM, (g + 1) * GM_GROUP_DIM)
        vn = (gv_ref[:, cols].astype(F32) - mu) * rstd * lg_ref[:, cols] + lb_ref[:, cols]
        if emit_vn:
            vn_ref[:, cols] = vn
        vnb = vn.astype(BF16)
        wg = wm_ref[g]
        bias = bt_ref[:, g:g + 1]
        for r in range(tm // GM_BLOCK):
            rows = slice(r * GM_BLOCK, (r + 1) * GM_BLOCK)
            mixed = jnp.dot(wg, vnb[rows], preferred_element_type=F32) + bias
            y_ref[rows, cols] = (u_ref[rows, cols] * mixed.astype(BF16)) * sz_ref[rows, cols]


def _gm_mix(ug, sz, wm, bt, ln_g, ln_b, *, tm, emit_vn):
    m = sz.shape[0]
    row_blk = lambda c: pl.BlockSpec((tm, GM_WIDTH), lambda i, c=c: (i, c))
    full = lambda shape: pl.BlockSpec(shape, lambda i: (0,) * len(shape))
    out_shape = [jax.ShapeDtypeStruct((m, GM_WIDTH), BF16)]
    out_specs = [row_blk(0)]
    if emit_vn:
        out_shape.append(jax.ShapeDtypeStruct((m, GM_WIDTH), F32))
        out_specs.append(row_blk(0))
    return pl.pallas_call(
        functools.partial(_gm_mix_kernel, emit_vn=emit_vn),
        grid=(m // tm,),
        in_specs=[row_blk(0), row_blk(1), row_blk(0),
                  full((GM_GROUPS, GM_BLOCK, GM_BLOCK)), full((GM_BLOCK, GM_GROUPS)),
                  full((1, GM_WIDTH)), full((1, GM_WIDTH))],
        out_specs=out_specs,
        out_shape=out_shape,
        compiler_params=_params(1),
        name="gm_mix",
    )(ug, ug, sz, wm, bt, ln_g.reshape(1, GM_WIDTH), ln_b.reshape(1, GM_WIDTH))


SB_BLOCK = 256
SB_TQ = 256
SB_HEADS_PER_STEP = 8
LOG2_E = 1.4426950408889634
SB_DEAD_LOG = -151.5


def _sb_chunk(s, strict, carry, tri):
    r = s.shape[0]
    bw = tri.shape[0]
    nb = s.shape[1] // bw
    sbs = [s[:, b * bw:(b + 1) * bw] for b in range(nb)]
    parts = []
    for b in range(nb):
        neg = -sbs[b]
        lf = jnp.minimum(neg, 0.0) - jnp.log2(1.0 + jnp.exp2(jnp.minimum(sbs[b], neg)))
        if strict is not None and b == nb - 1:
            lf = jnp.where(strict, lf, 0.0)
        parts.append(lf.astype(BF16))
    c_all = jnp.dot(jnp.concatenate(parts, axis=0), tri, preferred_element_type=F32)
    ps = [None] * nb
    for b in reversed(range(nb)):
        c = c_all[b * r:(b + 1) * r]
        p = jnp.exp2(sbs[b] + (c + carry))
        if strict is not None and b == nb - 1:
            p = jnp.where(strict, p, 0.0)
        ps[b] = p.astype(BF16)
        carry = carry + c[:, 0:1]
    return ps, carry


def _qk(q, k):
    return lax.dot_general(q, k, (((1,), (1,)), ((), ())), preferred_element_type=F32)


def _pv(ps, v):
    bw = ps[0].shape[1]
    acc = None
    for b, p in enumerate(ps):
        d = jnp.dot(p, v[b * bw:(b + 1) * bw], preferred_element_type=F32)
        acc = d if acc is None else acc + d
    return acc


def _head_cols(h):
    return slice(h * SB_HEAD_DIM, (h + 1) * SB_HEAD_DIM)


def _sb_prompt_kernel(q_ref, k_ref, v_ref, sz_ref, tri_ref, o_ref):
    tq = SB_TQ
    heads = range(SB_HEADS_PER_STEP)
    kb_ref, vb_ref = k_ref.at[0], v_ref.at[0]
    kmax = []
    for h in heads:
        ka = jnp.abs(kb_ref[:, _head_cols(h)].astype(F32))
        kmax.append(jnp.max(jnp.max(ka, axis=0, keepdims=True), axis=1, keepdims=True))

    tri = tri_ref[...]
    row = lax.broadcasted_iota(jnp.int32, (tq, SB_BLOCK), 0)
    col = lax.broadcasted_iota(jnp.int32, (tq, SB_BLOCK), 1)
    strict = col < row
    zero_carry = jnp.zeros((tq, 1), F32)

    def chunk(h, q, k0, width, mask, carry):
        s = _qk(q, kb_ref[pl.ds(k0, width), _head_cols(h)])
        ps, carry = _sb_chunk(s, mask, carry, tri)
        return _pv(ps, vb_ref[pl.ds(k0, width), _head_cols(h)]), carry

    def store(h, rows, acc):
        o_ref[0, rows, _head_cols(h)] = (acc * sz_ref[0, rows, _head_cols(h)].astype(F32)).astype(o_ref.dtype)

    for h in heads:
        acc, _ = chunk(h, q_ref[0, 0:tq, _head_cols(h)], 0, tq, strict, zero_carry)
        store(h, slice(0, tq), acc)

    def q_block(qi, _):
        rows = pl.ds(pl.multiple_of(qi * tq, tq), tq)
        qs = [q_ref[0, rows, _head_cols(h)] for h in heads]
        bounds = []
        for h in heads:
            q_l1 = jnp.sum(jnp.abs(qs[h].astype(F32)), axis=-1, keepdims=True)
            bounds.append(q_l1 * kmax[h] * 1.001)

        def alive(carries):
            worst = carries[0] + bounds[0]
            for h in heads[1:]:
                worst = jnp.maximum(worst, carries[h] + bounds[h])
            return (jnp.max(worst) > SB_DEAD_LOG).astype(jnp.int32)

        k0 = pl.multiple_of((qi - 1) * SB_BLOCK, SB_BLOCK)
        first = [chunk(h, qs[h], k0, SB_BLOCK + tq, strict, zero_carry) for h in heads]
        accs = tuple(f[0] for f in first)
        carries = tuple(f[1] for f in first)

        def cond(st):
            return jnp.logical_and(st[0] >= 0, st[1] > 0)

        def body(st):
            kb_idx, _, carries, accs = st
            k0 = pl.multiple_of(kb_idx * SB_BLOCK, SB_BLOCK)
            nxt = [chunk(h, qs[h], k0, SB_BLOCK, None, carries[h]) for h in heads]
            carries = tuple(n[1] for n in nxt)
            accs = tuple(a + n[0] for a, n in zip(accs, nxt))
            return kb_idx - 1, alive(carries), carries, accs

        st = lax.while_loop(cond, body, (qi - 2, alive(carries), carries, accs))
        for h in heads:
            store(h, rows, st[3][h])
        return 0

    lax.fori_loop(1, q_ref.shape[1] // tq, q_block, 0)


def _sb_prompt(q, k, v, sz, tri):
    b, l, _ = q.shape
    assert SB_TQ == SB_BLOCK and l % SB_TQ == 0
    gw = SB_HEADS_PER_STEP * SB_HEAD_DIM
    blk = pl.BlockSpec((1, l, gw), lambda bi, g: (bi, 0, g))
    return pl.pallas_call(
        _sb_prompt_kernel,
        grid=(b, SB_HEADS // SB_HEADS_PER_STEP),
        in_specs=[blk, blk, blk, blk, pl.BlockSpec((SB_BLOCK, SB_BLOCK), lambda bi, g: (0, 0))],
        out_specs=blk,
        out_shape=jax.ShapeDtypeStruct(q.shape, BF16),
        compiler_params=_params(2),
        name="sb_prompt",
    )(q, k, v, sz, tri)


def _sb_decode_kernel(q_ref, kn_ref, vn_ref, sz_ref, tri_ref, kc_ref, vc_ref, o_ref):
    past, nh, _ = kc_ref.shape
    kc_ref = kc_ref.reshape(past * nh, SB_HEAD_DIM)
    vc_ref = vc_ref.reshape(past * nh, SB_HEAD_DIM)
    tq = q_ref.shape[1]
    tri = tri_ref[...]
    row = lax.broadcasted_iota(jnp.int32, (tq, LANES), 0)
    col = lax.broadcasted_iota(jnp.int32, (tq, LANES), 1)
    pad = jnp.zeros((LANES - tq, SB_HEAD_DIM), BF16)
    for h in range(nh):
        cols = _head_cols(h)
        q = q_ref[0, :, cols]
        kn = jnp.concatenate([kn_ref[0, :, cols], pad], axis=0)
        vn = jnp.concatenate([vn_ref[0, :, cols], pad], axis=0)
        ps, carry = _sb_chunk(_qk(q, kn), col < row, jnp.zeros((tq, 1), F32), tri[:LANES, :LANES])
        acc = _pv(ps, vn)
        head_rows = pl.ds(h, past, stride=nh)
        s_t = _qk(kc_ref[head_rows, :].astype(BF16), jnp.concatenate([q, pad], axis=0))
        ps, _ = _sb_chunk(s_t.T[:tq], None, carry, tri)
        acc = acc + _pv(ps, vc_ref[head_rows, :].astype(BF16))
        o_ref[0, :, cols] = (acc * sz_ref[0, :, cols].astype(F32)).astype(o_ref.dtype)


def _sb_decode(q, k_new, v_new, k_cache, v_cache, sz, tri):
    b, t, _ = q.shape
    _, p, nh, dh = k_cache.shape
    assert nh == SB_HEADS and dh == SB_HEAD_DIM
    g = HEAD_TILE
    blk = pl.BlockSpec((1, t, g * dh), lambda bi, gi: (bi, 0, gi))
    blk_cache = pl.BlockSpec((None, p, g, dh), lambda bi, gi: (bi, 0, gi, 0))
    return pl.pallas_call(
        _sb_decode_kernel,
        grid=(b, nh // g),
        in_specs=[blk, blk, blk, blk, pl.BlockSpec((SB_BLOCK, SB_BLOCK), lambda bi, gi: (0, 0)),
                  blk_cache, blk_cache],
        out_specs=blk,
        out_shape=jax.ShapeDtypeStruct(q.shape, BF16),
        compiler_params=_params(2),
        name="sb_decode",
    )(q, k_new, v_new, sz, tri, k_cache, v_cache)


def _chunk_causal(n):
    pos = jnp.arange(n)
    return (pos[None, :] // CHUNK) <= (pos[:, None] // CHUNK)


def _mm_tn(k, w_itemsize=4, out_bytes=8):
    for tn in (1024, 512, 256):
        if 2 * (MM_TM * k * 2 + k * tn * w_itemsize + MM_TM * tn * out_bytes) <= MM_VMEM_BUDGET:
            return tn
    raise ValueError("matmul blocks do not fit VMEM")


def _gm_mix_params(seq_len, w_s, b_s):
    blk = min(seq_len, GM_BLOCK)
    w = w_s[:, :blk, :blk] * _chunk_causal(blk).astype(w_s.dtype)
    reps = GM_BLOCK // blk
    wm = jnp.einsum("ab,gts->gatbs", jnp.eye(reps, dtype=w.dtype), w).reshape(
        GM_GROUPS, GM_BLOCK, GM_BLOCK).astype(BF16)
    return wm, jnp.tile(b_s[:, :blk].T, (reps, 1))


def _gm_layer(xs, seq_lens, norm_g, w_in, ln_g, ln_b, w_s, b_s, w_out):
    d = xs[0].shape[1]
    hs = [_rmsnorm(x, norm_g, BF16, min(x.shape[0], 1024)) for x in xs]
    tn = _mm_tn(d, out_bytes=2)
    ugs = _matmul(hs, w_in, BF16, tn=tn, n=2 * GM_WIDTH, act="gelu", name="gm_in_uv")
    szs = _matmul(hs, w_in, BF16, tn=tn, col0=2 * GM_WIDTH, n=GM_WIDTH, act="silu", name="gm_in_z")
    ys, v_rows = [], None
    for g, (ug, sz, seq_len) in enumerate(zip(ugs, szs, seq_lens)):
        emit_vn = g == len(xs) - 1
        outs = _gm_mix(ug, sz, *_gm_mix_params(seq_len, w_s, b_s), ln_g, ln_b,
                       tm=min(ug.shape[0], GM_MIX_TM), emit_vn=emit_vn)
        ys.append(outs[0])
        v_rows = outs[1] if emit_vn else v_rows
    x_new = _matmul(ys, w_out, F32, tn=_mm_tn(GM_WIDTH, w_out.dtype.itemsize), res=xs, name="gm_out")
    return x_new, v_rows


def _sb_in(xs, norm_g, w_in):
    d = xs[0].shape[1]
    hs = [_rmsnorm(x, norm_g, BF16, min(x.shape[0], 1024)) for x in xs]
    tn = _mm_tn(d, out_bytes=2)
    w = SB_WIDTH
    q = _matmul(hs, w_in, BF16, tn=tn, n=w, scale=SB_HEAD_DIM ** -0.5 * LOG2_E, name="sb_in_q")
    k, kb = _matmul_heads(hs, w_in, col0=w, name="sb_in_k")
    v, vb = _matmul_heads(hs, w_in, col0=2 * w, name="sb_in_v")
    sz = _matmul(hs, w_in, BF16, tn=tn, col0=3 * w, n=w, act="silu", name="sb_in_z")
    return q, k, v, kb, vb, sz


def kernel(x_prompt, x_sample, cache_sb_k, cache_sb_v, norm_g, final_norm_g, gm_w_in, gm_ln_g, gm_ln_b,
           gm_w_s, gm_b_s, gm_w_out, sb_w_in, sb_w_out):
    bsz, seq, d = x_prompt.shape
    dbsz, dseq, _ = x_sample.shape
    xs = [x_prompt.reshape(bsz * seq, d), x_sample.reshape(dbsz * dseq, d)]

    tt = (jnp.arange(SB_BLOCK)[:, None] >= jnp.arange(SB_BLOCK)[None, :]).astype(BF16)

    gm_v_rows, kp_rows, vp_rows, ks_rows, vs_rows = [], [], [], [], []
    for i in range(DEPTH):
        j = i // N_MIXERS
        if i % N_MIXERS == 0:
            xs, v_new = _gm_layer(xs, (seq, dseq), norm_g[i], gm_w_in[j], gm_ln_g[j], gm_ln_b[j], gm_w_s[j],
                                  gm_b_s[j], gm_w_out[j].astype(BF16))
            gm_v_rows.append(v_new.reshape(dbsz, dseq, GM_WIDTH))
        else:
            q, k, v, kb, vb, sz = _sb_in(xs, norm_g[i], sb_w_in[j])
            shp_p, shp_s = (bsz, seq, SB_WIDTH), (dbsz, dseq, SB_WIDTH)
            yp = _sb_prompt(q[0].reshape(shp_p), kb[0].reshape(shp_p), vb[0].reshape(shp_p),
                            sz[0].reshape(shp_p), tt)
            ys = _sb_decode(q[1].reshape(shp_s), kb[1].reshape(shp_s), vb[1].reshape(shp_s), cache_sb_k[j],
                            cache_sb_v[j], sz[1].reshape(shp_s), tt)
            y = [yp.reshape(bsz * seq, SB_WIDTH), ys.reshape(dbsz * dseq, SB_WIDTH)]
            if i == DEPTH - 1:
                xs = _matmul_res_norm(y, sb_w_out[j].astype(BF16), xs, final_norm_g, tm=512, name="sb_out_norm")
            else:
                xs = _matmul(y, sb_w_out[j], F32, tn=_mm_tn(SB_WIDTH), res=xs, name="sb_out")
            kp_rows.append(k[0].reshape(bsz, seq, SB_HEADS, SB_HEAD_DIM))
            vp_rows.append(v[0].reshape(bsz, seq, SB_HEADS, SB_HEAD_DIM))
            ks_rows.append(k[1].reshape(dbsz, dseq, SB_HEADS, SB_HEAD_DIM))
            vs_rows.append(v[1].reshape(dbsz, dseq, SB_HEADS, SB_HEAD_DIM))

    if DEPTH % N_MIXERS != 0:
        xs = [_rmsnorm(x, final_norm_g, F32, min(x.shape[0], 512)) for x in xs]
    y_prompt, y_sample = xs[0].reshape(bsz, seq, d), xs[1].reshape(dbsz, dseq, d)
    return (y_prompt, y_sample, jnp.stack(kp_rows), jnp.stack(vp_rows), jnp.stack(ks_rows),
            jnp.stack(vs_rows), jnp.stack(gm_v_rows))
```

```python
import functools

import jax
import jax.numpy as jnp
from jax import lax
from jax.experimental import pallas as pl
from jax.experimental.pallas import tpu as pltpu

D_MODEL = 2048
DEPTH = 2
CHUNK = 64
N_MIXERS = 2
GM_WIDTH = 2 * D_MODEL
GM_BLOCK = 128
GM_GROUPS = 16
GM_GROUP_DIM = GM_WIDTH // GM_GROUPS
SB_HEADS = 16
SB_HEAD_DIM = D_MODEL // SB_HEADS
SB_WIDTH = SB_HEADS * SB_HEAD_DIM
NORM_EPS = 1e-6
LN_EPS = 1e-5

LANES = 128
VMEM_LIMIT_BYTES = 56 * 1024 * 1024
MM_VMEM_BUDGET = 40 * 1024 * 1024
F32 = jnp.float32
BF16 = jnp.bfloat16


def _params(n_grid_dims):
    return pltpu.CompilerParams(
        dimension_semantics=("arbitrary",) * n_grid_dims,
        vmem_limit_bytes=VMEM_LIMIT_BYTES,
    )


def _gelu(x):
    c = 0.7978845608028654
    h = 0.5 * x
    return h + h * jnp.tanh(x * (c + (c * 0.044715) * (x * x)))


def _silu(x):
    h = 0.5 * x
    return h + h * jnp.tanh(h)


def _rmsnorm_kernel(x_ref, g_ref, o_ref):
    x = x_ref[...]
    ms = jnp.mean(x * x, axis=-1, keepdims=True)
    o_ref[...] = (x * lax.rsqrt(ms + NORM_EPS) * g_ref[...]).astype(o_ref.dtype)


def _rmsnorm(x, g, out_dtype, tm):
    m, d = x.shape
    return pl.pallas_call(
        _rmsnorm_kernel,
        grid=(m // tm,),
        in_specs=[pl.BlockSpec((tm, d), lambda i: (i, 0)),
                  pl.BlockSpec((1, d), lambda i: (0, 0))],
        out_specs=pl.BlockSpec((tm, d), lambda i: (i, 0)),
        out_shape=jax.ShapeDtypeStruct((m, d), out_dtype),
        compiler_params=_params(1),
        name="rmsnorm",
    )(x, g.reshape(1, d))


MM_TM = 1024
GM_MIX_TM = 512
HEAD_TILE = 8


def _row_groups(ms, tm_max):
    tms = [min(m, tm_max) for m in ms]
    counts = [m // tm for m, tm in zip(ms, tms)]
    assert all(m == tm * c for m, tm, c in zip(ms, tms, counts))
    starts = [sum(counts[a + 1:]) for a in range(len(ms))]
    return tms, counts, starts


def _row_index(i, start, count):
    return jnp.clip(i - start, 0, count - 1)


def _on_group(i, a, counts, starts, fn):
    if len(counts) == 1:
        fn()
    else:
        pl.when(jnp.logical_and(i >= starts[a], i < starts[a] + counts[a]))(fn)


def _mm_kernel(*refs, groups, act, scale, has_res, has_gate):
    n_arr = len(groups[0])
    x_refs, w_ref = refs[:n_arr], refs[n_arr]
    pos = n_arr + 1
    r_refs = refs[pos:pos + n_arr] if has_res else None
    pos += n_arr * has_res
    gate_refs = (refs[pos:pos + n_arr], refs[pos + n_arr:pos + 2 * n_arr]) if has_gate else None
    o_refs = refs[-n_arr:]
    i = pl.program_id(1)

    def tile(a):
        acc = jnp.dot(x_refs[a][...], w_ref[...].astype(BF16), preferred_element_type=F32)
        if act == "gelu":
            acc = _gelu(acc)
        elif act == "silu":
            acc = _silu(acc)
        if scale is not None:
            acc = acc * scale
        if has_res:
            acc = acc + r_refs[a][...]
        acc = acc.astype(o_refs[a].dtype)
        if has_gate:
            acc = (gate_refs[0][a][...] * gate_refs[1][a][...]) * acc
        o_refs[a][...] = acc

    for a in range(n_arr):
        _on_group(i, a, *groups, functools.partial(tile, a))


def _matmul(xs, w, out_dtype, *, tn, col0=0, n=None, act=None, scale=None, res=None, gate=None, name="matmul"):
    k = xs[0].shape[1]
    n = w.shape[1] if n is None else n
    assert col0 % tn == 0 and n % tn == 0
    jb = col0 // tn
    tms, counts, starts = _row_groups([x.shape[0] for x in xs], MM_TM)
    in_specs = [pl.BlockSpec((tm, k), lambda j, i, s=s, c=c: (_row_index(i, s, c), 0))
                for tm, c, s in zip(tms, counts, starts)]
    in_specs.append(pl.BlockSpec((k, tn), lambda j, i: (0, j + jb)))
    row_col = [pl.BlockSpec((tm, tn), lambda j, i, s=s, c=c: (_row_index(i, s, c), j))
               for tm, c, s in zip(tms, counts, starts)]
    args = [*xs, w]
    if res is not None:
        in_specs += row_col
        args += list(res)
    if gate is not None:
        in_specs += row_col + row_col
        args += [*gate[0], *gate[1]]
    return pl.pallas_call(
        functools.partial(_mm_kernel, groups=(counts, starts), act=act, scale=scale, has_res=res is not None,
                          has_gate=gate is not None),
        grid=(n // tn, sum(counts)),
        in_specs=in_specs,
        out_specs=row_col,
        out_shape=[jax.ShapeDtypeStruct((x.shape[0], n), out_dtype) for x in xs],
        compiler_params=_params(2),
        name=name,
    )(*args)


def _mm_heads_kernel(*refs, groups):
    n_arr = len(groups[0])
    x_refs, w_ref = refs[:n_arr], refs[n_arr]
    of_refs, ob_refs = refs[n_arr + 1:2 * n_arr + 1], refs[2 * n_arr + 1:]
    i = pl.program_id(1)

    def tile(a):
        tm = x_refs[a].shape[0]
        acc = jnp.dot(x_refs[a][...], w_ref[...].astype(BF16), preferred_element_type=F32)
        ob_refs[a][...] = acc.astype(ob_refs[a].dtype)
        rows = of_refs[a].reshape(tm * HEAD_TILE, SB_HEAD_DIM)
        for h in range(HEAD_TILE):
            rows[pl.ds(h, tm, stride=HEAD_TILE), :] = acc[:, _head_cols(h)]

    for a in range(n_arr):
        _on_group(i, a, *groups, functools.partial(tile, a))


def _matmul_heads(xs, w, *, col0, name):
    k = xs[0].shape[1]
    tn = HEAD_TILE * SB_HEAD_DIM
    assert col0 % tn == 0
    jb = col0 // tn
    tms, counts, starts = _row_groups([x.shape[0] for x in xs], MM_TM)
    groups = list(zip(tms, counts, starts))
    in_specs = [pl.BlockSpec((tm, k), lambda j, i, s=s, c=c: (_row_index(i, s, c), 0)) for tm, c, s in groups]
    in_specs.append(pl.BlockSpec((k, tn), lambda j, i: (0, j + jb)))
    out_specs = [pl.BlockSpec((tm, HEAD_TILE, SB_HEAD_DIM), lambda j, i, s=s, c=c: (_row_index(i, s, c), j, 0))
                 for tm, c, s in groups]
    out_specs += [pl.BlockSpec((tm, tn), lambda j, i, s=s, c=c: (_row_index(i, s, c), j)) for tm, c, s in groups]
    outs = pl.pallas_call(
        functools.partial(_mm_heads_kernel, groups=(counts, starts)),
        grid=(SB_WIDTH // tn, sum(counts)),
        in_specs=in_specs,
        out_specs=out_specs,
        out_shape=[jax.ShapeDtypeStruct((x.shape[0], SB_HEADS, SB_HEAD_DIM), F32) for x in xs]
        + [jax.ShapeDtypeStruct((x.shape[0], SB_WIDTH), BF16) for x in xs],
        compiler_params=_params(2),
        name=name,
    )(*xs, w)
    return outs[:len(xs)], outs[len(xs):]


def _mm_res_norm_kernel(*refs, groups):
    n_arr = len(groups[0])
    x_refs, w_ref = refs[:n_arr], refs[n_arr]
    r_refs, g_ref, o_refs = refs[n_arr + 1:2 * n_arr + 1], refs[2 * n_arr + 1], refs[2 * n_arr + 2:]
    i = pl.program_id(0)

    def tile(a):
        acc = jnp.dot(x_refs[a][...], w_ref[...], preferred_element_type=F32) + r_refs[a][...]
        ms = jnp.mean(acc * acc, axis=-1, keepdims=True)
        o_refs[a][...] = acc * lax.rsqrt(ms + NORM_EPS) * g_ref[...]

    for a in range(n_arr):
        _on_group(i, a, *groups, functools.partial(tile, a))


def _matmul_res_norm(xs, w, res, g, *, tm, name):
    k, n = w.shape
    tms, counts, starts = _row_groups([x.shape[0] for x in xs], tm)
    row = lambda cols: [pl.BlockSpec((t, cols), lambda i, s=s, c=c: (_row_index(i, s, c), 0))
                        for t, c, s in zip(tms, counts, starts)]
    return pl.pallas_call(
        functools.partial(_mm_res_norm_kernel, groups=(counts, starts)),
        grid=(sum(counts),),
        in_specs=[*row(k), pl.BlockSpec((k, n), lambda i: (0, 0)), *row(n),
                  pl.BlockSpec((1, n), lambda i: (0, 0))],
        out_specs=row(n),
        out_shape=[jax.ShapeDtypeStruct((x.shape[0], n), F32) for x in xs],
        compiler_params=_params(1),
        name=name,
    )(*xs, w, *res, g.reshape(1, n))


def _gm_mix_kernel(gv_ref, wm_ref, bt_ref, lg_ref, lb_ref, *out_refs, emit_vn):
    if emit_vn:
        y_ref, vn_ref = out_refs
    else:
        (y_ref,) = out_refs
    tm = gv_ref.shape[0]
    inv_w = 1.0 / GM_WIDTH
    s1 = jnp.zeros((tm, 1), F32)
    for g in range(GM_GROUPS):
        cols = slice(g * GM_GROUP_DIM, (g + 1) * GM_GROUP_DIM)
        s1 = s1 + jnp.sum(gv_ref[:, cols].astype(F32), axis=-1, keepdims=True)
    mu = s1 * inv_w
    s2 = jnp.zeros((tm, 1), F32)
    for g in range(GM_GROUPS):
        cols = slice(g * GM_GROUP_DIM, (g + 1) * GM_GROUP_DIM)
        xc = gv_ref[:, cols].astype(F32) - mu
        s2 = s2 + jnp.sum(xc * xc, axis=-1, keepdims=True)
    rstd = lax.rsqrt(s2 * inv_w + LN_EPS)
    for g in range(GM_GROUPS):
        cols = slice(g * GM_GROUP_DIM, (g + 1) * GM_GROUP_DIM)
        vn = (gv_ref[:, cols].astype(F32) - mu) * rstd * lg_ref[:, cols] + lb_ref[:, cols]
        if emit_vn:
            vn_ref[:, cols] = vn
        vnb = vn.astype(BF16)
        wg = wm_ref[g]
        bias = bt_ref[:, g:g + 1]
        for r in range(tm // GM_BLOCK):
            rows = slice(r * GM_BLOCK, (r + 1) * GM_BLOCK)
            mixed = jnp.dot(wg, vnb[rows], preferred_element_type=F32) + bias
            y_ref[rows, cols] = mixed.astype(y_ref.dtype)


def _gm_mix(ug, wm, bt, ln_g, ln_b, *, tm, emit_vn):
    m = ug.shape[0]
    row_blk = lambda c: pl.BlockSpec((tm, GM_WIDTH), lambda i, c=c: (i, c))
    full = lambda shape: pl.BlockSpec(shape, lambda i: (0,) * len(shape))
    out_shape = [jax.ShapeDtypeStruct((m, GM_WIDTH), BF16)]
    out_specs = [row_blk(0)]
    if emit_vn:
        out_shape.append(jax.ShapeDtypeStruct((m, GM_WIDTH), F32))
        out_specs.append(row_blk(0))
    return pl.pallas_call(
        functools.partial(_gm_mix_kernel, emit_vn=emit_vn),
        grid=(m // tm,),
        in_specs=[row_blk(1), full((GM_GROUPS, GM_BLOCK, GM_BLOCK)), full((GM_BLOCK, GM_GROUPS)),
                  full((1, GM_WIDTH)), full((1, GM_WIDTH))],
        out_specs=out_specs,
        out_shape=out_shape,
        compiler_params=_params(1),
        name="gm_mix",
    )(ug, wm, bt, ln_g.reshape(1, GM_WIDTH), ln_b.reshape(1, GM_WIDTH))


SB_BLOCK = 256
SB_TQ = 256
SB_HEADS_PER_STEP = 8
LOG2_E = 1.4426950408889634
SB_DEAD_LOG = -151.5


def _sb_chunk(s, strict, carry, tri):
    r = s.shape[0]
    bw = tri.shape[0]
    nb = s.shape[1] // bw
    sbs = [s[:, b * bw:(b + 1) * bw] for b in range(nb)]
    parts = []
    for b in range(nb):
        neg = -sbs[b]
        lf = jnp.minimum(neg, 0.0) - jnp.log2(1.0 + jnp.exp2(jnp.minimum(sbs[b], neg)))
        if strict is not None and b == nb - 1:
            lf = jnp.where(strict, lf, 0.0)
        parts.append(lf.astype(BF16))
    c_all = jnp.dot(jnp.concatenate(parts, axis=0), tri, preferred_element_type=F32)
    ps = [None] * nb
    for b in reversed(range(nb)):
        c = c_all[b * r:(b + 1) * r]
        p = jnp.exp2(sbs[b] + (c + carry))
        if strict is not None and b == nb - 1:
            p = jnp.where(strict, p, 0.0)
        ps[b] = p.astype(BF16)
        carry = carry + c[:, 0:1]
    return ps, carry


def _qk(q, k):
    return lax.dot_general(q, k, (((1,), (1,)), ((), ())), preferred_element_type=F32)


def _pv(ps, v):
    bw = ps[0].shape[1]
    acc = None
    for b, p in enumerate(ps):
        d = jnp.dot(p, v[b * bw:(b + 1) * bw], preferred_element_type=F32)
        acc = d if acc is None else acc + d
    return acc


def _head_cols(h):
    return slice(h * SB_HEAD_DIM, (h + 1) * SB_HEAD_DIM)


def _sb_prompt_kernel(q_ref, k_ref, v_ref, sz_ref, tri_ref, o_ref):
    tq = SB_TQ
    heads = range(SB_HEADS_PER_STEP)
    kb_ref, vb_ref = k_ref.at[0], v_ref.at[0]
    kmax = []
    for h in heads:
        ka = jnp.abs(kb_ref[:, _head_cols(h)].astype(F32))
        kmax.append(jnp.max(jnp.max(ka, axis=0, keepdims=True), axis=1, keepdims=True))

    tri = tri_ref[...]
    row = lax.broadcasted_iota(jnp.int32, (tq, SB_BLOCK), 0)
    col = lax.broadcasted_iota(jnp.int32, (tq, SB_BLOCK), 1)
    strict = col < row
    zero_carry = jnp.zeros((tq, 1), F32)

    def chunk(h, q, k0, width, mask, carry):
        s = _qk(q, kb_ref[pl.ds(k0, width), _head_cols(h)])
        ps, carry = _sb_chunk(s, mask, carry, tri)
        return _pv(ps, vb_ref[pl.ds(k0, width), _head_cols(h)]), carry

    def store(h, rows, acc):
        o_ref[0, rows, _head_cols(h)] = (acc * sz_ref[0, rows, _head_cols(h)].astype(F32)).astype(o_ref.dtype)

    for h in heads:
        acc, _ = chunk(h, q_ref[0, 0:tq, _head_cols(h)], 0, tq, strict, zero_carry)
        store(h, slice(0, tq), acc)

    def q_block(qi, _):
        rows = pl.ds(pl.multiple_of(qi * tq, tq), tq)
        qs = [q_ref[0, rows, _head_cols(h)] for h in heads]
        bounds = []
        for h in heads:
            q_l1 = jnp.sum(jnp.abs(qs[h].astype(F32)), axis=-1, keepdims=True)
            bounds.append(q_l1 * kmax[h] * 1.001)

        def alive(carries):
            worst = carries[0] + bounds[0]
            for h in heads[1:]:
                worst = jnp.maximum(worst, carries[h] + bounds[h])
            return (jnp.max(worst) > SB_DEAD_LOG).astype(jnp.int32)

        k0 = pl.multiple_of((qi - 1) * SB_BLOCK, SB_BLOCK)
        first = [chunk(h, qs[h], k0, SB_BLOCK + tq, strict, zero_carry) for h in heads]
        accs = tuple(f[0] for f in first)
        carries = tuple(f[1] for f in first)

        def cond(st):
            return jnp.logical_and(st[0] >= 0, st[1] > 0)

        def body(st):
            kb_idx, _, carries, accs = st
            k0 = pl.multiple_of(kb_idx * SB_BLOCK, SB_BLOCK)
            nxt = [chunk(h, qs[h], k0, SB_BLOCK, None, carries[h]) for h in heads]
            carries = tuple(n[1] for n in nxt)
            accs = tuple(a + n[0] for a, n in zip(accs, nxt))
            return kb_idx - 1, alive(carries), carries, accs

        st = lax.while_loop(cond, body, (qi - 2, alive(carries), carries, accs))
        for h in heads:
            store(h, rows, st[3][h])
        return 0

    lax.fori_loop(1, q_ref.shape[1] // tq, q_block, 0)


def _sb_prompt(q, k, v, sz, tri):
    b, l, _ = q.shape
    assert SB_TQ == SB_BLOCK and l % SB_TQ == 0
    gw = SB_HEADS_PER_STEP * SB_HEAD_DIM
    blk = pl.BlockSpec((1, l, gw), lambda bi, g: (bi, 0, g))
    return pl.pallas_call(
        _sb_prompt_kernel,
        grid=(b, SB_HEADS // SB_HEADS_PER_STEP),
        in_specs=[blk, blk, blk, blk, pl.BlockSpec((SB_BLOCK, SB_BLOCK), lambda bi, g: (0, 0))],
        out_specs=blk,
        out_shape=jax.ShapeDtypeStruct(q.shape, BF16),
        compiler_params=_params(2),
        name="sb_prompt",
    )(q, k, v, sz, tri)


def _sb_decode_kernel(q_ref, kn_ref, vn_ref, sz_ref, tri_ref, kc_ref, vc_ref, o_ref):
    past, nh, _ = kc_ref.shape
    kc_ref = kc_ref.reshape(past * nh, SB_HEAD_DIM)
    vc_ref = vc_ref.reshape(past * nh, SB_HEAD_DIM)
    tq = q_ref.shape[1]
    tri = tri_ref[...]
    row = lax.broadcasted_iota(jnp.int32, (tq, LANES), 0)
    col = lax.broadcasted_iota(jnp.int32, (tq, LANES), 1)
    pad = jnp.zeros((LANES - tq, SB_HEAD_DIM), BF16)
    for h in range(nh):
        cols = _head_cols(h)
        q = q_ref[0, :, cols]
        kn = jnp.concatenate([kn_ref[0, :, cols], pad], axis=0)
        vn = jnp.concatenate([vn_ref[0, :, cols], pad], axis=0)
        ps, carry = _sb_chunk(_qk(q, kn), col < row, jnp.zeros((tq, 1), F32), tri[:LANES, :LANES])
        acc = _pv(ps, vn)
        head_rows = pl.ds(h, past, stride=nh)
        s_t = _qk(kc_ref[head_rows, :].astype(BF16), jnp.concatenate([q, pad], axis=0))
        ps, _ = _sb_chunk(s_t.T[:tq], None, carry, tri)
        acc = acc + _pv(ps, vc_ref[head_rows, :].astype(BF16))
        o_ref[0, :, cols] = (acc * sz_ref[0, :, cols].astype(F32)).astype(o_ref.dtype)


def _sb_decode(q, k_new, v_new, k_cache, v_cache, sz, tri):
    b, t, _ = q.shape
    _, p, nh, dh = k_cache.shape
    assert nh == SB_HEADS and dh == SB_HEAD_DIM
    g = HEAD_TILE
    blk = pl.BlockSpec((1, t, g * dh), lambda bi, gi: (bi, 0, gi))
    blk_cache = pl.BlockSpec((None, p, g, dh), lambda bi, gi: (bi, 0, gi, 0))
    return pl.pallas_call(
        _sb_decode_kernel,
        grid=(b, nh // g),
        in_specs=[blk, blk, blk, blk, pl.BlockSpec((SB_BLOCK, SB_BLOCK), lambda bi, gi: (0, 0)),
                  blk_cache, blk_cache],
        out_specs=blk,
        out_shape=jax.ShapeDtypeStruct(q.shape, BF16),
        compiler_params=_params(2),
        name="sb_decode",
    )(q, k_new, v_new, sz, tri, k_cache, v_cache)


def _chunk_causal(n):
    pos = jnp.arange(n)
    return (pos[None, :] // CHUNK) <= (pos[:, None] // CHUNK)


def _mm_tn(k, w_itemsize=4, out_bytes=8):
    for tn in (1024, 512, 256):
        if 2 * (MM_TM * k * 2 + k * tn * w_itemsize + MM_TM * tn * out_bytes) <= MM_VMEM_BUDGET:
            return tn
    raise ValueError("matmul blocks do not fit VMEM")


def _gm_mix_params(seq_len, w_s, b_s):
    blk = min(seq_len, GM_BLOCK)
    w = w_s[:, :blk, :blk] * _chunk_causal(blk).astype(w_s.dtype)
    reps = GM_BLOCK // blk
    wm = jnp.einsum("ab,gts->gatbs", jnp.eye(reps, dtype=w.dtype), w).reshape(
        GM_GROUPS, GM_BLOCK, GM_BLOCK).astype(BF16)
    return wm, jnp.tile(b_s[:, :blk].T, (reps, 1))


def _gm_layer(xs, seq_lens, norm_g, w_in, ln_g, ln_b, w_s, b_s, w_out):
    d = xs[0].shape[1]
    hs = [_rmsnorm(x, norm_g, BF16, min(x.shape[0], 1024)) for x in xs]
    tn = _mm_tn(d, out_bytes=2)
    ugs = _matmul(hs, w_in, BF16, tn=tn, n=2 * GM_WIDTH, act="gelu", name="gm_in_uv")
    mixed, v_rows = [], None
    for g, (ug, seq_len) in enumerate(zip(ugs, seq_lens)):
        emit_vn = g == len(xs) - 1
        outs = _gm_mix(ug, *_gm_mix_params(seq_len, w_s, b_s), ln_g, ln_b,
                       tm=min(ug.shape[0], GM_MIX_TM), emit_vn=emit_vn)
        mixed.append(outs[0])
        v_rows = outs[1] if emit_vn else v_rows
    ys = _matmul(hs, w_in, BF16, tn=tn, col0=2 * GM_WIDTH, n=GM_WIDTH, act="silu", gate=(ugs, mixed),
                 name="gm_in_z")
    x_new = _matmul(ys, w_out, F32, tn=_mm_tn(GM_WIDTH, w_out.dtype.itemsize), res=xs, name="gm_out")
    return x_new, v_rows


def _sb_in(xs, norm_g, w_in):
    d = xs[0].shape[1]
    hs = [_rmsnorm(x, norm_g, BF16, min(x.shape[0], 1024)) for x in xs]
    tn = _mm_tn(d, out_bytes=2)
    w = SB_WIDTH
    q = _matmul(hs, w_in, BF16, tn=tn, n=w, scale=SB_HEAD_DIM ** -0.5 * LOG2_E, name="sb_in_q")
    k, kb = _matmul_heads(hs, w_in, col0=w, name="sb_in_k")
    v, vb = _matmul_heads(hs, w_in, col0=2 * w, name="sb_in_v")
    sz = _matmul(hs, w_in, BF16, tn=tn, col0=3 * w, n=w, act="silu", name="sb_in_z")
    return q, k, v, kb, vb, sz


def kernel(x_prompt, x_sample, cache_sb_k, cache_sb_v, norm_g, final_norm_g, gm_w_in, gm_ln_g, gm_ln_b,
           gm_w_s, gm_b_s, gm_w_out, sb_w_in, sb_w_out):
    bsz, seq, d = x_prompt.shape
    dbsz, dseq, _ = x_sample.shape
    xs = [x_prompt.reshape(bsz * seq, d), x_sample.reshape(dbsz * dseq, d)]

    tt = (jnp.arange(SB_BLOCK)[:, None] >= jnp.arange(SB_BLOCK)[None, :]).astype(BF16)

    gm_v_rows, kp_rows, vp_rows, ks_rows, vs_rows = [], [], [], [], []
    for i in range(DEPTH):
        j = i // N_MIXERS
        if i % N_MIXERS == 0:
            xs, v_new = _gm_layer(xs, (seq, dseq), norm_g[i], gm_w_in[j], gm_ln_g[j], gm_ln_b[j], gm_w_s[j],
                                  gm_b_s[j], gm_w_out[j].astype(BF16))
            gm_v_rows.append(v_new.reshape(dbsz, dseq, GM_WIDTH))
        else:
            q, k, v, kb, vb, sz = _sb_in(xs, norm_g[i], sb_w_in[j])
            shp_p, shp_s = (bsz, seq, SB_WIDTH), (dbsz, dseq, SB_WIDTH)
            yp = _sb_prompt(q[0].reshape(shp_p), kb[0].reshape(shp_p), vb[0].reshape(shp_p),
                            sz[0].reshape(shp_p), tt)
            ys = _sb_decode(q[1].reshape(shp_s), kb[1].reshape(shp_s), vb[1].reshape(shp_s), cache_sb_k[j],
                            cache_sb_v[j], sz[1].reshape(shp_s), tt)
            y = [yp.reshape(bsz * seq, SB_WIDTH), ys.reshape(dbsz * dseq, SB_WIDTH)]
            if i == DEPTH - 1:
                xs = _matmul_res_norm(y, sb_w_out[j].astype(BF16), xs, final_norm_g, tm=512, name="sb_out_norm")
            else:
                xs = _matmul(y, sb_w_out[j], F32, tn=_mm_tn(SB_WIDTH), res=xs, name="sb_out")
            kp_rows.append(k[0].reshape(bsz, seq, SB_HEADS, SB_HEAD_DIM))
            vp_rows.append(v[0].reshape(bsz, seq, SB_HEADS, SB_HEAD_DIM))
            ks_rows.append(k[1].reshape(dbsz, dseq, SB_HEADS, SB_HEAD_DIM))
            vs_rows.append(v[1].reshape(dbsz, dseq, SB_HEADS, SB_HEAD_DIM))

    if DEPTH % N_MIXERS != 0:
        xs = [_rmsnorm(x, final_norm_g, F32, min(x.shape[0], 512)) for x in xs]
    y_prompt, y_sample = xs[0].reshape(bsz, seq, d), xs[1].reshape(dbsz, dseq, d)
    return (y_prompt, y_sample, jnp.stack(kp_rows), jnp.stack(vp_rows), jnp.stack(ks_rows),
            jnp.stack(vs_rows), jnp.stack(gm_v_rows))
```

```python
import functools

import jax
import jax.numpy as jnp
from jax import lax
from jax.experimental import pallas as pl
from jax.experimental.pallas import tpu as pltpu

D_MODEL = 2048
DEPTH = 2
CHUNK = 64
N_MIXERS = 2
GM_WIDTH = 2 * D_MODEL
GM_BLOCK = 128
GM_GROUPS = 16
GM_GROUP_DIM = GM_WIDTH // GM_GROUPS
SB_HEADS = 16
SB_HEAD_DIM = D_MODEL // SB_HEADS
SB_WIDTH = SB_HEADS * SB_HEAD_DIM
NORM_EPS = 1e-6
LN_EPS = 1e-5

LANES = 128
VMEM_LIMIT_BYTES = 56 * 1024 * 1024
MM_VMEM_BUDGET = 40 * 1024 * 1024
F32 = jnp.float32
BF16 = jnp.bfloat16


def _params(n_grid_dims):
    return pltpu.CompilerParams(
        dimension_semantics=("arbitrary",) * n_grid_dims,
        vmem_limit_bytes=VMEM_LIMIT_BYTES,
    )


def _gelu(x):
    c = 0.7978845608028654
    h = 0.5 * x
    return h + h * jnp.tanh(x * (c + (c * 0.044715) * (x * x)))


def _silu(x):
    h = 0.5 * x
    return h + h * jnp.tanh(h)


def _rmsnorm_kernel(x_ref, g_ref, o_ref):
    x = x_ref[...]
    ms = jnp.mean(x * x, axis=-1, keepdims=True)
    o_ref[...] = (x * lax.rsqrt(ms + NORM_EPS) * g_ref[...]).astype(o_ref.dtype)


def _rmsnorm(x, g, out_dtype, tm):
    m, d = x.shape
    return pl.pallas_call(
        _rmsnorm_kernel,
        grid=(m // tm,),
        in_specs=[pl.BlockSpec((tm, d), lambda i: (i, 0)),
                  pl.BlockSpec((1, d), lambda i: (0, 0))],
        out_specs=pl.BlockSpec((tm, d), lambda i: (i, 0)),
        out_shape=jax.ShapeDtypeStruct((m, d), out_dtype),
        compiler_params=_params(1),
        name="rmsnorm",
    )(x, g.reshape(1, d))


MM_TM = 1024
GM_MIX_TM = 512
HEAD_TILE = 8


def _row_groups(ms, tm_max):
    tms = [min(m, tm_max) for m in ms]
    counts = [m // tm for m, tm in zip(ms, tms)]
    assert all(m == tm * c for m, tm, c in zip(ms, tms, counts))
    starts = [sum(counts[a + 1:]) for a in range(len(ms))]
    return tms, counts, starts


def _row_index(i, start, count):
    return jnp.clip(i - start, 0, count - 1)


def _on_group(i, a, counts, starts, fn):
    if len(counts) == 1:
        fn()
    else:
        pl.when(jnp.logical_and(i >= starts[a], i < starts[a] + counts[a]))(fn)


def _mm_kernel(*refs, groups, act, scale, has_res, has_gate):
    n_arr = len(groups[0])
    x_refs, w_ref = refs[:n_arr], refs[n_arr]
    pos = n_arr + 1
    r_refs = refs[pos:pos + n_arr] if has_res else None
    pos += n_arr * has_res
    gate_refs = (refs[pos:pos + n_arr], refs[pos + n_arr:pos + 2 * n_arr]) if has_gate else None
    o_refs = refs[-n_arr:]
    i = pl.program_id(1)

    def tile(a):
        acc = jnp.dot(x_refs[a][...], w_ref[...].astype(BF16), preferred_element_type=F32)
        if act == "gelu":
            acc = _gelu(acc)
        elif act == "silu":
            acc = _silu(acc)
        if scale is not None:
            acc = acc * scale
        if has_res:
            acc = acc + r_refs[a][...]
        acc = acc.astype(o_refs[a].dtype)
        if has_gate:
            acc = (gate_refs[0][a][...] * gate_refs[1][a][...]) * acc
        o_refs[a][...] = acc

    for a in range(n_arr):
        _on_group(i, a, *groups, functools.partial(tile, a))


def _matmul(xs, w, out_dtype, *, tn, col0=0, n=None, act=None, scale=None, res=None, gate=None, name="matmul"):
    k = xs[0].shape[1]
    n = w.shape[1] if n is None else n
    assert col0 % tn == 0 and n % tn == 0
    jb = col0 // tn
    tms, counts, starts = _row_groups([x.shape[0] for x in xs], MM_TM)
    in_specs = [pl.BlockSpec((tm, k), lambda j, i, s=s, c=c: (_row_index(i, s, c), 0))
                for tm, c, s in zip(tms, counts, starts)]
    in_specs.append(pl.BlockSpec((k, tn), lambda j, i: (0, j + jb)))
    row_col = [pl.BlockSpec((tm, tn), lambda j, i, s=s, c=c: (_row_index(i, s, c), j))
               for tm, c, s in zip(tms, counts, starts)]
    args = [*xs, w]
    if res is not None:
        in_specs += row_col
        args += list(res)
    if gate is not None:
        in_specs += row_col + row_col
        args += [*gate[0], *gate[1]]
    return pl.pallas_call(
        functools.partial(_mm_kernel, groups=(counts, starts), act=act, scale=scale, has_res=res is not None,
                          has_gate=gate is not None),
        grid=(n // tn, sum(counts)),
        in_specs=in_specs,
        out_specs=row_col,
        out_shape=[jax.ShapeDtypeStruct((x.shape[0], n), out_dtype) for x in xs],
        compiler_params=_params(2),
        name=name,
    )(*args)


def _mm_heads_kernel(*refs, groups):
    n_arr = len(groups[0])
    x_refs, w_ref = refs[:n_arr], refs[n_arr]
    of_refs, ob_refs = refs[n_arr + 1:2 * n_arr + 1], refs[2 * n_arr + 1:]
    i = pl.program_id(1)

    def tile(a):
        tm = x_refs[a].shape[0]
        acc = jnp.dot(x_refs[a][...], w_ref[...].astype(BF16), preferred_element_type=F32)
        ob_refs[a][...] = acc.astype(ob_refs[a].dtype)
        rows = of_refs[a].reshape(tm * HEAD_TILE, SB_HEAD_DIM)
        for h in range(HEAD_TILE):
            rows[pl.ds(h, tm, stride=HEAD_TILE), :] = acc[:, _head_cols(h)]

    for a in range(n_arr):
        _on_group(i, a, *groups, functools.partial(tile, a))


def _matmul_heads(xs, w, *, col0, name):
    k = xs[0].shape[1]
    tn = HEAD_TILE * SB_HEAD_DIM
    assert col0 % tn == 0
    jb = col0 // tn
    tms, counts, starts = _row_groups([x.shape[0] for x in xs], MM_TM)
    groups = list(zip(tms, counts, starts))
    in_specs = [pl.BlockSpec((tm, k), lambda j, i, s=s, c=c: (_row_index(i, s, c), 0)) for tm, c, s in groups]
    in_specs.append(pl.BlockSpec((k, tn), lambda j, i: (0, j + jb)))
    out_specs = [pl.BlockSpec((tm, HEAD_TILE, SB_HEAD_DIM), lambda j, i, s=s, c=c: (_row_index(i, s, c), j, 0))
                 for tm, c, s in groups]
    out_specs += [pl.BlockSpec((tm, tn), lambda j, i, s=s, c=c: (_row_index(i, s, c), j)) for tm, c, s in groups]
    outs = pl.pallas_call(
        functools.partial(_mm_heads_kernel, groups=(counts, starts)),
        grid=(SB_WIDTH // tn, sum(counts)),
        in_specs=in_specs,
        out_specs=out_specs,
        out_shape=[jax.ShapeDtypeStruct((x.shape[0], SB_HEADS, SB_HEAD_DIM), F32) for x in xs]
        + [jax.ShapeDtypeStruct((x.shape[0], SB_WIDTH), BF16) for x in xs],
        compiler_params=_params(2),
        name=name,
    )(*xs, w)
    return outs[:len(xs)], outs[len(xs):]


def _mm_res_norm_kernel(*refs, groups):
    n_arr = len(groups[0])
    x_refs, w_ref = refs[:n_arr], refs[n_arr]
    r_refs, g_ref, o_refs = refs[n_arr + 1:2 * n_arr + 1], refs[2 * n_arr + 1], refs[2 * n_arr + 2:]
    i = pl.program_id(0)

    def tile(a):
        acc = jnp.dot(x_refs[a][...], w_ref[...], preferred_element_type=F32) + r_refs[a][...]
        ms = jnp.mean(acc * acc, axis=-1, keepdims=True)
        o_refs[a][...] = acc * lax.rsqrt(ms + NORM_EPS) * g_ref[...]

    for a in range(n_arr):
        _on_group(i, a, *groups, functools.partial(tile, a))


def _matmul_res_norm(xs, w, res, g, *, tm, name):
    k, n = w.shape
    tms, counts, starts = _row_groups([x.shape[0] for x in xs], tm)
    row = lambda cols: [pl.BlockSpec((t, cols), lambda i, s=s, c=c: (_row_index(i, s, c), 0))
                        for t, c, s in zip(tms, counts, starts)]
    return pl.pallas_call(
        functools.partial(_mm_res_norm_kernel, groups=(counts, starts)),
        grid=(sum(counts),),
        in_specs=[*row(k), pl.BlockSpec((k, n), lambda i: (0, 0)), *row(n),
                  pl.BlockSpec((1, n), lambda i: (0, 0))],
        out_specs=row(n),
        out_shape=[jax.ShapeDtypeStruct((x.shape[0], n), F32) for x in xs],
        compiler_params=_params(1),
        name=name,
    )(*xs, w, *res, g.reshape(1, n))


def _gm_mix_kernel(gv_ref, wm_ref, bt_ref, lg_ref, lb_ref, *out_refs, emit_vn):
    *out_refs, xc_ref = out_refs
    if emit_vn:
        y_ref, vn_ref = out_refs
    else:
        (y_ref,) = out_refs
    tm = gv_ref.shape[0]
    inv_w = 1.0 / GM_WIDTH
    s1 = jnp.zeros((tm, 1), F32)
    for g in range(GM_GROUPS):
        cols = slice(g * GM_GROUP_DIM, (g + 1) * GM_GROUP_DIM)
        s1 = s1 + jnp.sum(gv_ref[:, cols].astype(F32), axis=-1, keepdims=True)
    mu = s1 * inv_w
    s2 = jnp.zeros((tm, 1), F32)
    for g in range(GM_GROUPS):
        cols = slice(g * GM_GROUP_DIM, (g + 1) * GM_GROUP_DIM)
        xc = gv_ref[:, cols].astype(F32) - mu
        xc_ref[:, cols] = xc
        s2 = s2 + jnp.sum(xc * xc, axis=-1, keepdims=True)
    rstd = lax.rsqrt(s2 * inv_w + LN_EPS)
    for g in range(GM_GROUPS):
        cols = slice(g * GM_GROUP_DIM, (g + 1) * GM_GROUP_DIM)
        vn = xc_ref[:, cols] * rstd * lg_ref[:, cols] + lb_ref[:, cols]
        if emit_vn:
            vn_ref[:, cols] = vn
        vnb = vn.astype(BF16)
        wg = wm_ref[g]
        bias = bt_ref[:, g:g + 1]
        for r in range(tm // GM_BLOCK):
            rows = slice(r * GM_BLOCK, (r + 1) * GM_BLOCK)
            mixed = jnp.dot(wg, vnb[rows], preferred_element_type=F32) + bias
            y_ref[rows, cols] = mixed.astype(y_ref.dtype)


def _gm_mix(ug, wm, bt, ln_g, ln_b, *, tm, emit_vn):
    m = ug.shape[0]
    row_blk = lambda c: pl.BlockSpec((tm, GM_WIDTH), lambda i, c=c: (i, c))
    full = lambda shape: pl.BlockSpec(shape, lambda i: (0,) * len(shape))
    out_shape = [jax.ShapeDtypeStruct((m, GM_WIDTH), BF16)]
    out_specs = [row_blk(0)]
    if emit_vn:
        out_shape.append(jax.ShapeDtypeStruct((m, GM_WIDTH), F32))
        out_specs.append(row_blk(0))
    return pl.pallas_call(
        functools.partial(_gm_mix_kernel, emit_vn=emit_vn),
        grid=(m // tm,),
        in_specs=[row_blk(1), full((GM_GROUPS, GM_BLOCK, GM_BLOCK)), full((GM_BLOCK, GM_GROUPS)),
                  full((1, GM_WIDTH)), full((1, GM_WIDTH))],
        out_specs=out_specs,
        out_shape=out_shape,
        scratch_shapes=[pltpu.VMEM((tm, GM_WIDTH), F32)],
        compiler_params=_params(1),
        name="gm_mix",
    )(ug, wm, bt, ln_g.reshape(1, GM_WIDTH), ln_b.reshape(1, GM_WIDTH))


SB_BLOCK = 256
SB_TQ = 256
SB_HEADS_PER_STEP = 8
LOG2_E = 1.4426950408889634
SB_DEAD_LOG = -151.5


def _sb_chunk(s, strict, carry, tri):
    r = s.shape[0]
    bw = tri.shape[0]
    nb = s.shape[1] // bw
    sbs = [s[:, b * bw:(b + 1) * bw] for b in range(nb)]
    parts = []
    for b in range(nb):
        neg = -sbs[b]
        lf = jnp.minimum(neg, 0.0) - jnp.log2(1.0 + jnp.exp2(jnp.minimum(sbs[b], neg)))
        if strict is not None and b == nb - 1:
            lf = jnp.where(strict, lf, 0.0)
        parts.append(lf.astype(BF16))
    c_all = jnp.dot(jnp.concatenate(parts, axis=0), tri, preferred_element_type=F32)
    ps = [None] * nb
    for b in reversed(range(nb)):
        c = c_all[b * r:(b + 1) * r]
        p = jnp.exp2(sbs[b] + (c + carry))
        if strict is not None and b == nb - 1:
            p = jnp.where(strict, p, 0.0)
        ps[b] = p.astype(BF16)
        carry = carry + c[:, 0:1]
    return ps, carry


def _qk(q, k):
    return lax.dot_general(q, k, (((1,), (1,)), ((), ())), preferred_element_type=F32)


def _pv(ps, v):
    bw = ps[0].shape[1]
    acc = None
    for b, p in enumerate(ps):
        d = jnp.dot(p, v[b * bw:(b + 1) * bw], preferred_element_type=F32)
        acc = d if acc is None else acc + d
    return acc


def _head_cols(h):
    return slice(h * SB_HEAD_DIM, (h + 1) * SB_HEAD_DIM)


def _sb_prompt_kernel(q_ref, k_ref, v_ref, sz_ref, tri_ref, o_ref):
    tq = SB_TQ
    heads = range(SB_HEADS_PER_STEP)
    kb_ref, vb_ref = k_ref.at[0], v_ref.at[0]
    kmax = []
    for h in heads:
        ka = jnp.abs(kb_ref[:, _head_cols(h)].astype(F32))
        kmax.append(jnp.max(jnp.max(ka, axis=0, keepdims=True), axis=1, keepdims=True))

    tri = tri_ref[...]
    row = lax.broadcasted_iota(jnp.int32, (tq, SB_BLOCK), 0)
    col = lax.broadcasted_iota(jnp.int32, (tq, SB_BLOCK), 1)
    strict = col < row
    zero_carry = jnp.zeros((tq, 1), F32)

    def chunk(h, q, k0, width, mask, carry):
        s = _qk(q, kb_ref[pl.ds(k0, width), _head_cols(h)])
        ps, carry = _sb_chunk(s, mask, carry, tri)
        return _pv(ps, vb_ref[pl.ds(k0, width), _head_cols(h)]), carry

    def store(h, rows, acc):
        o_ref[0, rows, _head_cols(h)] = (acc * sz_ref[0, rows, _head_cols(h)].astype(F32)).astype(o_ref.dtype)

    for h in heads:
        acc, _ = chunk(h, q_ref[0, 0:tq, _head_cols(h)], 0, tq, strict, zero_carry)
        store(h, slice(0, tq), acc)

    def q_block(qi, _):
        rows = pl.ds(pl.multiple_of(qi * tq, tq), tq)
        qs = [q_ref[0, rows, _head_cols(h)] for h in heads]
        bounds = []
        for h in heads:
            q_l1 = jnp.sum(jnp.abs(qs[h].astype(F32)), axis=-1, keepdims=True)
            bounds.append(q_l1 * kmax[h] * 1.001)

        def alive(carries):
            worst = carries[0] + bounds[0]
            for h in heads[1:]:
                worst = jnp.maximum(worst, carries[h] + bounds[h])
            return (jnp.max(worst) > SB_DEAD_LOG).astype(jnp.int32)

        k0 = pl.multiple_of((qi - 1) * SB_BLOCK, SB_BLOCK)
        first = [chunk(h, qs[h], k0, SB_BLOCK + tq, strict, zero_carry) for h in heads]
        accs = tuple(f[0] for f in first)
        carries = tuple(f[1] for f in first)

        def cond(st):
            return jnp.logical_and(st[0] >= 0, st[1] > 0)

        def body(st):
            kb_idx, _, carries, accs = st
            k0 = pl.multiple_of(kb_idx * SB_BLOCK, SB_BLOCK)
            nxt = [chunk(h, qs[h], k0, SB_BLOCK, None, carries[h]) for h in heads]
            carries = tuple(n[1] for n in nxt)
            accs = tuple(a + n[0] for a, n in zip(accs, nxt))
            return kb_idx - 1, alive(carries), carries, accs

        st = lax.while_loop(cond, body, (qi - 2, alive(carries), carries, accs))
        for h in heads:
            store(h, rows, st[3][h])
        return 0

    lax.fori_loop(1, q_ref.shape[1] // tq, q_block, 0)


def _sb_prompt(q, k, v, sz, tri):
    b, l, _ = q.shape
    assert SB_TQ == SB_BLOCK and l % SB_TQ == 0
    gw = SB_HEADS_PER_STEP * SB_HEAD_DIM
    blk = pl.BlockSpec((1, l, gw), lambda bi, g: (bi, 0, g))
    return pl.pallas_call(
        _sb_prompt_kernel,
        grid=(b, SB_HEADS // SB_HEADS_PER_STEP),
        in_specs=[blk, blk, blk, blk, pl.BlockSpec((SB_BLOCK, SB_BLOCK), lambda bi, g: (0, 0))],
        out_specs=blk,
        out_shape=jax.ShapeDtypeStruct(q.shape, BF16),
        compiler_params=_params(2),
        name="sb_prompt",
    )(q, k, v, sz, tri)


def _sb_decode_kernel(q_ref, kn_ref, vn_ref, sz_ref, tri_ref, kc_ref, vc_ref, o_ref):
    past, nh, _ = kc_ref.shape
    kc_ref = kc_ref.reshape(past * nh, SB_HEAD_DIM)
    vc_ref = vc_ref.reshape(past * nh, SB_HEAD_DIM)
    tq = q_ref.shape[1]
    tri = tri_ref[...]
    row = lax.broadcasted_iota(jnp.int32, (tq, LANES), 0)
    col = lax.broadcasted_iota(jnp.int32, (tq, LANES), 1)
    pad = jnp.zeros((LANES - tq, SB_HEAD_DIM), BF16)
    for h in range(nh):
        cols = _head_cols(h)
        q = q_ref[0, :, cols]
        kn = jnp.concatenate([kn_ref[0, :, cols], pad], axis=0)
        vn = jnp.concatenate([vn_ref[0, :, cols], pad], axis=0)
        ps, carry = _sb_chunk(_qk(q, kn), col < row, jnp.zeros((tq, 1), F32), tri[:LANES, :LANES])
        acc = _pv(ps, vn)
        head_rows = pl.ds(h, past, stride=nh)
        s_t = _qk(kc_ref[head_rows, :].astype(BF16), jnp.concatenate([q, pad], axis=0))
        ps, _ = _sb_chunk(s_t.T[:tq], None, carry, tri)
        acc = acc + _pv(ps, vc_ref[head_rows, :].astype(BF16))
        o_ref[0, :, cols] = (acc * sz_ref[0, :, cols].astype(F32)).astype(o_ref.dtype)


def _sb_decode(q, k_new, v_new, k_cache, v_cache, sz, tri):
    b, t, _ = q.shape
    _, p, nh, dh = k_cache.shape
    assert nh == SB_HEADS and dh == SB_HEAD_DIM
    g = HEAD_TILE
    blk = pl.BlockSpec((1, t, g * dh), lambda bi, gi: (bi, 0, gi))
    blk_cache = pl.BlockSpec((None, p, g, dh), lambda bi, gi: (bi, 0, gi, 0))
    return pl.pallas_call(
        _sb_decode_kernel,
        grid=(b, nh // g),
        in_specs=[blk, blk, blk, blk, pl.BlockSpec((SB_BLOCK, SB_BLOCK), lambda bi, gi: (0, 0)),
                  blk_cache, blk_cache],
        out_specs=blk,
        out_shape=jax.ShapeDtypeStruct(q.shape, BF16),
        compiler_params=_params(2),
        name="sb_decode",
    )(q, k_new, v_new, sz, tri, k_cache, v_cache)


def _chunk_causal(n):
    pos = jnp.arange(n)
    return (pos[None, :] // CHUNK) <= (pos[:, None] // CHUNK)


def _mm_tn(k, w_itemsize=4, out_bytes=8):
    for tn in (1024, 512, 256):
        if 2 * (MM_TM * k * 2 + k * tn * w_itemsize + MM_TM * tn * out_bytes) <= MM_VMEM_BUDGET:
            return tn
    raise ValueError("matmul blocks do not fit VMEM")


def _gm_mix_params(seq_len, w_s, b_s):
    blk = min(seq_len, GM_BLOCK)
    w = w_s[:, :blk, :blk] * _chunk_causal(blk).astype(w_s.dtype)
    reps = GM_BLOCK // blk
    wm = jnp.einsum("ab,gts->gatbs", jnp.eye(reps, dtype=w.dtype), w).reshape(
        GM_GROUPS, GM_BLOCK, GM_BLOCK).astype(BF16)
    return wm, jnp.tile(b_s[:, :blk].T, (reps, 1))


def _gm_layer(xs, seq_lens, norm_g, w_in, ln_g, ln_b, w_s, b_s, w_out):
    d = xs[0].shape[1]
    hs = [_rmsnorm(x, norm_g, BF16, min(x.shape[0], 1024)) for x in xs]
    tn = _mm_tn(d, out_bytes=2)
    ugs = _matmul(hs, w_in, BF16, tn=tn, n=2 * GM_WIDTH, act="gelu", name="gm_in_uv")
    mixed, v_rows = [], None
    for g, (ug, seq_len) in enumerate(zip(ugs, seq_lens)):
        emit_vn = g == len(xs) - 1
        outs = _gm_mix(ug, *_gm_mix_params(seq_len, w_s, b_s), ln_g, ln_b,
                       tm=min(ug.shape[0], GM_MIX_TM), emit_vn=emit_vn)
        mixed.append(outs[0])
        v_rows = outs[1] if emit_vn else v_rows
    ys = _matmul(hs, w_in, BF16, tn=tn, col0=2 * GM_WIDTH, n=GM_WIDTH, act="silu", gate=(ugs, mixed),
                 name="gm_in_z")
    x_new = _matmul(ys, w_out, F32, tn=_mm_tn(GM_WIDTH, w_out.dtype.itemsize), res=xs, name="gm_out")
    return x_new, v_rows


def _sb_in(xs, norm_g, w_in):
    d = xs[0].shape[1]
    hs = [_rmsnorm(x, norm_g, BF16, min(x.shape[0], 1024)) for x in xs]
    tn = _mm_tn(d, out_bytes=2)
    w = SB_WIDTH
    q = _matmul(hs, w_in, BF16, tn=tn, n=w, scale=SB_HEAD_DIM ** -0.5 * LOG2_E, name="sb_in_q")
    k, kb = _matmul_heads(hs, w_in, col0=w, name="sb_in_k")
    v, vb = _matmul_heads(hs, w_in, col0=2 * w, name="sb_in_v")
    sz = _matmul(hs, w_in, BF16, tn=tn, col0=3 * w, n=w, act="silu", name="sb_in_z")
    return q, k, v, kb, vb, sz


def kernel(x_prompt, x_sample, cache_sb_k, cache_sb_v, norm_g, final_norm_g, gm_w_in, gm_ln_g, gm_ln_b,
           gm_w_s, gm_b_s, gm_w_out, sb_w_in, sb_w_out):
    bsz, seq, d = x_prompt.shape
    dbsz, dseq, _ = x_sample.shape
    xs = [x_prompt.reshape(bsz * seq, d), x_sample.reshape(dbsz * dseq, d)]

    tt = (jnp.arange(SB_BLOCK)[:, None] >= jnp.arange(SB_BLOCK)[None, :]).astype(BF16)

    gm_v_rows, kp_rows, vp_rows, ks_rows, vs_rows = [], [], [], [], []
    for i in range(DEPTH):
        j = i // N_MIXERS
        if i % N_MIXERS == 0:
            xs, v_new = _gm_layer(xs, (seq, dseq), norm_g[i], gm_w_in[j], gm_ln_g[j], gm_ln_b[j], gm_w_s[j],
                                  gm_b_s[j], gm_w_out[j].astype(BF16))
            gm_v_rows.append(v_new.reshape(dbsz, dseq, GM_WIDTH))
        else:
            q, k, v, kb, vb, sz = _sb_in(xs, norm_g[i], sb_w_in[j])
            shp_p, shp_s = (bsz, seq, SB_WIDTH), (dbsz, dseq, SB_WIDTH)
            yp = _sb_prompt(q[0].reshape(shp_p), kb[0].reshape(shp_p), vb[0].reshape(shp_p),
                            sz[0].reshape(shp_p), tt)
            ys = _sb_decode(q[1].reshape(shp_s), kb[1].reshape(shp_s), vb[1].reshape(shp_s), cache_sb_k[j],
                            cache_sb_v[j], sz[1].reshape(shp_s), tt)
            y = [yp.reshape(bsz * seq, SB_WIDTH), ys.reshape(dbsz * dseq, SB_WIDTH)]
            if i == DEPTH - 1:
                xs = _matmul_res_norm(y, sb_w_out[j].astype(BF16), xs, final_norm_g, tm=512, name="sb_out_norm")
            else:
                xs = _matmul(y, sb_w_out[j], F32, tn=_mm_tn(SB_WIDTH), res=xs, name="sb_out")
            kp_rows.append(k[0].reshape(bsz, seq, SB_HEADS, SB_HEAD_DIM))
            vp_rows.append(v[0].reshape(bsz, seq, SB_HEADS, SB_HEAD_DIM))
            ks_rows.append(k[1].reshape(dbsz, dseq, SB_HEADS, SB_HEAD_DIM))
            vs_rows.append(v[1].reshape(dbsz, dseq, SB_HEADS, SB_HEAD_DIM))

    if DEPTH % N_MIXERS != 0:
        xs = [_rmsnorm(x, final_norm_g, F32, min(x.shape[0], 512)) for x in xs]
    y_prompt, y_sample = xs[0].reshape(bsz, seq, d), xs[1].reshape(dbsz, dseq, d)
    return (y_prompt, y_sample, jnp.stack(kp_rows), jnp.stack(vp_rows), jnp.stack(ks_rows),
            jnp.stack(vs_rows), jnp.stack(gm_v_rows))
```

```python
import functools

import jax
import jax.numpy as jnp
from jax import lax
from jax.experimental import pallas as pl
from jax.experimental.pallas import tpu as pltpu

D_MODEL = 2048
DEPTH = 2
CHUNK = 64
N_MIXERS = 2
GM_WIDTH = 2 * D_MODEL
GM_BLOCK = 128
GM_GROUPS = 16
GM_GROUP_DIM = GM_WIDTH // GM_GROUPS
SB_HEADS = 16
SB_HEAD_DIM = D_MODEL // SB_HEADS
SB_WIDTH = SB_HEADS * SB_HEAD_DIM
NORM_EPS = 1e-6
LN_EPS = 1e-5

LANES = 128
VMEM_LIMIT_BYTES = 56 * 1024 * 1024
MM_VMEM_BUDGET = 40 * 1024 * 1024
F32 = jnp.float32
BF16 = jnp.bfloat16


def _params(n_grid_dims):
    return pltpu.CompilerParams(
        dimension_semantics=("arbitrary",) * n_grid_dims,
        vmem_limit_bytes=VMEM_LIMIT_BYTES,
    )


def _gelu(x):
    c = 0.7978845608028654
    h = 0.5 * x
    return h + h * jnp.tanh(x * (c + (c * 0.044715) * (x * x)))


def _silu(x):
    h = 0.5 * x
    return h + h * jnp.tanh(h)


def _rmsnorm_kernel(x_ref, g_ref, o_ref):
    x = x_ref[...]
    ms = jnp.mean(x * x, axis=-1, keepdims=True)
    o_ref[...] = (x * lax.rsqrt(ms + NORM_EPS) * g_ref[...]).astype(o_ref.dtype)


def _rmsnorm(x, g, out_dtype, tm):
    m, d = x.shape
    return pl.pallas_call(
        _rmsnorm_kernel,
        grid=(m // tm,),
        in_specs=[pl.BlockSpec((tm, d), lambda i: (i, 0)),
                  pl.BlockSpec((1, d), lambda i: (0, 0))],
        out_specs=pl.BlockSpec((tm, d), lambda i: (i, 0)),
        out_shape=jax.ShapeDtypeStruct((m, d), out_dtype),
        compiler_params=_params(1),
        name="rmsnorm",
    )(x, g.reshape(1, d))


MM_TM = 1024
GM_MIX_TM = 512
HEAD_TILE = 8


def _row_groups(ms, tm_max):
    tms = [min(m, tm_max) for m in ms]
    counts = [m // tm for m, tm in zip(ms, tms)]
    assert all(m == tm * c for m, tm, c in zip(ms, tms, counts))
    starts = [sum(counts[a + 1:]) for a in range(len(ms))]
    return tms, counts, starts


def _row_index(i, start, count):
    return jnp.clip(i - start, 0, count - 1)


def _on_group(i, a, counts, starts, fn):
    if len(counts) == 1:
        fn()
    else:
        pl.when(jnp.logical_and(i >= starts[a], i < starts[a] + counts[a]))(fn)


def _mm_kernel(*refs, groups, act, scale, has_res, has_gate):
    n_arr = len(groups[0])
    x_refs, w_ref = refs[:n_arr], refs[n_arr]
    pos = n_arr + 1
    r_refs = refs[pos:pos + n_arr] if has_res else None
    pos += n_arr * has_res
    gate_refs = (refs[pos:pos + n_arr], refs[pos + n_arr:pos + 2 * n_arr]) if has_gate else None
    o_refs = refs[-n_arr:]
    i = pl.program_id(1)

    def tile(a):
        acc = jnp.dot(x_refs[a][...], w_ref[...].astype(BF16), preferred_element_type=F32)
        if act == "gelu":
            acc = _gelu(acc)
        elif act == "silu":
            acc = _silu(acc)
        if scale is not None:
            acc = acc * scale
        if has_res:
            acc = acc + r_refs[a][...]
        acc = acc.astype(o_refs[a].dtype)
        if has_gate:
            acc = (gate_refs[0][a][...] * gate_refs[1][a][...]) * acc
        o_refs[a][...] = acc

    for a in range(n_arr):
        _on_group(i, a, *groups, functools.partial(tile, a))


def _matmul(xs, w, out_dtype, *, tn, col0=0, n=None, act=None, scale=None, res=None, gate=None, name="matmul"):
    k = xs[0].shape[1]
    n = w.shape[1] if n is None else n
    assert col0 % tn == 0 and n % tn == 0
    jb = col0 // tn
    tms, counts, starts = _row_groups([x.shape[0] for x in xs], MM_TM)
    in_specs = [pl.BlockSpec((tm, k), lambda j, i, s=s, c=c: (_row_index(i, s, c), 0))
                for tm, c, s in zip(tms, counts, starts)]
    in_specs.append(pl.BlockSpec((k, tn), lambda j, i: (0, j + jb)))
    row_col = [pl.BlockSpec((tm, tn), lambda j, i, s=s, c=c: (_row_index(i, s, c), j))
               for tm, c, s in zip(tms, counts, starts)]
    args = [*xs, w]
    if res is not None:
        in_specs += row_col
        args += list(res)
    if gate is not None:
        in_specs += row_col + row_col
        args += [*gate[0], *gate[1]]
    return pl.pallas_call(
        functools.partial(_mm_kernel, groups=(counts, starts), act=act, scale=scale, has_res=res is not None,
                          has_gate=gate is not None),
        grid=(n // tn, sum(counts)),
        in_specs=in_specs,
        out_specs=row_col,
        out_shape=[jax.ShapeDtypeStruct((x.shape[0], n), out_dtype) for x in xs],
        compiler_params=_params(2),
        name=name,
    )(*args)


def _mm_heads_kernel(*refs, groups):
    n_arr = len(groups[0])
    x_refs, w_ref = refs[:n_arr], refs[n_arr]
    of_refs, ob_refs = refs[n_arr + 1:2 * n_arr + 1], refs[2 * n_arr + 1:]
    i = pl.program_id(1)

    def tile(a):
        tm = x_refs[a].shape[0]
        acc = jnp.dot(x_refs[a][...], w_ref[...].astype(BF16), preferred_element_type=F32)
        ob_refs[a][...] = acc.astype(ob_refs[a].dtype)
        rows = of_refs[a].reshape(tm * HEAD_TILE, SB_HEAD_DIM)
        for h in range(HEAD_TILE):
            rows[pl.ds(h, tm, stride=HEAD_TILE), :] = acc[:, _head_cols(h)]

    for a in range(n_arr):
        _on_group(i, a, *groups, functools.partial(tile, a))


def _matmul_heads(xs, w, *, col0, name):
    k = xs[0].shape[1]
    tn = HEAD_TILE * SB_HEAD_DIM
    assert col0 % tn == 0
    jb = col0 // tn
    tms, counts, starts = _row_groups([x.shape[0] for x in xs], MM_TM)
    groups = list(zip(tms, counts, starts))
    in_specs = [pl.BlockSpec((tm, k), lambda j, i, s=s, c=c: (_row_index(i, s, c), 0)) for tm, c, s in groups]
    in_specs.append(pl.BlockSpec((k, tn), lambda j, i: (0, j + jb)))
    out_specs = [pl.BlockSpec((tm, HEAD_TILE, SB_HEAD_DIM), lambda j, i, s=s, c=c: (_row_index(i, s, c), j, 0))
                 for tm, c, s in groups]
    out_specs += [pl.BlockSpec((tm, tn), lambda j, i, s=s, c=c: (_row_index(i, s, c), j)) for tm, c, s in groups]
    outs = pl.pallas_call(
        functools.partial(_mm_heads_kernel, groups=(counts, starts)),
        grid=(SB_WIDTH // tn, sum(counts)),
        in_specs=in_specs,
        out_specs=out_specs,
        out_shape=[jax.ShapeDtypeStruct((x.shape[0], SB_HEADS, SB_HEAD_DIM), F32) for x in xs]
        + [jax.ShapeDtypeStruct((x.shape[0], SB_WIDTH), BF16) for x in xs],
        compiler_params=_params(2),
        name=name,
    )(*xs, w)
    return outs[:len(xs)], outs[len(xs):]


def _mm_res_norm_kernel(*refs, groups):
    n_arr = len(groups[0])
    x_refs, w_ref = refs[:n_arr], refs[n_arr]
    r_refs, g_ref, o_refs = refs[n_arr + 1:2 * n_arr + 1], refs[2 * n_arr + 1], refs[2 * n_arr + 2:]
    i = pl.program_id(0)

    def tile(a):
        acc = jnp.dot(x_refs[a][...], w_ref[...], preferred_element_type=F32) + r_refs[a][...]
        ms = jnp.mean(acc * acc, axis=-1, keepdims=True)
        o_refs[a][...] = acc * lax.rsqrt(ms + NORM_EPS) * g_ref[...]

    for a in range(n_arr):
        _on_group(i, a, *groups, functools.partial(tile, a))


def _matmul_res_norm(xs, w, res, g, *, tm, name):
    k, n = w.shape
    tms, counts, starts = _row_groups([x.shape[0] for x in xs], tm)
    row = lambda cols: [pl.BlockSpec((t, cols), lambda i, s=s, c=c: (_row_index(i, s, c), 0))
                        for t, c, s in zip(tms, counts, starts)]
    return pl.pallas_call(
        functools.partial(_mm_res_norm_kernel, groups=(counts, starts)),
        grid=(sum(counts),),
        in_specs=[*row(k), pl.BlockSpec((k, n), lambda i: (0, 0)), *row(n),
                  pl.BlockSpec((1, n), lambda i: (0, 0))],
        out_specs=row(n),
        out_shape=[jax.ShapeDtypeStruct((x.shape[0], n), F32) for x in xs],
        compiler_params=_params(1),
        name=name,
    )(*xs, w, *res, g.reshape(1, n))


def _gm_mix_kernel(gv_ref, wm_ref, bt_ref, lg_ref, lb_ref, *out_refs, emit_vn):
    *out_refs, xc_ref = out_refs
    if emit_vn:
        y_ref, vn_ref = out_refs
    else:
        (y_ref,) = out_refs
    tm = gv_ref.shape[0]
    inv_w = 1.0 / GM_WIDTH
    s1 = jnp.zeros((tm, 1), F32)
    for g in range(GM_GROUPS):
        cols = slice(g * GM_GROUP_DIM, (g + 1) * GM_GROUP_DIM)
        s1 = s1 + jnp.sum(gv_ref[:, cols].astype(F32), axis=-1, keepdims=True)
    mu = s1 * inv_w
    s2 = jnp.zeros((tm, 1), F32)
    for g in range(GM_GROUPS):
        cols = slice(g * GM_GROUP_DIM, (g + 1) * GM_GROUP_DIM)
        xc = gv_ref[:, cols].astype(F32) - mu
        xc_ref[:, cols] = xc
        s2 = s2 + jnp.sum(xc * xc, axis=-1, keepdims=True)
    rstd = lax.rsqrt(s2 * inv_w + LN_EPS)
    for g in range(GM_GROUPS):
        cols = slice(g * GM_GROUP_DIM, (g + 1) * GM_GROUP_DIM)
        vn = xc_ref[:, cols] * rstd * lg_ref[:, cols] + lb_ref[:, cols]
        if emit_vn:
            vn_ref[:, cols] = vn
        vnb = vn.astype(BF16)
        wg = wm_ref[g]
        bias = bt_ref[:, g:g + 1]
        for r in range(tm // GM_BLOCK):
            rows = slice(r * GM_BLOCK, (r + 1) * GM_BLOCK)
            mixed = jnp.dot(wg, vnb[rows], preferred_element_type=F32) + bias
            y_ref[rows, cols] = mixed.astype(y_ref.dtype)


def _gm_mix(ug, wm, bt, ln_g, ln_b, *, tm, emit_vn):
    m = ug.shape[0]
    row_blk = lambda c: pl.BlockSpec((tm, GM_WIDTH), lambda i, c=c: (i, c))
    full = lambda shape: pl.BlockSpec(shape, lambda i: (0,) * len(shape))
    out_shape = [jax.ShapeDtypeStruct((m, GM_WIDTH), BF16)]
    out_specs = [row_blk(0)]
    if emit_vn:
        out_shape.append(jax.ShapeDtypeStruct((m, GM_WIDTH), F32))
        out_specs.append(row_blk(0))
    return pl.pallas_call(
        functools.partial(_gm_mix_kernel, emit_vn=emit_vn),
        grid=(m // tm,),
        in_specs=[row_blk(1), full((GM_GROUPS, GM_BLOCK, GM_BLOCK)), full((GM_BLOCK, GM_GROUPS)),
                  full((1, GM_WIDTH)), full((1, GM_WIDTH))],
        out_specs=out_specs,
        out_shape=out_shape,
        scratch_shapes=[pltpu.VMEM((tm, GM_WIDTH), F32)],
        compiler_params=_params(1),
        name="gm_mix",
    )(ug, wm, bt, ln_g.reshape(1, GM_WIDTH), ln_b.reshape(1, GM_WIDTH))


SB_BLOCK = 256
SB_TQ = 256
SB_HEADS_PER_STEP = 8
LOG2_E = 1.4426950408889634
SB_DEAD_LOG = -151.5


def _sb_chunk(s, strict, carry, tri):
    r = s.shape[0]
    bw = tri.shape[0]
    nb = s.shape[1] // bw
    sbs = [s[:, b * bw:(b + 1) * bw] for b in range(nb)]
    parts = []
    for b in range(nb):
        neg = -sbs[b]
        lf = jnp.minimum(neg, 0.0) - jnp.log2(1.0 + jnp.exp2(jnp.minimum(sbs[b], neg)))
        if strict is not None and b == nb - 1:
            lf = jnp.where(strict, lf, 0.0)
        parts.append(lf.astype(BF16))
    c_all = jnp.dot(jnp.concatenate(parts, axis=0), tri, preferred_element_type=F32)
    ps = [None] * nb
    for b in reversed(range(nb)):
        c = c_all[b * r:(b + 1) * r]
        p = jnp.exp2(sbs[b] + (c + carry))
        if strict is not None and b == nb - 1:
            p = jnp.where(strict, p, 0.0)
        ps[b] = p.astype(BF16)
        carry = carry + c[:, 0:1]
    return ps, carry


def _qk(q, k):
    return lax.dot_general(q, k, (((1,), (1,)), ((), ())), preferred_element_type=F32)


def _pv(ps, v):
    bw = ps[0].shape[1]
    acc = None
    for b, p in enumerate(ps):
        d = jnp.dot(p, v[b * bw:(b + 1) * bw], preferred_element_type=F32)
        acc = d if acc is None else acc + d
    return acc


def _head_cols(h):
    return slice(h * SB_HEAD_DIM, (h + 1) * SB_HEAD_DIM)


def _sb_prompt_kernel(q_ref, k_ref, v_ref, sz_ref, tri_ref, o_ref):
    tq = SB_TQ
    heads = range(SB_HEADS_PER_STEP)
    kb_ref, vb_ref = k_ref.at[0], v_ref.at[0]
    kmax = []
    for h in heads:
        ka = jnp.abs(kb_ref[:, _head_cols(h)].astype(F32))
        kmax.append(jnp.max(jnp.max(ka, axis=0, keepdims=True), axis=1, keepdims=True))

    tri = tri_ref[...]
    row = lax.broadcasted_iota(jnp.int32, (tq, SB_BLOCK), 0)
    col = lax.broadcasted_iota(jnp.int32, (tq, SB_BLOCK), 1)
    strict = col < row
    zero_carry = jnp.zeros((tq, 1), F32)

    def chunk(h, q, k0, width, mask, carry):
        s = _qk(q, kb_ref[pl.ds(k0, width), _head_cols(h)])
        ps, carry = _sb_chunk(s, mask, carry, tri)
        return _pv(ps, vb_ref[pl.ds(k0, width), _head_cols(h)]), carry

    def store(h, rows, acc):
        o_ref[0, rows, _head_cols(h)] = (acc * sz_ref[0, rows, _head_cols(h)].astype(F32)).astype(o_ref.dtype)

    for h in heads:
        acc, _ = chunk(h, q_ref[0, 0:tq, _head_cols(h)], 0, tq, strict, zero_carry)
        store(h, slice(0, tq), acc)

    def q_block(qi, _):
        rows = pl.ds(pl.multiple_of(qi * tq, tq), tq)
        qs = [q_ref[0, rows, _head_cols(h)] for h in heads]
        bounds = []
        for h in heads:
            q_l1 = jnp.sum(jnp.abs(qs[h].astype(F32)), axis=-1, keepdims=True)
            bounds.append(q_l1 * kmax[h] * 1.001)

        def alive(carries):
            worst = carries[0] + bounds[0]
            for h in heads[1:]:
                worst = jnp.maximum(worst, carries[h] + bounds[h])
            return (jnp.max(worst) > SB_DEAD_LOG).astype(jnp.int32)

        k0 = pl.multiple_of((qi - 1) * SB_BLOCK, SB_BLOCK)
        first = [chunk(h, qs[h], k0, SB_BLOCK + tq, strict, zero_carry) for h in heads]
        accs = tuple(f[0] for f in first)
        carries = tuple(f[1] for f in first)

        def cond(st):
            return jnp.logical_and(st[0] >= 0, st[1] > 0)

        def body(st):
            kb_idx, _, carries, accs = st
            k0 = pl.multiple_of(kb_idx * SB_BLOCK, SB_BLOCK)
            nxt = [chunk(h, qs[h], k0, SB_BLOCK, None, carries[h]) for h in heads]
            carries = tuple(n[1] for n in nxt)
            accs = tuple(a + n[0] for a, n in zip(accs, nxt))
            return kb_idx - 1, alive(carries), carries, accs

        st = lax.while_loop(cond, body, (qi - 2, alive(carries), carries, accs))
        for h in heads:
            store(h, rows, st[3][h])
        return 0

    lax.fori_loop(1, q_ref.shape[1] // tq, q_block, 0)


def _sb_prompt(q, k, v, sz, tri):
    b, l, _ = q.shape
    assert SB_TQ == SB_BLOCK and l % SB_TQ == 0
    gw = SB_HEADS_PER_STEP * SB_HEAD_DIM
    blk = pl.BlockSpec((1, l, gw), lambda bi, g: (bi, 0, g))
    return pl.pallas_call(
        _sb_prompt_kernel,
        grid=(b, SB_HEADS // SB_HEADS_PER_STEP),
        in_specs=[blk, blk, blk, blk, pl.BlockSpec((SB_BLOCK, SB_BLOCK), lambda bi, g: (0, 0))],
        out_specs=blk,
        out_shape=jax.ShapeDtypeStruct(q.shape, BF16),
        compiler_params=_params(2),
        name="sb_prompt",
    )(q, k, v, sz, tri)


def _sb_decode_kernel(q_ref, kn_ref, vn_ref, sz_ref, tri_ref, kc_ref, vc_ref, o_ref):
    past, nh, _ = kc_ref.shape
    kc_ref = kc_ref.reshape(past * nh, SB_HEAD_DIM)
    vc_ref = vc_ref.reshape(past * nh, SB_HEAD_DIM)
    tq = q_ref.shape[1]
    tri = tri_ref[...]
    row = lax.broadcasted_iota(jnp.int32, (tq, LANES), 0)
    col = lax.broadcasted_iota(jnp.int32, (tq, LANES), 1)
    pad = jnp.zeros((LANES - tq, SB_HEAD_DIM), BF16)
    for h in range(nh):
        cols = _head_cols(h)
        q = q_ref[0, :, cols]
        kn = jnp.concatenate([kn_ref[0, :, cols], pad], axis=0)
        vn = jnp.concatenate([vn_ref[0, :, cols], pad], axis=0)
        ps, carry = _sb_chunk(_qk(q, kn), col < row, jnp.zeros((tq, 1), F32), tri[:LANES, :LANES])
        acc = _pv(ps, vn)
        head_rows = pl.ds(h, past, stride=nh)
        s_t = _qk(kc_ref[head_rows, :].astype(BF16), jnp.concatenate([q, pad], axis=0))
        ps, _ = _sb_chunk(s_t.T[:tq], None, carry, tri)
        acc = acc + _pv(ps, vc_ref[head_rows, :].astype(BF16))
        o_ref[0, :, cols] = (acc * sz_ref[0, :, cols].astype(F32)).astype(o_ref.dtype)


def _sb_decode(q, k_new, v_new, k_cache, v_cache, sz, tri):
    b, t, _ = q.shape
    _, p, nh, dh = k_cache.shape
    assert nh == SB_HEADS and dh == SB_HEAD_DIM
    g = HEAD_TILE
    blk = pl.BlockSpec((1, t, g * dh), lambda bi, gi: (bi, 0, gi))
    blk_cache = pl.BlockSpec((None, p, g, dh), lambda bi, gi: (bi, 0, gi, 0))
    return pl.pallas_call(
        _sb_decode_kernel,
        grid=(b, nh // g),
        in_specs=[blk, blk, blk, blk, pl.BlockSpec((SB_BLOCK, SB_BLOCK), lambda bi, gi: (0, 0)),
                  blk_cache, blk_cache],
        out_specs=blk,
        out_shape=jax.ShapeDtypeStruct(q.shape, BF16),
        compiler_params=_params(2),
        name="sb_decode",
    )(q, k_new, v_new, sz, tri, k_cache, v_cache)


def _chunk_causal(n):
    pos = jnp.arange(n)
    return (pos[None, :] // CHUNK) <= (pos[:, None] // CHUNK)


def _mm_tn(k, w_itemsize=4, out_bytes=8):
    for tn in (1024, 512, 256):
        if 2 * (MM_TM * k * 2 + k * tn * w_itemsize + MM_TM * tn * out_bytes) <= MM_VMEM_BUDGET:
            return tn
    raise ValueError("matmul blocks do not fit VMEM")


def _gm_mix_params(seq_len, w_s, b_s):
    blk = min(seq_len, GM_BLOCK)
    w = w_s[:, :blk, :blk] * _chunk_causal(blk).astype(w_s.dtype)
    reps = GM_BLOCK // blk
    wm = jnp.einsum("ab,gts->gatbs", jnp.eye(reps, dtype=w.dtype), w).reshape(
        GM_GROUPS, GM_BLOCK, GM_BLOCK).astype(BF16)
    return wm, jnp.tile(b_s[:, :blk].T, (reps, 1))


def _gm_layer(xs, seq_lens, norm_g, w_in, ln_g, ln_b, w_s, b_s, w_out):
    d = xs[0].shape[1]
    hs = [_rmsnorm(x, norm_g, BF16, min(x.shape[0], 1024)) for x in xs]
    tn = _mm_tn(d, out_bytes=2)
    ugs = _matmul(hs, w_in, BF16, tn=tn, n=2 * GM_WIDTH, act="gelu", name="gm_in_uv")
    mixed, v_rows = [], None
    for g, (ug, seq_len) in enumerate(zip(ugs, seq_lens)):
        emit_vn = g == len(xs) - 1
        outs = _gm_mix(ug, *_gm_mix_params(seq_len, w_s, b_s), ln_g, ln_b,
                       tm=min(ug.shape[0], GM_MIX_TM), emit_vn=emit_vn)
        mixed.append(outs[0])
        v_rows = outs[1] if emit_vn else v_rows
    ys = _matmul(hs, w_in, BF16, tn=tn, col0=2 * GM_WIDTH, n=GM_WIDTH, act="silu", gate=(ugs, mixed),
                 name="gm_in_z")
    x_new = _matmul(ys, w_out, F32, tn=_mm_tn(GM_WIDTH, w_out.dtype.itemsize), res=xs, name="gm_out")
    return x_new, v_rows


def _sb_in(xs, norm_g, w_in):
    d = xs[0].shape[1]
    hs = [_rmsnorm(x, norm_g, BF16, min(x.shape[0], 1024)) for x in xs]
    tn = _mm_tn(d, out_bytes=2)
    w = SB_WIDTH
    q = _matmul(hs, w_in, BF16, tn=tn, n=w, scale=SB_HEAD_DIM ** -0.5 * LOG2_E, name="sb_in_q")
    k, kb = _matmul_heads(hs, w_in, col0=w, name="sb_in_k")
    v, vb = _matmul_heads(hs, w_in, col0=2 * w, name="sb_in_v")
    sz = _matmul(hs, w_in, BF16, tn=tn, col0=3 * w, n=w, act="silu", name="sb_in_z")
    return q, k, v, kb, vb, sz


def kernel(x_prompt, x_sample, cache_sb_k, cache_sb_v, norm_g, final_norm_g, gm_w_in, gm_ln_g, gm_ln_b,
           gm_w_s, gm_b_s, gm_w_out, sb_w_in, sb_w_out):
    bsz, seq, d = x_prompt.shape
    dbsz, dseq, _ = x_sample.shape
    xs = [x_prompt.reshape(bsz * seq, d), x_sample.reshape(dbsz * dseq, d)]

    tt = (jnp.arange(SB_BLOCK)[:, None] >= jnp.arange(SB_BLOCK)[None, :]).astype(BF16)

    gm_v_rows, kp_rows, vp_rows, ks_rows, vs_rows = [], [], [], [], []
    for i in range(DEPTH):
        j = i // N_MIXERS
        if i % N_MIXERS == 0:
            xs, v_new = _gm_layer(xs, (seq, dseq), norm_g[i], gm_w_in[j], gm_ln_g[j], gm_ln_b[j], gm_w_s[j],
                                  gm_b_s[j], gm_w_out[j])
            gm_v_rows.append(v_new.reshape(dbsz, dseq, GM_WIDTH))
        else:
            q, k, v, kb, vb, sz = _sb_in(xs, norm_g[i], sb_w_in[j])
            shp_p, shp_s = (bsz, seq, SB_WIDTH), (dbsz, dseq, SB_WIDTH)
            yp = _sb_prompt(q[0].reshape(shp_p), kb[0].reshape(shp_p), vb[0].reshape(shp_p),
                            sz[0].reshape(shp_p), tt)
            ys = _sb_decode(q[1].reshape(shp_s), kb[1].reshape(shp_s), vb[1].reshape(shp_s), cache_sb_k[j],
                            cache_sb_v[j], sz[1].reshape(shp_s), tt)
            y = [yp.reshape(bsz * seq, SB_WIDTH), ys.reshape(dbsz * dseq, SB_WIDTH)]
            if i == DEPTH - 1:
                xs = _matmul_res_norm(y, sb_w_out[j].astype(BF16), xs, final_norm_g, tm=512, name="sb_out_norm")
            else:
                xs = _matmul(y, sb_w_out[j], F32, tn=_mm_tn(SB_WIDTH), res=xs, name="sb_out")
            kp_rows.append(k[0].reshape(bsz, seq, SB_HEADS, SB_HEAD_DIM))
            vp_rows.append(v[0].reshape(bsz, seq, SB_HEADS, SB_HEAD_DIM))
            ks_rows.append(k[1].reshape(dbsz, dseq, SB_HEADS, SB_HEAD_DIM))
            vs_rows.append(v[1].reshape(dbsz, dseq, SB_HEADS, SB_HEAD_DIM))

    if DEPTH % N_MIXERS != 0:
        xs = [_rmsnorm(x, final_norm_g, F32, min(x.shape[0], 512)) for x in xs]
    y_prompt, y_sample = xs[0].reshape(bsz, seq, d), xs[1].reshape(dbsz, dseq, d)
    return (y_prompt, y_sample, jnp.stack(kp_rows), jnp.stack(vp_rows), jnp.stack(ks_rows),
            jnp.stack(vs_rows), jnp.stack(gm_v_rows))
```
